```python
import math
import jax, jax.numpy as jnp
from jax import lax
import numpy as np

D_MODEL = 2048
BATCH = 4
SEQ = 4096
DEPTH = 2

N_A = DEPTH // 2
N_B = DEPTH - N_A
H_A = 16
HD_A = D_MODEL // H_A
QBLOCK = 128
HD_B = 64
H_B = D_MODEL // HD_B
KVH_B = H_B // 8
G_B = H_B // KVH_B
WINDOW = 128
ROPE_THETA = 10000.0
N_EXPERTS = 64
TOP_K = 8
N_GROUPS = 8
TOPK_GROUPS = 4
EXPERT_HIDDEN = 512
SHARED_HIDDEN = 512
ROUTED_SCALE = 2.5
MOE_BLOCK = 128
ALPHA = float((2 * DEPTH) ** 0.25)
BETA = float((8 * DEPTH) ** -0.25)
LN_EPS = 1e-5

kernel_name = "yoco_fox_swa_sink_moe_deepnorm"


def layer_norm(x, g, b):
    xf = x.astype(jnp.float32)
    mu = jnp.mean(xf, axis=-1, keepdims=True)
    var = jnp.mean(jnp.square(xf - mu), axis=-1, keepdims=True)
    y = (xf - mu) * lax.rsqrt(var + LN_EPS) * g.astype(jnp.float32) + b.astype(jnp.float32)
    return y.astype(x.dtype)


def rope(x, pos):
    hd = x.shape[-1]
    half = hd // 2
    inv = 1.0 / (ROPE_THETA ** (jnp.arange(half, dtype=jnp.float32) / half))
    ang = pos[:, None] * inv[None, :]
    cos = jnp.cos(ang)[None, :, None, :]
    sin = jnp.sin(ang)[None, :, None, :]
    xf = x.astype(jnp.float32)
    x1, x2 = xf[..., :half], xf[..., half:]
    out = jnp.concatenate([x1 * cos - x2 * sin, x2 * cos + x1 * sin], axis=-1)
    return out.astype(x.dtype)


def fox_attention(q, k, v, cum):
    B, S, H, hd = q.shape
    nb = S // QBLOCK
    scale = 1.0 / math.sqrt(hd)
    qb = q.reshape(B, nb, QBLOCK, H, hd).transpose(1, 0, 2, 3, 4)
    cb = cum.reshape(B, H, nb, QBLOCK).transpose(2, 0, 1, 3)
    key_pos = jnp.arange(S)

    def block(args):
        qi, ci, i = args
        s = jnp.einsum('bqhd,bkhd->bhqk', qi, k, preferred_element_type=jnp.float32) * scale
        s = s + ci[..., :, None] - cum[:, :, None, :]
        qpos = i * QBLOCK + jnp.arange(QBLOCK)
        mask = key_pos[None, :] <= qpos[:, None]
        s = jnp.where(mask[None, None], s, -jnp.inf)
        p = jax.nn.softmax(s, axis=-1)
        return jnp.einsum('bhqk,bkhd->bqhd', p.astype(v.dtype), v)

    out = lax.map(block, (qb, cb, jnp.arange(nb)))
    return out.transpose(1, 0, 2, 3, 4).reshape(B, S, H * hd)


def swa_attention(q, k, v, sinks):
    B, S, H, hd = q.shape
    nb = S // WINDOW
    scale = 1.0 / math.sqrt(hd)
    kp = jnp.pad(k, ((0, 0), (WINDOW, 0), (0, 0), (0, 0)))
    vp = jnp.pad(v, ((0, 0), (WINDOW, 0), (0, 0), (0, 0)))
    qg = q.reshape(B, nb, WINDOW, KVH_B, G_B, hd).transpose(1, 0, 2, 3, 4, 5)
    sink = sinks.astype(jnp.float32).reshape(KVH_B, G_B)[None, :, :, None, None]

    def block(args):
        qi, i = args
        ki = lax.dynamic_slice_in_dim(kp, i * WINDOW, 2 * WINDOW, axis=1)
        vi = lax.dynamic_slice_in_dim(vp, i * WINDOW, 2 * WINDOW, axis=1)
        s = jnp.einsum('bqkgd,bskd->bkgqs', qi, ki, preferred_element_type=jnp.float32) * scale
        qpos = i * WINDOW + jnp.arange(WINDOW)
        kpos = i * WINDOW - WINDOW + jnp.arange(2 * WINDOW)
        diff = qpos[:, None] - kpos[None, :]
        mask = (diff >= 0) & (diff < WINDOW) & (kpos[None, :] >= 0)
        s = jnp.where(mask, s, -jnp.inf)
        m = jnp.maximum(jnp.max(s, axis=-1, keepdims=True), sink)
        p = jnp.exp(s - m)
        p = p / (jnp.sum(p, axis=-1, keepdims=True) + jnp.exp(sink - m))
        return jnp.einsum('bkgqs,bskd->bqkgd', p.astype(vi.dtype), vi)

    out = lax.map(block, (qg, jnp.arange(nb)))
    return out.transpose(1, 0, 2, 3, 4, 5).reshape(B, S, H * hd)


def fox_layer(x, w_in, b_f, w_o):
    B, S, D = x.shape
    proj = x @ w_in
    q = proj[..., :D].reshape(B, S, H_A, HD_A)
    k = proj[..., D:2 * D].reshape(B, S, H_A, HD_A)
    v = proj[..., 2 * D:3 * D].reshape(B, S, H_A, HD_A)
    f = proj[..., 3 * D:]
    logf = jax.nn.log_sigmoid((f + b_f).astype(jnp.float32))
    cum = jnp.cumsum(logf, axis=1).transpose(0, 2, 1)
    return fox_attention(q, k, v, cum) @ w_o


def swa_layer(x, k_sh, v_sh, w_q, sinks, w_o, pos):
    B, S, D = x.shape
    q = rope((x @ w_q).reshape(B, S, H_B, HD_B), pos)
    return swa_attention(q, k_sh, v_sh, sinks) @ w_o


def moe(x, router_w, router_bias, w_gu, w_down, s_gu, s_down):
    B, S, D = x.shape
    xt = x.reshape(-1, D)
    T = xt.shape[0]
    logits = jnp.einsum('td,de->te', xt, router_w, preferred_element_type=jnp.float32)
    scores = jax.nn.sigmoid(logits)
    choice = scores + router_bias.astype(jnp.float32)
    grp = choice.reshape(T, N_GROUPS, N_EXPERTS // N_GROUPS)
    gscore = jnp.sum(lax.top_k(grp, 2)[0], axis=-1)
    _, gidx = lax.top_k(gscore, TOPK_GROUPS)
    gmask = jnp.sum(jax.nn.one_hot(gidx, N_GROUPS, dtype=jnp.int32), axis=1) > 0
    emask = jnp.repeat(gmask, N_EXPERTS // N_GROUPS, axis=1)
    _, eidx = lax.top_k(jnp.where(emask, choice, -jnp.inf), TOP_K)
    wts = jnp.take_along_axis(scores, eidx, axis=1)
    wts = wts / jnp.sum(wts, axis=-1, keepdims=True) * ROUTED_SCALE
    M = T * TOP_K
    M_PAD = M + N_EXPERTS * MOE_BLOCK
    NB = M_PAD // MOE_BLOCK
    flat_e = eidx.reshape(-1)
    flat_tok = jnp.repeat(jnp.arange(T, dtype=jnp.int32), TOP_K)
    flat_w = wts.reshape(-1).astype(xt.dtype)
    order = jnp.argsort(flat_e)
    sorted_e = flat_e[order]
    counts = jnp.bincount(flat_e, length=N_EXPERTS)
    padded = ((counts + MOE_BLOCK - 1) // MOE_BLOCK) * MOE_BLOCK
    ends_pad = jnp.cumsum(padded)
    starts_pad = ends_pad - padded
    starts = jnp.cumsum(counts) - counts
    dest = starts_pad[sorted_e] + (jnp.arange(M) - starts[sorted_e])
    tok_pad = jnp.zeros((M_PAD,), jnp.int32).at[dest].set(flat_tok[order])
    gate_pad = jnp.zeros((M_PAD,), xt.dtype).at[dest].set(flat_w[order])
    block_e = jnp.clip(jnp.searchsorted(ends_pad, jnp.arange(NB) * MOE_BLOCK, side='right'),
                       0, N_EXPERTS - 1)

    def body(acc, blk):
        tok, g, e = blk
        h = xt[tok]
        gu = h @ w_gu[e]
        a = jax.nn.silu(gu[:, :EXPERT_HIDDEN]) * gu[:, EXPERT_HIDDEN:]
        y = (a @ w_down[e]) * g[:, None]
        return acc.at[tok].add(y), None

    routed, _ = lax.scan(body, jnp.zeros_like(xt),
                         (tok_pad.reshape(NB, MOE_BLOCK), gate_pad.reshape(NB, MOE_BLOCK), block_e))
    sg = xt @ s_gu
    shared = (jax.nn.silu(sg[:, :SHARED_HIDDEN]) * sg[:, SHARED_HIDDEN:]) @ s_down
    return (routed + shared).reshape(B, S, D)


def setup_inputs(seed: int = 0) -> dict:
    key = jax.random.key(seed)
    ks = jax.random.split(key, 20)
    D = D_MODEL
    f32 = jnp.float32
    nrm = lambda k, shp, s: jax.random.normal(k, shp, f32) * s
    return {
        "x": jax.random.normal(ks[0], (BATCH, SEQ, D), f32),
        "a_w_in": nrm(ks[1], (N_A, D, 3 * D + H_A), D ** -0.5),
        "a_b_f": jax.random.uniform(ks[2], (N_A, H_A), f32, 1.0, 4.0),
        "a_w_o": nrm(ks[3], (N_A, D, D), D ** -0.5 * BETA),
        "kv_w": nrm(ks[4], (D, 2 * KVH_B * HD_B), D ** -0.5),
        "b_w_q": nrm(ks[5], (N_B, D, H_B * HD_B), D ** -0.5),
        "b_sinks": nrm(ks[6], (N_B, H_B), 0.5),
        "b_w_o": nrm(ks[7], (N_B, H_B * HD_B, D), (H_B * HD_B) ** -0.5 * BETA),
        "router_w": nrm(ks[8], (DEPTH, D, N_EXPERTS), D ** -0.5),
        "router_bias": nrm(ks[9], (DEPTH, N_EXPERTS), 0.01),
        "moe_w_gu": nrm(ks[10], (DEPTH, N_EXPERTS, D, 2 * EXPERT_HIDDEN), D ** -0.5),
        "moe_w_down": nrm(ks[11], (DEPTH, N_EXPERTS, EXPERT_HIDDEN, D), EXPERT_HIDDEN ** -0.5 * BETA),
        "shared_w_gu": nrm(ks[12], (DEPTH, D, 2 * SHARED_HIDDEN), D ** -0.5),
        "shared_w_down": nrm(ks[13], (DEPTH, SHARED_HIDDEN, D), SHARED_HIDDEN ** -0.5 * BETA),
        "ln1_g": 1.0 + nrm(ks[14], (DEPTH, D), 0.02),
        "ln1_b": nrm(ks[15], (DEPTH, D), 0.02),
        "ln2_g": 1.0 + nrm(ks[16], (DEPTH, D), 0.02),
        "ln2_b": nrm(ks[17], (DEPTH, D), 0.02),
    }


def reference(x, a_w_in, a_b_f, a_w_o, kv_w, b_w_q, b_sinks, b_w_o, router_w, router_bias,
              moe_w_gu, moe_w_down, shared_w_gu, shared_w_down, ln1_g, ln1_b, ln2_g, ln2_b):
    B, S, D = x.shape
    pos = jnp.arange(S, dtype=jnp.float32)
    k_sh = None
    v_sh = None
    for i in range(DEPTH):
        if i < N_A:
            h = fox_layer(x, a_w_in[i], a_b_f[i], a_w_o[i])
        else:
            j = i - N_A
            h = swa_layer(x, k_sh, v_sh, b_w_q[j], b_sinks[j], b_w_o[j], pos)
        x = layer_norm(ALPHA * x + h, ln1_g[i], ln1_b[i])
        x = layer_norm(ALPHA * x + moe(x, router_w[i], router_bias[i], moe_w_gu[i], moe_w_down[i],
                                       shared_w_gu[i], shared_w_down[i]), ln2_g[i], ln2_b[i])
        if i == N_A - 1:
            kv = x @ kv_w
            k_sh = rope(kv[..., :KVH_B * HD_B].reshape(B, S, KVH_B, HD_B), pos)
            v_sh = kv[..., KVH_B * HD_B:].reshape(B, S, KVH_B, HD_B)
    return x
```

```python
import functools
import math

import jax
import jax.numpy as jnp
from jax import lax
from jax.experimental import pallas as pl
from jax.experimental.pallas import tpu as pltpu

F32 = jnp.float32
BF16 = jnp.bfloat16
U32 = jnp.uint32
I32 = jnp.int32

LANES = 128
HD_B = 64
WINDOW = 128
ROPE_THETA = 10000.0
TOP_K = 8
N_GROUPS = 8
TOPK_GROUPS = 4
ROUTED_SCALE = 2.5
LN_EPS = 1e-5
VMEM_LIMIT_BYTES = 56 * 1024 * 1024
HI_MASK = 0xFFFF0000


def _cparams(*sem):
    return pltpu.CompilerParams(dimension_semantics=sem, vmem_limit_bytes=VMEM_LIMIT_BYTES)


def _tile(dim, pref):
    t = min(dim, pref)
    assert dim % t == 0, (dim, pref)
    return t


def _silu(x):
    return x * jax.nn.sigmoid(x)


def _layer_norm(z, g, b):
    mu = jnp.mean(z, axis=-1, keepdims=True)
    zc = z - mu
    var = jnp.mean(zc * zc, axis=-1, keepdims=True)
    return zc * lax.rsqrt(var + LN_EPS) * g + b


def _pack_rows(y):
    half = y.shape[1] // 2
    yb = y.astype(BF16).astype(F32)
    lo = lax.bitcast_convert_type(yb[:, :half], U32)
    hi = lax.bitcast_convert_type(yb[:, half:], U32)
    return (hi & jnp.uint32(HI_MASK)) | (lo >> jnp.uint32(16))


def _unpack_rows(pk):
    lo = lax.bitcast_convert_type(pk << jnp.uint32(16), F32).astype(BF16)
    hi = lax.bitcast_convert_type(pk & jnp.uint32(HI_MASK), F32).astype(BF16)
    return lo, hi


def _proj_kernel(x_ref, w_ref, o_ref, *, scaled_tiles, scale):
    j = pl.program_id(0)
    y = jnp.dot(x_ref[...].astype(BF16), w_ref[...], preferred_element_type=F32)
    s = jnp.where(j < scaled_tiles, jnp.float32(scale), jnp.float32(1.0))
    o_ref[...] = (y * s).astype(o_ref.dtype)


def _proj(x2d, w_bf, *, scaled_cols, scale, tm=512, tn=2048):
    t, d = x2d.shape
    n = w_bf.shape[1]
    tm, tn = _tile(t, tm), _tile(n, tn)
    assert scaled_cols % tn == 0
    return pl.pallas_call(
        functools.partial(_proj_kernel, scaled_tiles=scaled_cols // tn, scale=scale),
        grid=(n // tn, t // tm),
        in_specs=[pl.BlockSpec((tm, d), lambda j, i: (i, 0)),
                  pl.BlockSpec((d, tn), lambda j, i: (0, j))],
        out_specs=pl.BlockSpec((tm, tn), lambda j, i: (i, j)),
        out_shape=jax.ShapeDtypeStruct((t, n), BF16),
        compiler_params=_cparams("arbitrary", "arbitrary"),
    )(x2d, w_bf)


def _fox_cum_kernel(x_ref, w_ref, b_ref, o_ref, carry_ref):
    @pl.when(pl.program_id(1) == 0)
    def _():
        carry_ref[...] = jnp.zeros_like(carry_ref)

    ts = x_ref.shape[0]
    z = jnp.dot(x_ref[...].astype(BF16), w_ref[...], preferred_element_type=F32) + b_ref[...]
    logf = -(jnp.maximum(-z, 0.0) + jnp.log1p(jnp.exp(-jnp.abs(z))))
    row = lax.broadcasted_iota(I32, (ts, ts), 0)
    col = lax.broadcasted_iota(I32, (ts, ts), 1)
    tril = (row >= col).astype(F32)
    cum = jnp.dot(tril, logf, preferred_element_type=F32, precision=lax.Precision.HIGHEST) + carry_ref[...]
    o_ref[...] = cum
    carry_ref[...] = cum[ts - 1:ts, :]


def _fox_cum(x2d, w_f, b_f, batch, seq, ts=512):
    t, d = x2d.shape
    h = w_f.shape[1]
    assert h <= LANES
    ts = _tile(seq, ts)
    w_pad = jnp.zeros((d, LANES), BF16).at[:, :h].set(w_f.astype(BF16))
    b_pad = jnp.zeros((1, LANES), F32).at[0, :h].set(b_f.astype(F32))
    ns = seq // ts
    return pl.pallas_call(
        _fox_cum_kernel,
        grid=(batch, ns),
        in_specs=[pl.BlockSpec((ts, d), lambda b, s: (b * ns + s, 0)),
                  pl.BlockSpec((d, LANES), lambda b, s: (0, 0)),
                  pl.BlockSpec((1, LANES), lambda b, s: (0, 0))],
        out_specs=pl.BlockSpec((ts, LANES), lambda b, s: (b * ns + s, 0)),
        out_shape=jax.ShapeDtypeStruct((t, LANES), F32),
        scratch_shapes=[pltpu.VMEM((1, LANES), F32)],
        compiler_params=_cparams("arbitrary", "arbitrary"),
    )(x2d, w_pad, b_pad)


def _fox_attn_kernel(q_ref, k_ref, v_ref, c_ref, o_ref, *, tb):
    qi = pl.program_id(2)
    q = q_ref[...]
    hd = q.shape[1]

    def scores(j):
        k = k_ref[pl.ds(pl.multiple_of(j * tb, tb), tb), :]
        s = lax.dot_general(q, k, (((1,), (1,)), ((), ())), preferred_element_type=F32)
        return s - c_ref[j]

    def update(j, t, carry):
        m, l, acc = carry
        v = v_ref[pl.ds(pl.multiple_of(j * tb, tb), tb), :]
        m_new = jnp.maximum(m, jnp.max(t, axis=-1, keepdims=True))
        p = jnp.exp(t - m_new)
        alpha = jnp.exp(m - m_new)
        l = alpha * l + jnp.sum(p, axis=-1, keepdims=True)
        acc = alpha * acc + jnp.dot(p.astype(BF16), v, preferred_element_type=F32)
        return m_new, l, acc

    init = (jnp.full((tb, 1), -jnp.inf, F32), jnp.zeros((tb, 1), F32), jnp.zeros((tb, hd), F32))
    carry = lax.fori_loop(0, qi, lambda j, c: update(j, scores(j), c), init)
    row = lax.broadcasted_iota(I32, (tb, tb), 0)
    col = lax.broadcasted_iota(I32, (tb, tb), 1)
    t = jnp.where(row >= col, scores(qi), -jnp.inf)
    _, l, acc = update(qi, t, carry)
    o_ref[...] = (acc / l).astype(o_ref.dtype)


def _fox_attn(qkv, cum_t, batch, seq, heads, hd, tb=512):
    t = qkv.shape[0]
    d = heads * hd
    tb = _tile(seq, tb)
    nq = seq // tb
    assert hd % LANES == 0
    return pl.pallas_call(
        functools.partial(_fox_attn_kernel, tb=tb),
        grid=(batch, heads, nq),
        in_specs=[pl.BlockSpec((tb, hd), lambda b, h, i: (b * nq + i, h)),
                  pl.BlockSpec((seq, hd), lambda b, h, i: (b, heads + h)),
                  pl.BlockSpec((seq, hd), lambda b, h, i: (b, 2 * heads + h)),
                  pl.BlockSpec((None, None, nq, 1, tb), lambda b, h, i: (b, h, 0, 0, 0))],
        out_specs=pl.BlockSpec((tb, hd), lambda b, h, i: (b * nq + i, h)),
        out_shape=jax.ShapeDtypeStruct((t, d), BF16),
        compiler_params=_cparams("arbitrary", "arbitrary", "arbitrary"),
    )(qkv, qkv, qkv, cum_t)


def _oproj_ln_kernel(a_ref, w_ref, x_ref, g_ref, b_ref, o_ref, pk_ref, acc_ref, *, alpha):
    k = pl.program_id(1)

    @pl.when(k == 0)
    def _():
        acc_ref[...] = jnp.zeros_like(acc_ref)

    acc_ref[...] += jnp.dot(a_ref[...], w_ref[...], preferred_element_type=F32)

    @pl.when(k == pl.num_programs(1) - 1)
    def _():
        y = _layer_norm(alpha * x_ref[...] + acc_ref[...], g_ref[...], b_ref[...])
        o_ref[...] = y
        pk_ref[...] = _pack_rows(y)


def _oproj_ln(a_bf, w_bf, x2d, g, b, alpha, tm=512, tk=1024):
    t, d = x2d.shape
    kk = a_bf.shape[1]
    tm, tk = _tile(t, tm), _tile(kk, tk)
    return pl.pallas_call(
        functools.partial(_oproj_ln_kernel, alpha=alpha),
        grid=(t // tm, kk // tk),
        in_specs=[pl.BlockSpec((tm, tk), lambda i, k: (i, k)),
                  pl.BlockSpec((tk, d), lambda i, k: (k, 0)),
                  pl.BlockSpec((tm, d), lambda i, k: (i, 0)),
                  pl.BlockSpec((1, d), lambda i, k: (0, 0)),
                  pl.BlockSpec((1, d), lambda i, k: (0, 0))],
        out_specs=[pl.BlockSpec((tm, d), lambda i, k: (i, 0)),
                   pl.BlockSpec((tm, d // 2), lambda i, k: (i, 0))],
        out_shape=[jax.ShapeDtypeStruct((t, d), F32), jax.ShapeDtypeStruct((t, d // 2), U32)],
        scratch_shapes=[pltpu.VMEM((tm, d), F32)],
        compiler_params=_cparams("arbitrary", "arbitrary"),
    )(a_bf, w_bf, x2d, g.reshape(1, d).astype(F32), b.reshape(1, d).astype(F32))


def _first_max(vals, idx, sentinel):
    m = jnp.max(vals, axis=0, keepdims=True)
    first = jnp.min(jnp.where(vals == m, idx, sentinel), axis=0, keepdims=True)
    return m, first


def _router_kernel(x_ref, rwt_ref, bias_ref, eidx_ref, wts_ref):
    e = rwt_ref.shape[0]
    tm = x_ref.shape[0]
    per = e // N_GROUPS
    logits = lax.dot_general(rwt_ref[...], x_ref[...], (((1,), (1,)), ((), ())),
                             preferred_element_type=F32, precision=lax.Precision.HIGHEST)
    scores = jax.nn.sigmoid(logits)
    choice = scores + bias_ref[...]
    neg = jnp.float32(-jnp.inf)

    ip = lax.broadcasted_iota(I32, (per, tm), 0)
    gs = []
    for g in range(N_GROUPS):
        c = choice[g * per:(g + 1) * per, :]
        m1, f1 = _first_max(c, ip, per)
        m2 = jnp.max(jnp.where(ip == f1, neg, c), axis=0, keepdims=True)
        gs.append(m1 + m2)
    gscore = jnp.concatenate(gs, axis=0)

    ig = lax.broadcasted_iota(I32, (N_GROUPS, tm), 0)
    sel = jnp.zeros((N_GROUPS, tm), F32)
    for _ in range(TOPK_GROUPS):
        _, fg = _first_max(gscore, ig, N_GROUPS)
        hit = ig == fg
        sel = jnp.where(hit, 1.0, sel)
        gscore = jnp.where(hit, neg, gscore)

    masked = jnp.concatenate(
        [jnp.where(sel[g:g + 1, :] > 0.0, choice[g * per:(g + 1) * per, :], neg) for g in range(N_GROUPS)],
        axis=0)
    ie = lax.broadcasted_iota(I32, (e, tm), 0)
    idxs, ws = [], []
    for _ in range(TOP_K):
        _, fe = _first_max(masked, ie, e)
        hit = ie == fe
        idxs.append(fe)
        ws.append(jnp.sum(jnp.where(hit, scores, 0.0), axis=0, keepdims=True))
        masked = jnp.where(hit, neg, masked)
    w = jnp.concatenate(ws, axis=0)
    eidx_ref[...] = jnp.concatenate(idxs, axis=0)
    wts_ref[...] = w / jnp.sum(w, axis=0, keepdims=True) * ROUTED_SCALE


def _router(x2d, router_w, router_bias, tm=512):
    t, d = x2d.shape
    e = router_w.shape[1]
    assert e % N_GROUPS == 0 and (e // N_GROUPS) % 8 == 0
    tm = _tile(t, tm)
    return pl.pallas_call(
        _router_kernel,
        grid=(t // tm,),
        in_specs=[pl.BlockSpec((tm, d), lambda i: (i, 0)),
                  pl.BlockSpec((e, d), lambda i: (0, 0)),
                  pl.BlockSpec((e, 1), lambda i: (0, 0))],
        out_specs=[pl.BlockSpec((TOP_K, tm), lambda i: (0, i)),
                   pl.BlockSpec((TOP_K, tm), lambda i: (0, i))],
        out_shape=[jax.ShapeDtypeStruct((TOP_K, t), I32), jax.ShapeDtypeStruct((TOP_K, t), F32)],
        compiler_params=_cparams("arbitrary"),
    )(x2d, router_w.T.astype(F32), router_bias.reshape(e, 1).astype(F32))


def _dispatch_kernel(idx_ref, x_hbm, z_hbm, o_hbm, sem, *, zero_steps, chunk):
    s = pl.program_id(0)
    tokens = chunk // TOP_K

    def wait_all():
        def body(r, c):
            pltpu.make_async_copy(z_hbm.at[pl.ds(0, 1)], o_hbm.at[pl.ds(0, 1)], sem).wait()
            return c
        lax.fori_loop(0, chunk, body, 0)

    @pl.when(s < zero_steps)
    def _():
        def body(r, c):
            pltpu.make_async_copy(z_hbm.at[pl.ds(0, 1)], o_hbm.at[pl.ds(idx_ref[r], 1)], sem).start()
            return c
        lax.fori_loop(0, chunk, body, 0)
        wait_all()

    @pl.when(s >= zero_steps)
    def _():
        t0 = (s - zero_steps) * tokens

        def body(t, c):
            for k in range(TOP_K):
                pltpu.make_async_copy(x_hbm.at[pl.ds(t0 + t, 1)],
                                      o_hbm.at[pl.ds(idx_ref[t * TOP_K + k], 1)], sem).start()
            return c
        lax.fori_loop(0, tokens, body, 0)
        wait_all()


def _dispatch(xpk, pad_dest, dest, m_pad, chunk=2048):
    t, dh = xpk.shape
    chunk = _tile(dest.size, chunk)
    assert pad_dest.size % chunk == 0 and chunk % TOP_K == 0
    idx_all = jnp.concatenate([pad_dest.reshape(-1), dest.reshape(-1)]).astype(I32)
    zero_steps = pad_dest.size // chunk
    zrow = jnp.zeros((8, dh), U32)
    return pl.pallas_call(
        functools.partial(_dispatch_kernel, zero_steps=zero_steps, chunk=chunk),
        grid=(idx_all.size // chunk,),
        in_specs=[pl.BlockSpec((chunk,), lambda s: (s,), memory_space=pltpu.SMEM),
                  pl.BlockSpec(memory_space=pl.ANY),
                  pl.BlockSpec(memory_space=pl.ANY)],
        out_specs=pl.BlockSpec(memory_space=pl.ANY),
        out_shape=jax.ShapeDtypeStruct((m_pad, dh), U32),
        scratch_shapes=[pltpu.SemaphoreType.DMA(())],
        compiler_params=_cparams("arbitrary"),
    )(idx_all, xpk, zrow)


def _expert_kernel(be_ref, nv_ref, xs_ref, wgu_ref, wd_ref, y_ref, wgu_bf, wd_bf, *, cast_rows):
    i = pl.program_id(0)

    @pl.when(i < nv_ref[0])
    def _():
        prev = be_ref[jnp.maximum(i - 1, 0)]

        @pl.when((i == 0) | (be_ref[i] != prev))
        def _():
            def cast_gu(c, carry):
                r = pl.ds(pl.multiple_of(c * cast_rows, cast_rows), cast_rows)
                wgu_bf[r, :] = wgu_ref[r, :].astype(BF16)
                return carry
            lax.fori_loop(0, wgu_ref.shape[0] // cast_rows, cast_gu, 0)
            wd_bf[...] = wd_ref[...].astype(BF16)

        half = xs_ref.shape[1]
        eh = wd_ref.shape[0]
        lo, hi = _unpack_rows(xs_ref[...])
        gu = (jnp.dot(lo, wgu_bf[:half, :], preferred_element_type=F32)
              + jnp.dot(hi, wgu_bf[half:, :], preferred_element_type=F32))
        a = (_silu(gu[:, :eh]) * gu[:, eh:]).astype(BF16)
        y_ref[...] = jnp.dot(a, wd_bf[...], preferred_element_type=F32)


def _experts(xs, block_e, nvalid, w_gu, w_down, bm):
    m_pad, dh = xs.shape
    e, d, eh2 = w_gu.shape
    eh = w_down.shape[1]
    nb = m_pad // bm
    cast_rows = _tile(d, 256)

    def blk(i, be, nv):
        return jnp.minimum(i, nv[0] - 1)

    grid_spec = pltpu.PrefetchScalarGridSpec(
        num_scalar_prefetch=2,
        grid=(nb,),
        in_specs=[pl.BlockSpec((bm, dh), lambda i, be, nv: (blk(i, be, nv), 0)),
                  pl.BlockSpec((None, d, eh2), lambda i, be, nv: (be[blk(i, be, nv)], 0, 0)),
                  pl.BlockSpec((None, eh, d), lambda i, be, nv: (be[blk(i, be, nv)], 0, 0))],
        out_specs=pl.BlockSpec((bm, d), lambda i, be, nv: (blk(i, be, nv), 0)),
        scratch_shapes=[pltpu.VMEM((d, eh2), BF16), pltpu.VMEM((eh, d), BF16)],
    )
    return pl.pallas_call(
        functools.partial(_expert_kernel, cast_rows=cast_rows),
        grid_spec=grid_spec,
        out_shape=jax.ShapeDtypeStruct((m_pad, d), F32),
        compiler_params=_cparams("arbitrary"),
    )(block_e, nvalid, xs, w_gu, w_down)


def _combine_kernel(dcur_ref, dnext_ref, x_ref, w_ref, sgu_ref, sd_ref, g_ref, b_ref, y_hbm,
                    o_ref, pk_ref, ybuf, sem, *, alpha):
    i = pl.program_id(0)
    n = pl.num_programs(0)
    tm = x_ref.shape[0]
    slot = i % 2

    def issue(d_ref, s):
        def body(t, c):
            for k in range(TOP_K):
                pltpu.make_async_copy(y_hbm.at[pl.ds(d_ref[t * TOP_K + k], 1)],
                                      ybuf.at[s, k, pl.ds(t, 1)], sem.at[s]).start()
            return c
        lax.fori_loop(0, tm, body, 0)

    @pl.when(i == 0)
    def _():
        issue(dcur_ref, 0)

    @pl.when(i + 1 < n)
    def _():
        issue(dnext_ref, 1 - slot)

    def wait_body(r, c):
        pltpu.make_async_copy(y_hbm.at[pl.ds(0, 1)], ybuf.at[slot, 0, pl.ds(0, 1)], sem.at[slot]).wait()
        return c
    lax.fori_loop(0, tm * TOP_K, wait_body, 0)

    x = x_ref[...]
    w = w_ref[...]
    sh = sd_ref.shape[0]
    sg = jnp.dot(x.astype(BF16), sgu_ref[...], preferred_element_type=F32)
    a = (_silu(sg[:, :sh]) * sg[:, sh:]).astype(BF16)
    moe = jnp.dot(a, sd_ref[...], preferred_element_type=F32)
    for k in range(TOP_K):
        moe = moe + w[:, k:k + 1] * ybuf[slot, k]
    y = _layer_norm(alpha * x + moe, g_ref[...], b_ref[...])
    o_ref[...] = y
    pk_ref[...] = _pack_rows(y)


def _combine(x2d, wts, dest, y_sorted, s_gu_bf, s_down_bf, g, b, alpha, tm=128):
    t, d = x2d.shape
    tm = _tile(t, tm)
    n = t // tm
    sh = s_down_bf.shape[0]
    dflat = dest.reshape(-1).astype(I32)
    return pl.pallas_call(
        functools.partial(_combine_kernel, alpha=alpha),
        grid=(n,),
        in_specs=[pl.BlockSpec((tm * TOP_K,), lambda i: (i,), memory_space=pltpu.SMEM),
                  pl.BlockSpec((tm * TOP_K,), lambda i: (jnp.minimum(i + 1, n - 1),), memory_space=pltpu.SMEM),
                  pl.BlockSpec((tm, d), lambda i: (i, 0)),
                  pl.BlockSpec((tm, TOP_K), lambda i: (i, 0)),
                  pl.BlockSpec((d, 2 * sh), lambda i: (0, 0)),
                  pl.BlockSpec((sh, d), lambda i: (0, 0)),
                  pl.BlockSpec((1, d), lambda i: (0, 0)),
                  pl.BlockSpec((1, d), lambda i: (0, 0)),
                  pl.BlockSpec(memory_space=pl.ANY)],
        out_specs=[pl.BlockSpec((tm, d), lambda i: (i, 0)),
                   pl.BlockSpec((tm, d // 2), lambda i: (i, 0))],
        out_shape=[jax.ShapeDtypeStruct((t, d), F32), jax.ShapeDtypeStruct((t, d // 2), U32)],
        scratch_shapes=[pltpu.VMEM((2, TOP_K, tm, d), F32), pltpu.SemaphoreType.DMA((2,))],
        compiler_params=_cparams("arbitrary"),
    )(dflat, dflat, x2d, wts, s_gu_bf, s_down_bf,
      g.reshape(1, d).astype(F32), b.reshape(1, d).astype(F32), y_sorted)


def _moe_layer(x2d, xpk, router_w, router_bias, w_gu, w_down, s_gu, s_down, g, b, alpha, bm=256):
    t, d = x2d.shape
    e = router_w.shape[1]
    eidx_t, wts_t = _router(x2d, router_w, router_bias)
    eidx, wts = eidx_t.T, wts_t.T

    onehot = jnp.sum(jax.nn.one_hot(eidx, e, dtype=I32), axis=1)
    cum = jnp.cumsum(onehot, axis=0)
    rank = jnp.take_along_axis(cum - onehot, eidx, axis=1)
    counts = cum[-1]
    padded = ((counts + bm - 1) // bm) * bm
    ends = jnp.cumsum(padded)
    starts = ends - padded
    dest = starts[eidx] + rank
    m_pad = t * TOP_K + e * bm
    nb = m_pad // bm
    block_e = jnp.clip(jnp.searchsorted(ends, jnp.arange(nb, dtype=I32) * bm, side='right'), 0, e - 1).astype(I32)
    nvalid = (ends[-1:] // bm).astype(I32)
    q = jnp.arange(e * bm, dtype=I32)
    qe, qr = q // bm, q % bm
    is_pad = qr < (padded - counts)[qe]
    tail = ends[-1] + jnp.cumsum(jnp.logical_not(is_pad).astype(I32)) - 1
    pad_dest = jnp.where(is_pad, starts[qe] + counts[qe] + qr, tail).astype(I32)

    xs = _dispatch(xpk, pad_dest, dest, m_pad)
    y_sorted = _experts(xs, block_e, nvalid, w_gu, w_down, bm)
    return _combine(x2d, wts, dest, y_sorted, s_gu.astype(BF16), s_down.astype(BF16), g, b, alpha)


def _proj_rope_kernel(x_ref, w_ref, cos_ref, sin_ref, o_ref):
    y = jnp.dot(x_ref[...].astype(BF16), w_ref[...], preferred_element_type=F32)
    tm, n = y.shape
    tw = cos_ref.shape[1]
    lane = lax.broadcasted_iota(I32, (tm, LANES), 1)
    first_half = (lane % HD_B) < (HD_B // 2)
    for c in range(n // LANES):
        yc = y[:, c * LANES:(c + 1) * LANES]
        off = (c * LANES) % tw
        rot = jnp.where(first_half, pltpu.roll(yc, LANES - HD_B // 2, axis=1), pltpu.roll(yc, HD_B // 2, axis=1))
        yc = yc * cos_ref[:, off:off + LANES] + rot * sin_ref[:, off:off + LANES]
        o_ref[:, c * LANES:(c + 1) * LANES] = yc.astype(o_ref.dtype)


def _proj_rope(x2d, w_bf, cos, sin, seq, tm=512, tn=1024):
    t, d = x2d.shape
    n = w_bf.shape[1]
    tw = cos.shape[1]
    tm, tn = _tile(seq, tm), _tile(n, tn)
    assert tn % tw == 0 and tw % LANES == 0
    ns = seq // tm
    return pl.pallas_call(
        _proj_rope_kernel,
        grid=(n // tn, t // tm),
        in_specs=[pl.BlockSpec((tm, d), lambda j, i: (i, 0)),
                  pl.BlockSpec((d, tn), lambda j, i: (0, j)),
                  pl.BlockSpec((tm, tw), lambda j, i: (i % ns, 0)),
                  pl.BlockSpec((tm, tw), lambda j, i: (i % ns, 0))],
        out_specs=pl.BlockSpec((tm, tn), lambda j, i: (i, j)),
        out_shape=jax.ShapeDtypeStruct((t, n), BF16),
        compiler_params=_cparams("arbitrary", "arbitrary"),
    )(x2d, w_bf, cos, sin)


def _rope_tables(seq, scale, heads, plain_cols=0):
    half = HD_B // 2
    inv = 1.0 / (ROPE_THETA ** (jnp.arange(half, dtype=F32) / half))
    ang = jnp.arange(seq, dtype=F32)[:, None] * inv[None, :]
    cos, sin = jnp.cos(ang), jnp.sin(ang)
    cos_t = jnp.tile(jnp.concatenate([cos, cos], axis=1), (1, heads)) * scale
    sin_t = jnp.tile(jnp.concatenate([-sin, sin], axis=1), (1, heads)) * scale
    cos_t = jnp.concatenate([cos_t, jnp.ones((seq, plain_cols), F32)], axis=1)
    sin_t = jnp.concatenate([sin_t, jnp.zeros((seq, plain_cols), F32)], axis=1)
    return cos_t, sin_t


def _swa_kernel(q_ref, kc_ref, kp_ref, vc_ref, vp_ref, sink_ref, o_ref, *, groups):
    qi = pl.program_id(2)
    tq = q_ref.shape[0]
    w = WINDOW
    row = lax.broadcasted_iota(I32, (w, 2 * w), 0)
    col = lax.broadcasted_iota(I32, (w, 2 * w), 1)
    band = (col > row) & (col <= row + w)
    neg = jnp.float32(-jnp.inf)
    for j in range(tq // w):
        if j == 0:
            keys = [jnp.concatenate([kp_ref[s], kc_ref[s, :w, :]], axis=0) for s in range(2)]
            vals = [jnp.concatenate([vp_ref[s], vc_ref[s, :w, :]], axis=0) for s in range(2)]
            mask = band & ((col >= w) | (qi > 0))
        else:
            keys = [kc_ref[s, (j - 1) * w:(j + 1) * w, :] for s in range(2)]
            vals = [vc_ref[s, (j - 1) * w:(j + 1) * w, :] for s in range(2)]
            mask = band
        for p in range(groups // 2):
            qs = q_ref[j * w:(j + 1) * w, p * LANES:(p + 1) * LANES]
            out = jnp.zeros((w, LANES), F32)
            for s in range(2):
                sink = sink_ref[2 * p + s:2 * p + s + 1, 0:1]
                sc = lax.dot_general(qs, keys[s], (((1,), (1,)), ((), ())), preferred_element_type=F32)
                sc = jnp.where(mask, sc, neg)
                m = jnp.maximum(jnp.max(sc, axis=-1, keepdims=True), sink)
                pr = jnp.exp(sc - m)
                pr = pr / (jnp.sum(pr, axis=-1, keepdims=True) + jnp.exp(sink - m))
                out = out + jnp.dot(pr.astype(BF16), vals[s], preferred_element_type=F32)
            o_ref[j * w:(j + 1) * w, p * LANES:(p + 1) * LANES] = out.astype(o_ref.dtype)


def _swa_attn(q, kk, vv, sink_tab, batch, seq, kvh, groups, tq=512):
    t = q.shape[0]
    tq = _tile(seq, tq)
    assert tq % WINDOW == 0 and groups % 2 == 0 and 2 * HD_B == LANES
    nq = seq // tq
    per = tq // WINDOW
    gw = groups * HD_B
    cur = pl.BlockSpec((None, None, 2, tq, LANES), lambda b, h, i: (b, h, 0, i, 0))
    prev = pl.BlockSpec((None, None, 2, WINDOW, LANES), lambda b, h, i: (b, h, 0, jnp.maximum(i * per - 1, 0), 0))
    return pl.pallas_call(
        functools.partial(_swa_kernel, groups=groups),
        grid=(batch, kvh, nq),
        in_specs=[pl.BlockSpec((tq, gw), lambda b, h, i: (b * nq + i, h)),
                  cur, prev, cur, prev,
                  pl.BlockSpec((None, groups, LANES), lambda b, h, i: (h, 0, 0))],
        out_specs=pl.BlockSpec((tq, gw), lambda b, h, i: (b * nq + i, h)),
        out_shape=jax.ShapeDtypeStruct((t, kvh * gw), BF16),
        compiler_params=_cparams("arbitrary", "arbitrary", "arbitrary"),
    )(q, kk, kk, vv, vv, sink_tab)


def _lane_pairs(a, batch, seq, kvh):
    a4 = a.reshape(batch, seq, kvh, HD_B).transpose(0, 2, 1, 3)
    z = jnp.zeros_like(a4)
    return jnp.stack([jnp.concatenate([a4, z], axis=-1), jnp.concatenate([z, a4], axis=-1)], axis=2)


def kernel(x, a_w_in, a_b_f, a_w_o, kv_w, b_w_q, b_sinks, b_w_o, router_w, router_bias, moe_w_gu, moe_w_down,
           shared_w_gu, shared_w_down, ln1_g, ln1_b, ln2_g, ln2_b):
    batch, seq, d = x.shape
    n_a, n_b = a_w_in.shape[0], b_w_q.shape[0]
    depth = n_a + n_b
    alpha = float((2 * depth) ** 0.25)
    heads_a = a_b_f.shape[1]
    hd_a = d // heads_a
    heads_b = b_sinks.shape[1]
    kvh = kv_w.shape[1] // (2 * HD_B)
    groups = heads_b // kvh
    t = batch * seq

    x2d = x.reshape(t, d).astype(F32)
    xpk = None
    kk = vv = None
    cos_q, sin_q = _rope_tables(seq, 1.0 / math.sqrt(HD_B), LANES // HD_B)
    cos_k, sin_k = _rope_tables(seq, 1.0, kvh, plain_cols=kvh * HD_B)
    for i in range(depth):
        if i < n_a:
            w_in = a_w_in[i]
            qkv = _proj(x2d, w_in[:, :3 * d].astype(BF16), scaled_cols=d, scale=1.0 / math.sqrt(hd_a),
                        tn=min(d, 2048))
            cum = _fox_cum(x2d, w_in[:, 3 * d:], a_b_f[i], batch, seq)
            tb = _tile(seq, 512)
            cum_t = cum[:, :heads_a].reshape(batch, seq, heads_a).transpose(0, 2, 1)
            cum_t = cum_t.reshape(batch, heads_a, seq // tb, 1, tb)
            attn = _fox_attn(qkv, cum_t, batch, seq, heads_a, hd_a, tb=tb)
            w_o = a_w_o[i]
        else:
            j = i - n_a
            q = _proj_rope(x2d, b_w_q[j].astype(BF16), cos_q, sin_q, seq)
            sink_tab = jnp.broadcast_to(b_sinks[j].astype(F32).reshape(kvh, groups, 1), (kvh, groups, LANES))
            attn = _swa_attn(q, kk, vv, sink_tab, batch, seq, kvh, groups)
            w_o = b_w_o[j]
        x2d, xpk = _oproj_ln(attn, w_o.astype(BF16), x2d, ln1_g[i], ln1_b[i], alpha)
        x2d, xpk = _moe_layer(x2d, xpk, router_w[i], router_bias[i], moe_w_gu[i], moe_w_down[i],
                              shared_w_gu[i], shared_w_down[i], ln2_g[i], ln2_b[i], alpha)
        if i == n_a - 1:
            kv = _proj_rope(x2d, kv_w.astype(BF16), cos_k, sin_k, seq, tn=kv_w.shape[1])
            kk = _lane_pairs(kv[:, :kvh * HD_B], batch, seq, kvh)
            vv = _lane_pairs(kv[:, kvh * HD_B:], batch, seq, kvh)
    return x2d.reshape(batch, seq, d).astype(x.dtype)
```

```python
import functools
import math

import jax
import jax.numpy as jnp
from jax import lax
from jax.experimental import pallas as pl
from jax.experimental.pallas import tpu as pltpu

F32 = jnp.float32
BF16 = jnp.bfloat16
U32 = jnp.uint32
I32 = jnp.int32

LANES = 128
SUBLANES = 8
HD_B = 64
WINDOW = 128
ROPE_THETA = 10000.0
TOP_K = 8
N_GROUPS = 8
TOPK_GROUPS = 4
ROUTED_SCALE = 2.5
LN_EPS = 1e-5
VMEM_LIMIT_BYTES = 56 * 1024 * 1024
HI_MASK = 0xFFFF0000


def _cparams(*sem):
    return pltpu.CompilerParams(dimension_semantics=sem, vmem_limit_bytes=VMEM_LIMIT_BYTES)


def _tile(dim, pref):
    t = min(dim, pref)
    assert dim % t == 0, (dim, pref)
    return t


def _silu(x):
    return x * jax.nn.sigmoid(x)


def _layer_norm(z, g, b):
    mu = jnp.mean(z, axis=-1, keepdims=True)
    zc = z - mu
    var = jnp.mean(zc * zc, axis=-1, keepdims=True)
    return zc * lax.rsqrt(var + LN_EPS) * g + b


def _pack_halves(lo, hi):
    lo = lax.bitcast_convert_type(lo, U32)
    hi = lax.bitcast_convert_type(hi, U32)
    return (hi & jnp.uint32(HI_MASK)) | (lo >> jnp.uint32(16))


def _pack_rows(y):
    half = y.shape[1] // 2
    yb = y.astype(BF16).astype(F32)
    return _pack_halves(yb[:, :half], yb[:, half:])


def _unpack_rows(pk):
    lo = lax.bitcast_convert_type(pk << jnp.uint32(16), F32).astype(BF16)
    hi = lax.bitcast_convert_type(pk & jnp.uint32(HI_MASK), F32).astype(BF16)
    return lo, hi


def _proj_kernel(x_ref, w_ref, o_ref, *, scaled_tiles, scale):
    j = pl.program_id(0)
    y = jnp.dot(x_ref[...].astype(BF16), w_ref[...], preferred_element_type=F32)
    s = jnp.where(j < scaled_tiles, jnp.float32(scale), jnp.float32(1.0))
    o_ref[...] = (y * s).astype(o_ref.dtype)


def _proj(x2d, w_bf, *, scaled_cols, scale, tm=512, tn=2048):
    t, d = x2d.shape
    n = w_bf.shape[1]
    tm, tn = _tile(t, tm), _tile(n, tn)
    assert scaled_cols % tn == 0
    return pl.pallas_call(
        functools.partial(_proj_kernel, scaled_tiles=scaled_cols // tn, scale=scale),
        grid=(n // tn, t // tm),
        in_specs=[pl.BlockSpec((tm, d), lambda j, i: (i, 0)),
                  pl.BlockSpec((d, tn), lambda j, i: (0, j))],
        out_specs=pl.BlockSpec((tm, tn), lambda j, i: (i, j)),
        out_shape=jax.ShapeDtypeStruct((t, n), BF16),
        compiler_params=_cparams("arbitrary", "arbitrary"),
    )(x2d, w_bf)


def _fox_cum_kernel(x_ref, w_ref, b_ref, o_ref, carry_ref):
    @pl.when(pl.program_id(1) == 0)
    def _():
        carry_ref[...] = jnp.zeros_like(carry_ref)

    ts = x_ref.shape[0]
    z = jnp.dot(x_ref[...].astype(BF16), w_ref[...], preferred_element_type=F32) + b_ref[...]
    logf = -(jnp.maximum(-z, 0.0) + jnp.log1p(jnp.exp(-jnp.abs(z))))
    row = lax.broadcasted_iota(I32, (ts, ts), 0)
    col = lax.broadcasted_iota(I32, (ts, ts), 1)
    tril = (row >= col).astype(F32)
    cum = jnp.dot(tril, logf, preferred_element_type=F32, precision=lax.Precision.HIGHEST) + carry_ref[...]
    o_ref[...] = cum
    carry_ref[...] = cum[ts - 1:ts, :]


def _fox_cum(x2d, w_f, b_f, batch, seq, ts=512):
    t, d = x2d.shape
    h = w_f.shape[1]
    assert h <= LANES
    ts = _tile(seq, ts)
    w_pad = jnp.zeros((d, LANES), BF16).at[:, :h].set(w_f.astype(BF16))
    b_pad = jnp.zeros((1, LANES), F32).at[0, :h].set(b_f.astype(F32))
    ns = seq // ts
    return pl.pallas_call(
        _fox_cum_kernel,
        grid=(batch, ns),
        in_specs=[pl.BlockSpec((ts, d), lambda b, s: (b * ns + s, 0)),
                  pl.BlockSpec((d, LANES), lambda b, s: (0, 0)),
                  pl.BlockSpec((1, LANES), lambda b, s: (0, 0))],
        out_specs=pl.BlockSpec((ts, LANES), lambda b, s: (b * ns + s, 0)),
        out_shape=jax.ShapeDtypeStruct((t, LANES), F32),
        scratch_shapes=[pltpu.VMEM((1, LANES), F32)],
        compiler_params=_cparams("arbitrary", "arbitrary"),
    )(x2d, w_pad, b_pad)


def _fox_attn_kernel(q_ref, k_ref, v_ref, c_ref, o_ref, *, tb):
    qi = pl.program_id(2)
    q = q_ref[...]
    hd = q.shape[1]

    def scores(j):
        k = k_ref[pl.ds(pl.multiple_of(j * tb, tb), tb), :]
        s = lax.dot_general(q, k, (((1,), (1,)), ((), ())), preferred_element_type=F32)
        return s - c_ref[j]

    def update(j, t, carry):
        m, l, acc = carry
        v = v_ref[pl.ds(pl.multiple_of(j * tb, tb), tb), :]
        m_new = jnp.maximum(m, jnp.max(t, axis=-1, keepdims=True))
        p = jnp.exp(t - m_new)
        alpha = jnp.exp(m - m_new)
        l = alpha * l + jnp.sum(p, axis=-1, keepdims=True)
        acc = alpha * acc + jnp.dot(p.astype(BF16), v, preferred_element_type=F32)
        return m_new, l, acc

    init = (jnp.full((tb, 1), -jnp.inf, F32), jnp.zeros((tb, 1), F32), jnp.zeros((tb, hd), F32))
    carry = lax.fori_loop(0, qi, lambda j, c: update(j, scores(j), c), init)
    row = lax.broadcasted_iota(I32, (tb, tb), 0)
    col = lax.broadcasted_iota(I32, (tb, tb), 1)
    t = jnp.where(row >= col, scores(qi), -jnp.inf)
    _, l, acc = update(qi, t, carry)
    o_ref[...] = (acc / l).astype(o_ref.dtype)


def _fox_attn(qkv, cum_t, batch, seq, heads, hd, tb=512):
    t = qkv.shape[0]
    d = heads * hd
    tb = _tile(seq, tb)
    nq = seq // tb
    assert hd % LANES == 0
    return pl.pallas_call(
        functools.partial(_fox_attn_kernel, tb=tb),
        grid=(batch, heads, nq),
        in_specs=[pl.BlockSpec((tb, hd), lambda b, h, i: (b * nq + i, h)),
                  pl.BlockSpec((seq, hd), lambda b, h, i: (b, heads + h)),
                  pl.BlockSpec((seq, hd), lambda b, h, i: (b, 2 * heads + h)),
                  pl.BlockSpec((None, None, nq, 1, tb), lambda b, h, i: (b, h, 0, 0, 0))],
        out_specs=pl.BlockSpec((tb, hd), lambda b, h, i: (b * nq + i, h)),
        out_shape=jax.ShapeDtypeStruct((t, d), BF16),
        compiler_params=_cparams("arbitrary", "arbitrary", "arbitrary"),
    )(qkv, qkv, qkv, cum_t)


def _oproj_ln_kernel(a_ref, w_ref, x_ref, g_ref, b_ref, o_ref, ob_ref, acc_ref, *, alpha):
    k = pl.program_id(1)

    @pl.when(k == 0)
    def _():
        acc_ref[...] = jnp.zeros_like(acc_ref)

    acc_ref[...] += jnp.dot(a_ref[...], w_ref[...], preferred_element_type=F32)

    @pl.when(k == pl.num_programs(1) - 1)
    def _():
        y = _layer_norm(alpha * x_ref[...] + acc_ref[...], g_ref[...], b_ref[...])
        o_ref[...] = y
        ob_ref[...] = y.astype(BF16)


def _oproj_ln(a_bf, w_bf, x2d, g, b, alpha, tm=512, tk=1024):
    t, d = x2d.shape
    kk = a_bf.shape[1]
    tm, tk = _tile(t, tm), _tile(kk, tk)
    return pl.pallas_call(
        functools.partial(_oproj_ln_kernel, alpha=alpha),
        grid=(t // tm, kk // tk),
        in_specs=[pl.BlockSpec((tm, tk), lambda i, k: (i, k)),
                  pl.BlockSpec((tk, d), lambda i, k: (k, 0)),
                  pl.BlockSpec((tm, d), lambda i, k: (i, 0)),
                  pl.BlockSpec((1, d), lambda i, k: (0, 0)),
                  pl.BlockSpec((1, d), lambda i, k: (0, 0))],
        out_specs=[pl.BlockSpec((tm, d), lambda i, k: (i, 0)),
                   pl.BlockSpec((tm, d), lambda i, k: (i, 0))],
        out_shape=[jax.ShapeDtypeStruct((t, d), F32), jax.ShapeDtypeStruct((t, d), BF16)],
        scratch_shapes=[pltpu.VMEM((tm, d), F32)],
        compiler_params=_cparams("arbitrary", "arbitrary"),
    )(a_bf, w_bf, x2d, g.reshape(1, d).astype(F32), b.reshape(1, d).astype(F32))


def _first_max(vals, idx, sentinel):
    m = jnp.max(vals, axis=0, keepdims=True)
    first = jnp.min(jnp.where(vals == m, idx, sentinel), axis=0, keepdims=True)
    return m, first


def _router_kernel(x_ref, rwt_ref, bias_ref, eidx_ref, wts_ref):
    e = rwt_ref.shape[0]
    tm = x_ref.shape[0]
    per = e // N_GROUPS
    logits = lax.dot_general(rwt_ref[...], x_ref[...], (((1,), (1,)), ((), ())),
                             preferred_element_type=F32, precision=lax.Precision.HIGHEST)
    scores = jax.nn.sigmoid(logits)
    choice = scores + bias_ref[...]
    neg = jnp.float32(-jnp.inf)

    ip = lax.broadcasted_iota(I32, (per, tm), 0)
    gs = []
    for g in range(N_GROUPS):
        c = choice[g * per:(g + 1) * per, :]
        m1, f1 = _first_max(c, ip, per)
        m2 = jnp.max(jnp.where(ip == f1, neg, c), axis=0, keepdims=True)
        gs.append(m1 + m2)
    gscore = jnp.concatenate(gs, axis=0)

    ig = lax.broadcasted_iota(I32, (N_GROUPS, tm), 0)
    sel = jnp.zeros((N_GROUPS, tm), F32)
    for _ in range(TOPK_GROUPS):
        _, fg = _first_max(gscore, ig, N_GROUPS)
        hit = ig == fg
        sel = jnp.where(hit, 1.0, sel)
        gscore = jnp.where(hit, neg, gscore)

    masked = jnp.concatenate(
        [jnp.where(sel[g:g + 1, :] > 0.0, choice[g * per:(g + 1) * per, :], neg) for g in range(N_GROUPS)],
        axis=0)
    ie = lax.broadcasted_iota(I32, (e, tm), 0)
    idxs, ws = [], []
    for _ in range(TOP_K):
        _, fe = _first_max(masked, ie, e)
        hit = ie == fe
        idxs.append(fe)
        ws.append(jnp.sum(jnp.where(hit, scores, 0.0), axis=0, keepdims=True))
        masked = jnp.where(hit, neg, masked)
    w = jnp.concatenate(ws, axis=0)
    eidx_ref[...] = jnp.concatenate(idxs, axis=0)
    wts_ref[...] = w / jnp.sum(w, axis=0, keepdims=True) * ROUTED_SCALE


def _router(x2d, router_w, router_bias, tm=512):
    t, d = x2d.shape
    e = router_w.shape[1]
    assert e % N_GROUPS == 0 and (e // N_GROUPS) % 8 == 0
    tm = _tile(t, tm)
    return pl.pallas_call(
        _router_kernel,
        grid=(t // tm,),
        in_specs=[pl.BlockSpec((tm, d), lambda i: (i, 0)),
                  pl.BlockSpec((e, d), lambda i: (0, 0)),
                  pl.BlockSpec((e, 1), lambda i: (0, 0))],
        out_specs=[pl.BlockSpec((TOP_K, tm), lambda i: (0, i)),
                   pl.BlockSpec((TOP_K, tm), lambda i: (0, i))],
        out_shape=[jax.ShapeDtypeStruct((TOP_K, t), I32), jax.ShapeDtypeStruct((TOP_K, t), F32)],
        compiler_params=_cparams("arbitrary"),
    )(x2d, router_w.T.astype(F32), router_bias.reshape(e, 1).astype(F32))


def _seg_copy(src, dst, src_off, dst_off, length, max_len, sem, wait=False):
    done = 0
    b = max_len
    while b >= SUBLANES:
        piece = length & b

        @pl.when(piece != 0)
        def _(b=b, done=done):
            cp = pltpu.make_async_copy(src.at[pl.ds(pl.multiple_of(src_off + done, SUBLANES), b)],
                                       dst.at[pl.ds(pl.multiple_of(dst_off + done, SUBLANES), b)], sem)
            if wait:
                cp.wait()
            else:
                cp.start()
        done = done + piece
        b //= 2


def _pow2_floor(n):
    return 1 << (n.bit_length() - 1)


def _dispatch_kernel(tab_ref, pad_ref, lpt_ref, x_ref, o_hbm, xs, zbuf, sem, *, n_exp, rc):
    i = pl.program_id(0)
    n = pl.num_programs(0)
    slot = i % 2
    tt = x_ref.shape[0]
    r_tot, half = xs.shape[1], xs.shape[2]

    def wait_slot(s, step):
        _seg_copy(xs.at[s], o_hbm, 0, 0, pad_ref[2 * n_exp + step], _pow2_floor(r_tot), sem.at[s], wait=True)

    @pl.when(i >= 2)
    def _():
        wait_slot(slot, i - 2)

    xlo, xhi = x_ref[:, :half], x_ref[:, half:]
    lpt = lpt_ref[...]
    for c in range(r_tot // rc):
        rows = lax.broadcasted_iota(I32, (rc, tt), 0) + c * rc
        hit = rows == lpt[0:1, :]
        for k in range(1, TOP_K):
            hit = hit | (rows == lpt[k:k + 1, :])
        g = jnp.where(hit, 1.0, 0.0).astype(BF16)
        lo = jnp.dot(g, xlo, preferred_element_type=F32)
        hi = jnp.dot(g, xhi, preferred_element_type=F32)
        xs[slot, c * rc:(c + 1) * rc, :] = _pack_halves(lo, hi)

    for e in range(n_exp):
        _seg_copy(xs.at[slot], o_hbm, tab_ref[2 * n_exp + e], tab_ref[e], tab_ref[n_exp + e], tt, sem.at[slot])

    @pl.when(i == 0)
    def _():
        zbuf[...] = jnp.zeros_like(zbuf)
        for w in (False, True):
            for e in range(n_exp):
                _seg_copy(zbuf, o_hbm, 0, pad_ref[e], pad_ref[n_exp + e], zbuf.shape[0] // 2, sem.at[2], wait=w)

    @pl.when(i == n - 1)
    def _():
        @pl.when(n >= 2)
        def _():
            wait_slot(1 - slot, i - 1)
        wait_slot(slot, i)


def _tile_rows(tt, n_exp, rc=512):
    r = tt * TOP_K + (SUBLANES - 1) * n_exp
    rc = min(rc, tt * TOP_K)
    return -(-r // rc) * rc, rc


def _dispatch(xbf, tab, pad_tab, lpos_t, m_pad, n_exp, tt, bm):
    t, d = xbf.shape
    assert tt & (tt - 1) == 0 and bm & (bm - 1) == 0 and d % 2 == 0 and tt % SUBLANES == 0
    r_tot, rc = _tile_rows(tt, n_exp)
    return pl.pallas_call(
        functools.partial(_dispatch_kernel, n_exp=n_exp, rc=rc),
        grid=(t // tt,),
        in_specs=[pl.BlockSpec((4 * n_exp,), lambda i: (i,), memory_space=pltpu.SMEM),
                  pl.BlockSpec(memory_space=pltpu.SMEM),
                  pl.BlockSpec((TOP_K, tt), lambda i: (0, i)),
                  pl.BlockSpec((tt, d), lambda i: (i, 0))],
        out_specs=pl.BlockSpec(memory_space=pl.ANY),
        out_shape=jax.ShapeDtypeStruct((m_pad, d // 2), U32),
        scratch_shapes=[pltpu.VMEM((2, r_tot, d // 2), U32), pltpu.VMEM((bm, d // 2), U32),
                        pltpu.SemaphoreType.DMA((3,))],
        compiler_params=_cparams("arbitrary"),
    )(tab, pad_tab, lpos_t, xbf)


def _expert_kernel(be_ref, nv_ref, xs_ref, wgu_ref, wd_ref, y_ref, wgu_bf, wd_bf, *, cast_rows):
    i = pl.program_id(0)

    @pl.when(i < nv_ref[0])
    def _():
        prev = be_ref[jnp.maximum(i - 1, 0)]

        @pl.when((i == 0) | (be_ref[i] != prev))
        def _():
            def cast_gu(c, carry):
                r = pl.ds(pl.multiple_of(c * cast_rows, cast_rows), cast_rows)
                wgu_bf[r, :] = wgu_ref[r, :].astype(BF16)
                return carry
            lax.fori_loop(0, wgu_ref.shape[0] // cast_rows, cast_gu, 0)
            wd_bf[...] = wd_ref[...].astype(BF16)

        half = xs_ref.shape[1]
        eh = wd_ref.shape[0]
        lo, hi = _unpack_rows(xs_ref[...])
        gu = (jnp.dot(lo, wgu_bf[:half, :], preferred_element_type=F32)
              + jnp.dot(hi, wgu_bf[half:, :], preferred_element_type=F32))
        a = (_silu(gu[:, :eh]) * gu[:, eh:]).astype(BF16)
        y_ref[...] = _pack_rows(jnp.dot(a, wd_bf[...], preferred_element_type=F32))


def _experts(xs, block_e, nvalid, w_gu, w_down, bm):
    m_pad, dh = xs.shape
    e, d, eh2 = w_gu.shape
    eh = w_down.shape[1]
    nb = m_pad // bm
    cast_rows = _tile(d, 256)

    def blk(i, be, nv):
        return jnp.minimum(i, nv[0] - 1)

    grid_spec = pltpu.PrefetchScalarGridSpec(
        num_scalar_prefetch=2,
        grid=(nb,),
        in_specs=[pl.BlockSpec((bm, dh), lambda i, be, nv: (blk(i, be, nv), 0)),
                  pl.BlockSpec((None, d, eh2), lambda i, be, nv: (be[blk(i, be, nv)], 0, 0)),
                  pl.BlockSpec((None, eh, d), lambda i, be, nv: (be[blk(i, be, nv)], 0, 0))],
        out_specs=pl.BlockSpec((bm, dh), lambda i, be, nv: (blk(i, be, nv), 0)),
        scratch_shapes=[pltpu.VMEM((d, eh2), BF16), pltpu.VMEM((eh, d), BF16)],
    )
    return pl.pallas_call(
        functools.partial(_expert_kernel, cast_rows=cast_rows),
        grid_spec=grid_spec,
        out_shape=jax.ShapeDtypeStruct((m_pad, dh), U32),
        compiler_params=_cparams("arbitrary"),
    )(block_e, nvalid, xs, w_gu, w_down)


def _combine_kernel(tabc_ref, tabn_ref, lp_ref, w_ref, x_ref, sgu_ref, sd_ref, g_ref, b_ref, y_hbm,
                    o_ref, ybuf, sem, *, alpha, n_exp):
    i = pl.program_id(0)
    n = pl.num_programs(0)
    tt = x_ref.shape[0]
    r_tot = ybuf.shape[1]
    slot = i % 2

    def issue(tab_ref, s):
        for e in range(n_exp):
            _seg_copy(y_hbm, ybuf.at[s], tab_ref[e], tab_ref[2 * n_exp + e], tab_ref[n_exp + e], tt, sem.at[s])

    @pl.when(i == 0)
    def _():
        ybuf[...] = jnp.zeros_like(ybuf)
        issue(tabc_ref, 0)

    @pl.when(i + 1 < n)
    def _():
        issue(tabn_ref, 1 - slot)

    _seg_copy(y_hbm, ybuf.at[slot], 0, 0, tabc_ref[3 * n_exp], _pow2_floor(r_tot), sem.at[slot], wait=True)

    lp = lp_ref[...]
    w = w_ref[...]
    cols = lax.broadcasted_iota(I32, (tt, r_tot), 1)
    p = jnp.zeros((tt, r_tot), F32)
    for k in range(TOP_K):
        p = p + jnp.where(cols == lp[:, k:k + 1], w[:, k:k + 1], 0.0)
    pb = p.astype(BF16)
    ylo, yhi = _unpack_rows(ybuf[slot])
    routed = jnp.concatenate([jnp.dot(pb, ylo, preferred_element_type=F32),
                              jnp.dot(pb, yhi, preferred_element_type=F32)], axis=1)

    x = x_ref[...]
    sh = sd_ref.shape[0]
    sg = jnp.dot(x.astype(BF16), sgu_ref[...], preferred_element_type=F32)
    a = (_silu(sg[:, :sh]) * sg[:, sh:]).astype(BF16)
    shared = jnp.dot(a, sd_ref[...], preferred_element_type=F32)
    o_ref[...] = _layer_norm(alpha * x + (routed + shared), g_ref[...], b_ref[...])


def _combine(x2d, wts, lpos, tab, y_sorted, s_gu_bf, s_down_bf, g, b, alpha, n_exp, tt):
    t, d = x2d.shape
    n = t // tt
    sh = s_down_bf.shape[0]
    return pl.pallas_call(
        functools.partial(_combine_kernel, alpha=alpha, n_exp=n_exp),
        grid=(n,),
        in_specs=[pl.BlockSpec((4 * n_exp,), lambda i: (i,), memory_space=pltpu.SMEM),
                  pl.BlockSpec((4 * n_exp,), lambda i: (jnp.minimum(i + 1, n - 1),), memory_space=pltpu.SMEM),
                  pl.BlockSpec((tt, TOP_K), lambda i: (i, 0)),
                  pl.BlockSpec((tt, TOP_K), lambda i: (i, 0)),
                  pl.BlockSpec((tt, d), lambda i: (i, 0)),
                  pl.BlockSpec((d, 2 * sh), lambda i: (0, 0)),
                  pl.BlockSpec((sh, d), lambda i: (0, 0)),
                  pl.BlockSpec((1, d), lambda i: (0, 0)),
                  pl.BlockSpec((1, d), lambda i: (0, 0)),
                  pl.BlockSpec(memory_space=pl.ANY)],
        out_specs=pl.BlockSpec((tt, d), lambda i: (i, 0)),
        out_shape=jax.ShapeDtypeStruct((t, d), F32),
        scratch_shapes=[pltpu.VMEM((2, _tile_rows(tt, n_exp)[0], d // 2), U32), pltpu.SemaphoreType.DMA((2,))],
        compiler_params=_cparams("arbitrary"),
    )(tab, tab, lpos, wts, x2d, s_gu_bf, s_down_bf,
      g.reshape(1, d).astype(F32), b.reshape(1, d).astype(F32), y_sorted)


def _moe_layer(x2d, xbf, router_w, router_bias, w_gu, w_down, s_gu, s_down, g, b, alpha, bm=256, tt=256):
    t, d = x2d.shape
    e = router_w.shape[1]
    tt = _tile(t, tt)
    nt = t // tt
    eidx_t, wts_t = _router(x2d, router_w, router_bias)
    eidx, wts = eidx_t.T, wts_t.T

    onehot = jnp.sum((eidx[:, :, None] == jnp.arange(e, dtype=I32)).astype(I32), axis=1)
    oh = onehot.reshape(nt, tt, e)
    rank_in_tile = jnp.cumsum(oh, axis=1) - oh
    cnt = jnp.sum(oh, axis=1)
    run = ((cnt + SUBLANES - 1) // SUBLANES) * SUBLANES
    before = jnp.cumsum(run, axis=0) - run
    off = jnp.cumsum(run, axis=1) - run
    counts = jnp.sum(run, axis=0)
    padded = ((counts + bm - 1) // bm) * bm
    ends = jnp.cumsum(padded)
    starts = ends - padded
    lpos = jnp.take_along_axis((off[:, None, :] + rank_in_tile).reshape(t, e), eidx, axis=1).astype(I32)
    totals = jnp.sum(run, axis=1)
    spare = jnp.zeros_like(cnt).at[:, 0].set(totals)
    tab = jnp.concatenate([starts[None, :] + before, run, off, spare], axis=1).reshape(-1).astype(I32)
    pad_tab = jnp.concatenate([starts + counts, padded - counts, totals]).astype(I32)
    m_pad = -(-(t * TOP_K + (SUBLANES - 1) * nt * e) // bm) * bm + e * bm
    nb = m_pad // bm
    block_start = jnp.arange(nb, dtype=I32) * bm
    block_e = jnp.minimum(jnp.sum((ends[None, :] <= block_start[:, None]).astype(I32), axis=1), e - 1).astype(I32)
    nvalid = (ends[-1:] // bm).astype(I32)

    xs = _dispatch(xbf, tab, pad_tab, lpos.T, m_pad, e, tt, bm)
    y_sorted = _experts(xs, block_e, nvalid, w_gu, w_down, bm)
    return _combine(x2d, wts, lpos, tab, y_sorted, s_gu.astype(BF16), s_down.astype(BF16), g, b, alpha, e, tt)


def _proj_rope_kernel(x_ref, w_ref, cos_ref, sin_ref, o_ref):
    y = jnp.dot(x_ref[...].astype(BF16), w_ref[...], preferred_element_type=F32)
    tm, n = y.shape
    tw = cos_ref.shape[1]
    lane = lax.broadcasted_iota(I32, (tm, LANES), 1)
    first_half = (lane % HD_B) < (HD_B // 2)
    for c in range(n // LANES):
        yc = y[:, c * LANES:(c + 1) * LANES]
        off = (c * LANES) % tw
        rot = jnp.where(first_half, pltpu.roll(yc, LANES - HD_B // 2, axis=1), pltpu.roll(yc, HD_B // 2, axis=1))
        yc = yc * cos_ref[:, off:off + LANES] + rot * sin_ref[:, off:off + LANES]
        o_ref[:, c * LANES:(c + 1) * LANES] = yc.astype(o_ref.dtype)


def _proj_rope(x2d, w_bf, cos, sin, seq, tm=512, tn=1024):
    t, d = x2d.shape
    n = w_bf.shape[1]
    tw = cos.shape[1]
    tm, tn = _tile(seq, tm), _tile(n, tn)
    assert tn % tw == 0 and tw % LANES == 0
    ns = seq // tm
    return pl.pallas_call(
        _proj_rope_kernel,
        grid=(n // tn, t // tm),
        in_specs=[pl.BlockSpec((tm, d), lambda j, i: (i, 0)),
                  pl.BlockSpec((d, tn), lambda j, i: (0, j)),
                  pl.BlockSpec((tm, tw), lambda j, i: (i % ns, 0)),
                  pl.BlockSpec((tm, tw), lambda j, i: (i % ns, 0))],
        out_specs=pl.BlockSpec((tm, tn), lambda j, i: (i, j)),
        out_shape=jax.ShapeDtypeStruct((t, n), BF16),
        compiler_params=_cparams("arbitrary", "arbitrary"),
    )(x2d, w_bf, cos, sin)


def _rope_tables(seq, scale, heads, plain_cols=0):
    half = HD_B // 2
    inv = 1.0 / (ROPE_THETA ** (jnp.arange(half, dtype=F32) / half))
    ang = jnp.arange(seq, dtype=F32)[:, None] * inv[None, :]
    cos, sin = jnp.cos(ang), jnp.sin(ang)
    cos_t = jnp.tile(jnp.concatenate([cos, cos], axis=1), (1, heads)) * scale
    sin_t = jnp.tile(jnp.concatenate([-sin, sin], axis=1), (1, heads)) * scale
    cos_t = jnp.concatenate([cos_t, jnp.ones((seq, plain_cols), F32)], axis=1)
    sin_t = jnp.concatenate([sin_t, jnp.zeros((seq, plain_cols), F32)], axis=1)
    return cos_t, sin_t


def _swa_kernel(q_ref, kc_ref, kp_ref, vc_ref, vp_ref, sink_ref, o_ref, *, groups):
    qi = pl.program_id(2)
    tq = q_ref.shape[0]
    w = WINDOW
    row = lax.broadcasted_iota(I32, (w, 2 * w), 0)
    col = lax.broadcasted_iota(I32, (w, 2 * w), 1)
    band = (col > row) & (col <= row + w)
    neg = jnp.float32(-jnp.inf)
    for j in range(tq // w):
        if j == 0:
            keys = [jnp.concatenate([kp_ref[s], kc_ref[s, :w, :]], axis=0) for s in range(2)]
            vals = [jnp.concatenate([vp_ref[s], vc_ref[s, :w, :]], axis=0) for s in range(2)]
            mask = band & ((col >= w) | (qi > 0))
        else:
            keys = [kc_ref[s, (j - 1) * w:(j + 1) * w, :] for s in range(2)]
            vals = [vc_ref[s, (j - 1) * w:(j + 1) * w, :] for s in range(2)]
            mask = band
        for p in range(groups // 2):
            qs = q_ref[j * w:(j + 1) * w, p * LANES:(p + 1) * LANES]
            out = jnp.zeros((w, LANES), F32)
            for s in range(2):
                sink = sink_ref[2 * p + s:2 * p + s + 1, 0:1]
                sc = lax.dot_general(qs, keys[s], (((1,), (1,)), ((), ())), preferred_element_type=F32)
                sc = jnp.where(mask, sc, neg)
                m = jnp.maximum(jnp.max(sc, axis=-1, keepdims=True), sink)
                pr = jnp.exp(sc - m)
                pr = pr / (jnp.sum(pr, axis=-1, keepdims=True) + jnp.exp(sink - m))
                out = out + jnp.dot(pr.astype(BF16), vals[s], preferred_element_type=F32)
            o_ref[j * w:(j + 1) * w, p * LANES:(p + 1) * LANES] = out.astype(o_ref.dtype)


def _swa_attn(q, kk, vv, sink_tab, batch, seq, kvh, groups, tq=512):
    t = q.shape[0]
    tq = _tile(seq, tq)
    assert tq % WINDOW == 0 and groups % 2 == 0 and 2 * HD_B == LANES
    nq = seq // tq
    per = tq // WINDOW
    gw = groups * HD_B
    cur = pl.BlockSpec((None, None, 2, tq, LANES), lambda b, h, i: (b, h, 0, i, 0))
    prev = pl.BlockSpec((None, None, 2, WINDOW, LANES), lambda b, h, i: (b, h, 0, jnp.maximum(i * per - 1, 0), 0))
    return pl.pallas_call(
        functools.partial(_swa_kernel, groups=groups),
        grid=(batch, kvh, nq),
        in_specs=[pl.BlockSpec((tq, gw), lambda b, h, i: (b * nq + i, h)),
                  cur, prev, cur, prev,
                  pl.BlockSpec((None, groups, LANES), lambda b, h, i: (h, 0, 0))],
        out_specs=pl.BlockSpec((tq, gw), lambda b, h, i: (b * nq + i, h)),
        out_shape=jax.ShapeDtypeStruct((t, kvh * gw), BF16),
        compiler_params=_cparams("arbitrary", "arbitrary", "arbitrary"),
    )(q, kk, kk, vv, vv, sink_tab)


def _lane_pairs(a, batch, seq, kvh):
    a4 = a.reshape(batch, seq, kvh, HD_B).transpose(0, 2, 1, 3)
    z = jnp.zeros_like(a4)
    return jnp.stack([jnp.concatenate([a4, z], axis=-1), jnp.concatenate([z, a4], axis=-1)], axis=2)


def kernel(x, a_w_in, a_b_f, a_w_o, kv_w, b_w_q, b_sinks, b_w_o, router_w, router_bias, moe_w_gu, moe_w_down,
           shared_w_gu, shared_w_down, ln1_g, ln1_b, ln2_g, ln2_b):
    batch, seq, d = x.shape
    n_a, n_b = a_w_in.shape[0], b_w_q.shape[0]
    depth = n_a + n_b
    alpha = float((2 * depth) ** 0.25)
    heads_a = a_b_f.shape[1]
    hd_a = d // heads_a
    heads_b = b_sinks.shape[1]
    kvh = kv_w.shape[1] // (2 * HD_B)
    groups = heads_b // kvh
    t = batch * seq

    x2d = x.reshape(t, d).astype(F32)
    kk = vv = None
    cos_q, sin_q = _rope_tables(seq, 1.0 / math.sqrt(HD_B), LANES // HD_B)
    cos_k, sin_k = _rope_tables(seq, 1.0, kvh, plain_cols=kvh * HD_B)
    for i in range(depth):
        if i < n_a:
            w_in = a_w_in[i]
            qkv = _proj(x2d, w_in[:, :3 * d].astype(BF16), scaled_cols=d, scale=1.0 / math.sqrt(hd_a),
                        tn=min(d, 2048))
            cum = _fox_cum(x2d, w_in[:, 3 * d:], a_b_f[i], batch, seq)
            tb = _tile(seq, 512)
            cum_t = cum[:, :heads_a].reshape(batch, seq, heads_a).transpose(0, 2, 1)
            cum_t = cum_t.reshape(batch, heads_a, seq // tb, 1, tb)
            attn = _fox_attn(qkv, cum_t, batch, seq, heads_a, hd_a, tb=tb)
            w_o = a_w_o[i]
        else:
            j = i - n_a
            q = _proj_rope(x2d, b_w_q[j].astype(BF16), cos_q, sin_q, seq)
            sink_tab = jnp.broadcast_to(b_sinks[j].astype(F32).reshape(kvh, groups, 1), (kvh, groups, LANES))
            attn = _swa_attn(q, kk, vv, sink_tab, batch, seq, kvh, groups)
            w_o = b_w_o[j]
        x2d, xbf = _oproj_ln(attn, w_o.astype(BF16), x2d, ln1_g[i], ln1_b[i], alpha)
        x2d = _moe_layer(x2d, xbf, router_w[i], router_bias[i], moe_w_gu[i], moe_w_down[i],
                         shared_w_gu[i], shared_w_down[i], ln2_g[i], ln2_b[i], alpha)
        if i == n_a - 1:
            kv = _proj_rope(x2d, kv_w.astype(BF16), cos_k, sin_k, seq, tn=kv_w.shape[1])
            kk = _lane_pairs(kv[:, :kvh * HD_B], batch, seq, kvh)
            vv = _lane_pairs(kv[:, kvh * HD_B:], batch, seq, kvh)
    return x2d.reshape(batch, seq, d).astype(x.dtype)
```

```python
import functools
import math

import jax
import jax.numpy as jnp
from jax import lax
from jax.experimental import pallas as pl
from jax.experimental.pallas import tpu as pltpu

F32 = jnp.float32
BF16 = jnp.bfloat16
U32 = jnp.uint32
I32 = jnp.int32

LANES = 128
SUBLANES = 8
HD_B = 64
WINDOW = 128
ROPE_THETA = 10000.0
TOP_K = 8
N_GROUPS = 8
TOPK_GROUPS = 4
ROUTED_SCALE = 2.5
LN_EPS = 1e-5
VMEM_LIMIT_BYTES = 56 * 1024 * 1024
HI_MASK = 0xFFFF0000
LOG2E = math.log2(math.e)


def _cparams(*sem):
    return pltpu.CompilerParams(dimension_semantics=sem, vmem_limit_bytes=VMEM_LIMIT_BYTES)


def _tile(dim, pref):
    t = min(dim, pref)
    assert dim % t == 0, (dim, pref)
    return t


def _silu(x):
    return x * jax.nn.sigmoid(x)


def _layer_norm(z, g, b):
    mu = jnp.mean(z, axis=-1, keepdims=True)
    zc = z - mu
    var = jnp.mean(zc * zc, axis=-1, keepdims=True)
    return zc * lax.rsqrt(var + LN_EPS) * g + b


def _pack_halves(lo, hi):
    lo = lax.bitcast_convert_type(lo, U32)
    hi = lax.bitcast_convert_type(hi, U32)
    return (hi & jnp.uint32(HI_MASK)) | (lo >> jnp.uint32(16))


def _pack_rows(y):
    half = y.shape[1] // 2
    yb = y.astype(BF16).astype(F32)
    return _pack_halves(yb[:, :half], yb[:, half:])


def _unpack_rows(pk):
    lo = lax.bitcast_convert_type(pk << jnp.uint32(16), F32).astype(BF16)
    hi = lax.bitcast_convert_type(pk & jnp.uint32(HI_MASK), F32).astype(BF16)
    return lo, hi


def _proj_kernel(x_ref, w_ref, o_ref, w_bf, *, scaled_tiles, scale, cast_rows):
    j = pl.program_id(0)

    @pl.when(pl.program_id(1) == 0)
    def _():
        def cast(c, carry):
            r = pl.ds(pl.multiple_of(c * cast_rows, cast_rows), cast_rows)
            w_bf[r, :] = w_ref[r, :].astype(BF16)
            return carry
        lax.fori_loop(0, w_ref.shape[0] // cast_rows, cast, 0)

    y = jnp.dot(x_ref[...].astype(BF16), w_bf[...], preferred_element_type=F32)
    s = jnp.where(j < scaled_tiles, jnp.float32(scale), jnp.float32(1.0))
    o_ref[...] = (y * s).astype(o_ref.dtype)


def _proj(x2d, w, n, *, scaled_cols, scale, tm=512, tn=1024):
    t, d = x2d.shape
    tm, tn = _tile(t, tm), _tile(scaled_cols, tn)
    assert n % tn == 0
    return pl.pallas_call(
        functools.partial(_proj_kernel, scaled_tiles=scaled_cols // tn, scale=scale, cast_rows=_tile(d, 256)),
        grid=(n // tn, t // tm),
        in_specs=[pl.BlockSpec((tm, d), lambda j, i: (i, 0)),
                  pl.BlockSpec((d, tn), lambda j, i: (0, j))],
        out_specs=pl.BlockSpec((tm, tn), lambda j, i: (i, j)),
        out_shape=jax.ShapeDtypeStruct((t, n), BF16),
        scratch_shapes=[pltpu.VMEM((d, tn), BF16)],
        compiler_params=_cparams("arbitrary", "arbitrary"),
    )(x2d, w)


def _fox_cum_kernel(x_ref, w_ref, b_ref, o_ref, carry_ref, *, heads):
    @pl.when(pl.program_id(1) == 0)
    def _():
        carry_ref[...] = jnp.zeros_like(carry_ref)

    ts = x_ref.shape[0]
    lane = lax.broadcasted_iota(I32, w_ref.shape, 1)
    w = jnp.where(lane < heads, w_ref[...], 0.0).astype(BF16)
    z = jnp.dot(x_ref[...].astype(BF16), w, preferred_element_type=F32) + b_ref[...]
    logf = -(jnp.maximum(-z, 0.0) + jnp.log1p(jnp.exp(-jnp.abs(z))))
    row = lax.broadcasted_iota(I32, (ts, ts), 0)
    col = lax.broadcasted_iota(I32, (ts, ts), 1)
    tril = (row >= col).astype(F32)
    cum = jnp.dot(tril, logf, preferred_element_type=F32, precision=lax.Precision.HIGHEST) + carry_ref[...]
    o_ref[...] = cum * LOG2E
    carry_ref[...] = cum[ts - 1:ts, :]


def _fox_cum(x2d, w_in, col0, b_f, batch, seq, ts=512):
    t, d = x2d.shape
    h = w_in.shape[1] - col0
    assert h <= LANES and col0 % LANES == 0
    ts = _tile(seq, ts)
    b_pad = jnp.zeros((1, LANES), F32).at[0, :h].set(b_f.astype(F32))
    ns = seq // ts
    return pl.pallas_call(
        functools.partial(_fox_cum_kernel, heads=h),
        grid=(batch, ns),
        in_specs=[pl.BlockSpec((ts, d), lambda b, s: (b * ns + s, 0)),
                  pl.BlockSpec((d, LANES), lambda b, s: (0, col0 // LANES)),
                  pl.BlockSpec((1, LANES), lambda b, s: (0, 0))],
        out_specs=pl.BlockSpec((ts, LANES), lambda b, s: (b * ns + s, 0)),
        out_shape=jax.ShapeDtypeStruct((t, LANES), F32),
        scratch_shapes=[pltpu.VMEM((1, LANES), F32)],
        compiler_params=_cparams("arbitrary", "arbitrary"),
    )(x2d, w_in, b_pad)


def _fox_attn_kernel(q_ref, k_ref, v_ref, c_ref, o_ref, *, tb):
    qi = pl.program_id(2)
    q = q_ref[...]
    hd = q.shape[1]

    def scores(j):
        k = k_ref[pl.ds(pl.multiple_of(j * tb, tb), tb), :]
        s = lax.dot_general(q, k, (((1,), (1,)), ((), ())), preferred_element_type=F32)
        return s - c_ref[j]

    def update(j, t, carry):
        m, l, acc = carry
        v = v_ref[pl.ds(pl.multiple_of(j * tb, tb), tb), :]
        m_new = jnp.maximum(m, jnp.max(t, axis=-1, keepdims=True))
        p = jnp.exp2(t - m_new)
        alpha = jnp.exp2(m - m_new)
        l = alpha * l + jnp.sum(p, axis=-1, keepdims=True)
        acc = alpha * acc + jnp.dot(p.astype(BF16), v, preferred_element_type=F32)
        return m_new, l, acc

    init = (jnp.full((tb, 1), -jnp.inf, F32), jnp.zeros((tb, 1), F32), jnp.zeros((tb, hd), F32))
    carry = lax.fori_loop(0, qi, lambda j, c: update(j, scores(j), c), init)
    row = lax.broadcasted_iota(I32, (tb, tb), 0)
    col = lax.broadcasted_iota(I32, (tb, tb), 1)
    t = jnp.where(row >= col, scores(qi), -jnp.inf)
    _, l, acc = update(qi, t, carry)
    o_ref[...] = (acc / l).astype(o_ref.dtype)


def _fox_attn(qkv, cum_t, batch, seq, heads, hd, tb=512):
    t = qkv.shape[0]
    d = heads * hd
    tb = _tile(seq, tb)
    nq = seq // tb
    assert hd % LANES == 0
    return pl.pallas_call(
        functools.partial(_fox_attn_kernel, tb=tb),
        grid=(batch, heads, nq),
        in_specs=[pl.BlockSpec((tb, hd), lambda b, h, i: (b * nq + i, h)),
                  pl.BlockSpec((seq, hd), lambda b, h, i: (b, heads + h)),
                  pl.BlockSpec((seq, hd), lambda b, h, i: (b, 2 * heads + h)),
                  pl.BlockSpec((None, None, nq, 1, tb), lambda b, h, i: (b, h, 0, 0, 0))],
        out_specs=pl.BlockSpec((tb, hd), lambda b, h, i: (b * nq + i, h)),
        out_shape=jax.ShapeDtypeStruct((t, d), BF16),
        compiler_params=_cparams("arbitrary", "arbitrary", "arbitrary"),
    )(qkv, qkv, qkv, cum_t)


def _oproj_ln_kernel(a_ref, w_ref, x_ref, g_ref, b_ref, o_ref, ob_ref, acc_ref, *, alpha):
    k = pl.program_id(1)

    @pl.when(k == 0)
    def _():
        acc_ref[...] = jnp.zeros_like(acc_ref)

    acc_ref[...] += jnp.dot(a_ref[...], w_ref[...], preferred_element_type=F32)

    @pl.when(k == pl.num_programs(1) - 1)
    def _():
        y = _layer_norm(alpha * x_ref[...] + acc_ref[...], g_ref[...], b_ref[...])
        o_ref[...] = y
        ob_ref[...] = y.astype(BF16)


def _oproj_ln(a_bf, w_bf, x2d, g, b, alpha, tm=512, tk=1024):
    t, d = x2d.shape
    kk = a_bf.shape[1]
    tm, tk = _tile(t, tm), _tile(kk, tk)
    return pl.pallas_call(
        functools.partial(_oproj_ln_kernel, alpha=alpha),
        grid=(t // tm, kk // tk),
        in_specs=[pl.BlockSpec((tm, tk), lambda i, k: (i, k)),
                  pl.BlockSpec((tk, d), lambda i, k: (k, 0)),
                  pl.BlockSpec((tm, d), lambda i, k: (i, 0)),
                  pl.BlockSpec((1, d), lambda i, k: (0, 0)),
                  pl.BlockSpec((1, d), lambda i, k: (0, 0))],
        out_specs=[pl.BlockSpec((tm, d), lambda i, k: (i, 0)),
                   pl.BlockSpec((tm, d), lambda i, k: (i, 0))],
        out_shape=[jax.ShapeDtypeStruct((t, d), F32), jax.ShapeDtypeStruct((t, d), BF16)],
        scratch_shapes=[pltpu.VMEM((tm, d), F32)],
        compiler_params=_cparams("arbitrary", "arbitrary"),
    )(a_bf, w_bf, x2d, g.reshape(1, d).astype(F32), b.reshape(1, d).astype(F32))


def _first_max(vals, idx, sentinel):
    m = jnp.max(vals, axis=0, keepdims=True)
    first = jnp.min(jnp.where(vals == m, idx, sentinel), axis=0, keepdims=True)
    return m, first


def _router_kernel(x_ref, rwt_ref, bias_ref, eidx_ref, wts_ref):
    e = rwt_ref.shape[0]
    tm = x_ref.shape[0]
    per = e // N_GROUPS
    logits = lax.dot_general(rwt_ref[...], x_ref[...], (((1,), (1,)), ((), ())),
                             preferred_element_type=F32, precision=lax.Precision.HIGHEST)
    scores = jax.nn.sigmoid(logits)
    choice = scores + bias_ref[...]
    neg = jnp.float32(-jnp.inf)

    ip = lax.broadcasted_iota(I32, (per, tm), 0)
    gs = []
    for g in range(N_GROUPS):
        c = choice[g * per:(g + 1) * per, :]
        m1, f1 = _first_max(c, ip, per)
        m2 = jnp.max(jnp.where(ip == f1, neg, c), axis=0, keepdims=True)
        gs.append(m1 + m2)
    gscore = jnp.concatenate(gs, axis=0)

    ig = lax.broadcasted_iota(I32, (N_GROUPS, tm), 0)
    sel = jnp.zeros((N_GROUPS, tm), F32)
    for _ in range(TOPK_GROUPS):
        _, fg = _first_max(gscore, ig, N_GROUPS)
        hit = ig == fg
        sel = jnp.where(hit, 1.0, sel)
        gscore = jnp.where(hit, neg, gscore)

    masked = jnp.concatenate(
        [jnp.where(sel[g:g + 1, :] > 0.0, choice[g * per:(g + 1) * per, :], neg) for g in range(N_GROUPS)],
        axis=0)
    ie = lax.broadcasted_iota(I32, (e, tm), 0)
    idxs, ws = [], []
    for _ in range(TOP_K):
        _, fe = _first_max(masked, ie, e)
        hit = ie == fe
        idxs.append(fe)
        ws.append(jnp.sum(jnp.where(hit, scores, 0.0), axis=0, keepdims=True))
        masked = jnp.where(hit, neg, masked)
    w = jnp.concatenate(ws, axis=0)
    eidx_ref[...] = jnp.concatenate(idxs, axis=0)
    wts_ref[...] = w / jnp.sum(w, axis=0, keepdims=True) * ROUTED_SCALE


def _router(x2d, router_w, router_bias, tm=512):
    t, d = x2d.shape
    e = router_w.shape[1]
    assert e % N_GROUPS == 0 and (e // N_GROUPS) % 8 == 0
    tm = _tile(t, tm)
    return pl.pallas_call(
        _router_kernel,
        grid=(t // tm,),
        in_specs=[pl.BlockSpec((tm, d), lambda i: (i, 0)),
                  pl.BlockSpec((e, d), lambda i: (0, 0)),
                  pl.BlockSpec((e, 1), lambda i: (0, 0))],
        out_specs=[pl.BlockSpec((TOP_K, tm), lambda i: (0, i)),
                   pl.BlockSpec((TOP_K, tm), lambda i: (0, i))],
        out_shape=[jax.ShapeDtypeStruct((TOP_K, t), I32), jax.ShapeDtypeStruct((TOP_K, t), F32)],
        compiler_params=_cparams("arbitrary"),
    )(x2d, router_w.T.astype(F32), router_bias.reshape(e, 1).astype(F32))


def _seg_copy(src, dst, src_off, dst_off, length, max_len, sem, wait=False):
    done = 0
    b = max_len
    while b >= SUBLANES:
        piece = length & b

        @pl.when(piece != 0)
        def _(b=b, done=done):
            cp = pltpu.make_async_copy(src.at[pl.ds(pl.multiple_of(src_off + done, SUBLANES), b)],
                                       dst.at[pl.ds(pl.multiple_of(dst_off + done, SUBLANES), b)], sem)
            if wait:
                cp.wait()
            else:
                cp.start()
        done = done + piece
        b //= 2


def _pow2_floor(n):
    return 1 << (n.bit_length() - 1)


def _dispatch_kernel(tab_ref, pad_ref, lpt_ref, x_ref, o_hbm, xs, zbuf, sem, *, n_exp, rc):
    i = pl.program_id(0)
    n = pl.num_programs(0)
    slot = i % 2
    tt = x_ref.shape[0]
    r_tot, half = xs.shape[1], xs.shape[2]

    def wait_slot(s, step):
        _seg_copy(xs.at[s], o_hbm, 0, 0, pad_ref[2 * n_exp + step], _pow2_floor(r_tot), sem.at[s], wait=True)

    @pl.when(i >= 2)
    def _():
        wait_slot(slot, i - 2)

    xlo, xhi = x_ref[:, :half], x_ref[:, half:]
    lpt = lpt_ref[...]
    for c in range(r_tot // rc):
        rows = lax.broadcasted_iota(I32, (rc, tt), 0) + c * rc
        hit = rows == lpt[0:1, :]
        for k in range(1, TOP_K):
            hit = hit | (rows == lpt[k:k + 1, :])
        g = jnp.where(hit, 1.0, 0.0).astype(BF16)
        lo = jnp.dot(g, xlo, preferred_element_type=F32)
        hi = jnp.dot(g, xhi, preferred_element_type=F32)
        xs[slot, c * rc:(c + 1) * rc, :] = _pack_halves(lo, hi)

    for e in range(n_exp):
        _seg_copy(xs.at[slot], o_hbm, tab_ref[2 * n_exp + e], tab_ref[e], tab_ref[n_exp + e], tt, sem.at[slot])

    @pl.when(i == 0)
    def _():
        zbuf[...] = jnp.zeros_like(zbuf)
        for w in (False, True):
            for e in range(n_exp):
                _seg_copy(zbuf, o_hbm, 0, pad_ref[e], pad_ref[n_exp + e], zbuf.shape[0] // 2, sem.at[2], wait=w)

    @pl.when(i == n - 1)
    def _():
        @pl.when(n >= 2)
        def _():
            wait_slot(1 - slot, i - 1)
        wait_slot(slot, i)


def _tile_rows(tt, n_exp, rc=512):
    r = tt * TOP_K + (SUBLANES - 1) * n_exp
    rc = min(rc, tt * TOP_K)
    return -(-r // rc) * rc, rc


def _dispatch(xbf, tab, pad_tab, lpos_t, m_pad, n_exp, tt, bm):
    t, d = xbf.shape
    assert tt & (tt - 1) == 0 and bm & (bm - 1) == 0 and d % 2 == 0 and tt % SUBLANES == 0
    r_tot, rc = _tile_rows(tt, n_exp)
    return pl.pallas_call(
        functools.partial(_dispatch_kernel, n_exp=n_exp, rc=rc),
        grid=(t // tt,),
        in_specs=[pl.BlockSpec((4 * n_exp,), lambda i: (i,), memory_space=pltpu.SMEM),
                  pl.BlockSpec(memory_space=pltpu.SMEM),
                  pl.BlockSpec((TOP_K, tt), lambda i: (0, i)),
                  pl.BlockSpec((tt, d), lambda i: (i, 0))],
        out_specs=pl.BlockSpec(memory_space=pl.ANY),
        out_shape=jax.ShapeDtypeStruct((m_pad, d // 2), U32),
        scratch_shapes=[pltpu.VMEM((2, r_tot, d // 2), U32), pltpu.VMEM((bm, d // 2), U32),
                        pltpu.SemaphoreType.DMA((3,))],
        compiler_params=_cparams("arbitrary"),
    )(tab, pad_tab, lpos_t, xbf)


def _expert_kernel(be_ref, nv_ref, xs_ref, wgu_ref, wd_ref, y_ref, wgu_bf, wd_bf, *, cast_rows):
    i = pl.program_id(0)

    @pl.when(i < nv_ref[0])
    def _():
        prev = be_ref[jnp.maximum(i - 1, 0)]

        @pl.when((i == 0) | (be_ref[i] != prev))
        def _():
            def cast_gu(c, carry):
                r = pl.ds(pl.multiple_of(c * cast_rows, cast_rows), cast_rows)
                wgu_bf[r, :] = wgu_ref[r, :].astype(BF16)
                return carry
            lax.fori_loop(0, wgu_ref.shape[0] // cast_rows, cast_gu, 0)
            wd_bf[...] = wd_ref[...].astype(BF16)

        half = xs_ref.shape[1]
        eh = wd_ref.shape[0]
        lo, hi = _unpack_rows(xs_ref[...])
        gu = (jnp.dot(lo, wgu_bf[:half, :], preferred_element_type=F32)
              + jnp.dot(hi, wgu_bf[half:, :], preferred_element_type=F32))
        a = (_silu(gu[:, :eh]) * gu[:, eh:]).astype(BF16)
        y_ref[...] = _pack_rows(jnp.dot(a, wd_bf[...], preferred_element_type=F32))


def _experts(xs, block_e, nvalid, w_gu, w_down, bm):
    m_pad, dh = xs.shape
    e, d, eh2 = w_gu.shape
    eh = w_down.shape[1]
    nb = m_pad // bm
    cast_rows = _tile(d, 256)

    def blk(i, be, nv):
        return jnp.minimum(i, nv[0] - 1)

    grid_spec = pltpu.PrefetchScalarGridSpec(
        num_scalar_prefetch=2,
        grid=(nb,),
        in_specs=[pl.BlockSpec((bm, dh), lambda i, be, nv: (blk(i, be, nv), 0)),
                  pl.BlockSpec((None, d, eh2), lambda i, be, nv: (be[blk(i, be, nv)], 0, 0)),
                  pl.BlockSpec((None, eh, d), lambda i, be, nv: (be[blk(i, be, nv)], 0, 0))],
        out_specs=pl.BlockSpec((bm, dh), lambda i, be, nv: (blk(i, be, nv), 0)),
        scratch_shapes=[pltpu.VMEM((d, eh2), BF16), pltpu.VMEM((eh, d), BF16)],
    )
    return pl.pallas_call(
        functools.partial(_expert_kernel, cast_rows=cast_rows),
        grid_spec=grid_spec,
        out_shape=jax.ShapeDtypeStruct((m_pad, dh), U32),
        compiler_params=_cparams("arbitrary"),
    )(block_e, nvalid, xs, w_gu, w_down)


def _combine_kernel(tabc_ref, tabn_ref, lp_ref, w_ref, x_ref, sgu_ref, sd_ref, g_ref, b_ref, y_hbm,
                    o_ref, ybuf, sem, *, alpha, n_exp):
    i = pl.program_id(0)
    n = pl.num_programs(0)
    tt = x_ref.shape[0]
    r_tot = ybuf.shape[1]
    slot = i % 2

    def issue(tab_ref, s):
        for e in range(n_exp):
            _seg_copy(y_hbm, ybuf.at[s], tab_ref[e], tab_ref[2 * n_exp + e], tab_ref[n_exp + e], tt, sem.at[s])

    @pl.when(i == 0)
    def _():
        ybuf[...] = jnp.zeros_like(ybuf)
        issue(tabc_ref, 0)

    @pl.when(i + 1 < n)
    def _():
        issue(tabn_ref, 1 - slot)

    _seg_copy(y_hbm, ybuf.at[slot], 0, 0, tabc_ref[3 * n_exp], _pow2_floor(r_tot), sem.at[slot], wait=True)

    lp = lp_ref[...]
    w = w_ref[...]
    cols = lax.broadcasted_iota(I32, (tt, r_tot), 1)
    p = jnp.zeros((tt, r_tot), F32)
    for k in range(TOP_K):
        p = p + jnp.where(cols == lp[:, k:k + 1], w[:, k:k + 1], 0.0)
    pb = p.astype(BF16)
    ylo, yhi = _unpack_rows(ybuf[slot])
    routed = jnp.concatenate([jnp.dot(pb, ylo, preferred_element_type=F32),
                              jnp.dot(pb, yhi, preferred_element_type=F32)], axis=1)

    x = x_ref[...]
    sh = sd_ref.shape[0]
    sg = jnp.dot(x.astype(BF16), sgu_ref[...], preferred_element_type=F32)
    a = (_silu(sg[:, :sh]) * sg[:, sh:]).astype(BF16)
    shared = jnp.dot(a, sd_ref[...], preferred_element_type=F32)
    o_ref[...] = _layer_norm(alpha * x + (routed + shared), g_ref[...], b_ref[...])


def _combine(x2d, wts, lpos, tab, y_sorted, s_gu_bf, s_down_bf, g, b, alpha, n_exp, tt):
    t, d = x2d.shape
    n = t // tt
    sh = s_down_bf.shape[0]
    return pl.pallas_call(
        functools.partial(_combine_kernel, alpha=alpha, n_exp=n_exp),
        grid=(n,),
        in_specs=[pl.BlockSpec((4 * n_exp,), lambda i: (i,), memory_space=pltpu.SMEM),
                  pl.BlockSpec((4 * n_exp,), lambda i: (jnp.minimum(i + 1, n - 1),), memory_space=pltpu.SMEM),
                  pl.BlockSpec((tt, TOP_K), lambda i: (i, 0)),
                  pl.BlockSpec((tt, TOP_K), lambda i: (i, 0)),
                  pl.BlockSpec((tt, d), lambda i: (i, 0)),
                  pl.BlockSpec((d, 2 * sh), lambda i: (0, 0)),
                  pl.BlockSpec((sh, d), lambda i: (0, 0)),
                  pl.BlockSpec((1, d), lambda i: (0, 0)),
                  pl.BlockSpec((1, d), lambda i: (0, 0)),
                  pl.BlockSpec(memory_space=pl.ANY)],
        out_specs=pl.BlockSpec((tt, d), lambda i: (i, 0)),
        out_shape=jax.ShapeDtypeStruct((t, d), F32),
        scratch_shapes=[pltpu.VMEM((2, _tile_rows(tt, n_exp)[0], d // 2), U32), pltpu.SemaphoreType.DMA((2,))],
        compiler_params=_cparams("arbitrary"),
    )(tab, tab, lpos, wts, x2d, s_gu_bf, s_down_bf,
      g.reshape(1, d).astype(F32), b.reshape(1, d).astype(F32), y_sorted)


def _moe_layer(x2d, xbf, router_w, router_bias, w_gu, w_down, s_gu, s_down, g, b, alpha, bm=512, tt=256):
    t, d = x2d.shape
    e = router_w.shape[1]
    tt = _tile(t, tt)
    nt = t // tt
    eidx_t, wts_t = _router(x2d, router_w, router_bias)
    eidx, wts = eidx_t.T, wts_t.T

    onehot = jnp.sum((eidx[:, :, None] == jnp.arange(e, dtype=I32)).astype(I32), axis=1)
    oh = onehot.reshape(nt, tt, e)
    rank_in_tile = jnp.cumsum(oh, axis=1) - oh
    cnt = jnp.sum(oh, axis=1)
    run = ((cnt + SUBLANES - 1) // SUBLANES) * SUBLANES
    before = jnp.cumsum(run, axis=0) - run
    off = jnp.cumsum(run, axis=1) - run
    counts = jnp.sum(run, axis=0)
    padded = ((counts + bm - 1) // bm) * bm
    ends = jnp.cumsum(padded)
    starts = ends - padded
    lpos = jnp.take_along_axis((off[:, None, :] + rank_in_tile).reshape(t, e), eidx, axis=1).astype(I32)
    totals = jnp.sum(run, axis=1)
    spare = jnp.zeros_like(cnt).at[:, 0].set(totals)
    tab = jnp.concatenate([starts[None, :] + before, run, off, spare], axis=1).reshape(-1).astype(I32)
    pad_tab = jnp.concatenate([starts + counts, padded - counts, totals]).astype(I32)
    m_pad = -(-(t * TOP_K + (SUBLANES - 1) * nt * e) // bm) * bm + e * bm
    nb = m_pad // bm
    block_start = jnp.arange(nb, dtype=I32) * bm
    block_e = jnp.minimum(jnp.sum((ends[None, :] <= block_start[:, None]).astype(I32), axis=1), e - 1).astype(I32)
    nvalid = (ends[-1:] // bm).astype(I32)

    xs = _dispatch(xbf, tab, pad_tab, lpos.T, m_pad, e, tt, bm)
    y_sorted = _experts(xs, block_e, nvalid, w_gu, w_down, bm)
    return _combine(x2d, wts, lpos, tab, y_sorted, s_gu.astype(BF16), s_down.astype(BF16), g, b, alpha, e, tt)


def _proj_rope_kernel(x_ref, w_ref, cos_ref, sin_ref, o_ref):
    y = jnp.dot(x_ref[...].astype(BF16), w_ref[...], preferred_element_type=F32)
    tm, n = y.shape
    tw = cos_ref.shape[1]
    lane = lax.broadcasted_iota(I32, (tm, LANES), 1)
    first_half = (lane % HD_B) < (HD_B // 2)
    for c in range(n // LANES):
        yc = y[:, c * LANES:(c + 1) * LANES]
        off = (c * LANES) % tw
        rot = jnp.where(first_half, pltpu.roll(yc, LANES - HD_B // 2, axis=1), pltpu.roll(yc, HD_B // 2, axis=1))
        yc = yc * cos_ref[:, off:off + LANES] + rot * sin_ref[:, off:off + LANES]
        o_ref[:, c * LANES:(c + 1) * LANES] = yc.astype(o_ref.dtype)


def _proj_rope(x2d, w_bf, cos, sin, seq, tm=512, tn=1024):
    t, d = x2d.shape
    n = w_bf.shape[1]
    tw = cos.shape[1]
    tm, tn = _tile(seq, tm), _tile(n, tn)
    assert tn % tw == 0 and tw % LANES == 0
    ns = seq // tm
    return pl.pallas_call(
        _proj_rope_kernel,
        grid=(n // tn, t // tm),
        in_specs=[pl.BlockSpec((tm, d), lambda j, i: (i, 0)),
                  pl.BlockSpec((d, tn), lambda j, i: (0, j)),
                  pl.BlockSpec((tm, tw), lambda j, i: (i % ns, 0)),
                  pl.BlockSpec((tm, tw), lambda j, i: (i % ns, 0))],
        out_specs=pl.BlockSpec((tm, tn), lambda j, i: (i, j)),
        out_shape=jax.ShapeDtypeStruct((t, n), BF16),
        compiler_params=_cparams("arbitrary", "arbitrary"),
    )(x2d, w_bf, cos, sin)


def _rope_tables(seq, scale, heads, plain_cols=0):
    half = HD_B // 2
    inv = 1.0 / (ROPE_THETA ** (jnp.arange(half, dtype=F32) / half))
    ang = jnp.arange(seq, dtype=F32)[:, None] * inv[None, :]
    cos, sin = jnp.cos(ang), jnp.sin(ang)
    cos_t = jnp.tile(jnp.concatenate([cos, cos], axis=1), (1, heads)) * scale
    sin_t = jnp.tile(jnp.concatenate([-sin, sin], axis=1), (1, heads)) * scale
    cos_t = jnp.concatenate([cos_t, jnp.ones((seq, plain_cols), F32)], axis=1)
    sin_t = jnp.concatenate([sin_t, jnp.zeros((seq, plain_cols), F32)], axis=1)
    return cos_t, sin_t


def _swa_kernel(q_ref, kc_ref, kp_ref, vc_ref, vp_ref, sink_ref, o_ref, *, groups):
    qi = pl.program_id(2)
    tq = q_ref.shape[0]
    w = WINDOW
    row = lax.broadcasted_iota(I32, (w, 2 * w), 0)
    col = lax.broadcasted_iota(I32, (w, 2 * w), 1)
    band = (col > row) & (col <= row + w)
    neg = jnp.float32(-jnp.inf)
    for j in range(tq // w):
        if j == 0:
            keys = [jnp.concatenate([kp_ref[s], kc_ref[s, :w, :]], axis=0) for s in range(2)]
            vals = [jnp.concatenate([vp_ref[s], vc_ref[s, :w, :]], axis=0) for s in range(2)]
            mask = band & ((col >= w) | (qi > 0))
        else:
            keys = [kc_ref[s, (j - 1) * w:(j + 1) * w, :] for s in range(2)]
            vals = [vc_ref[s, (j - 1) * w:(j + 1) * w, :] for s in range(2)]
            mask = band
        for p in range(groups // 2):
            qs = q_ref[j * w:(j + 1) * w, p * LANES:(p + 1) * LANES]
            out = jnp.zeros((w, LANES), F32)
            for s in range(2):
                sink = sink_ref[2 * p + s:2 * p + s + 1, 0:1]
                sc = lax.dot_general(qs, keys[s], (((1,), (1,)), ((), ())), preferred_element_type=F32)
                sc = jnp.where(mask, sc, neg)
                m = jnp.maximum(jnp.max(sc, axis=-1, keepdims=True), sink)
                pr = jnp.exp(sc - m)
                pr = pr * (1.0 / (jnp.sum(pr, axis=-1, keepdims=True) + jnp.exp(sink - m)))
                out = out + jnp.dot(pr.astype(BF16), vals[s], preferred_element_type=F32)
            o_ref[j * w:(j + 1) * w, p * LANES:(p + 1) * LANES] = out.astype(o_ref.dtype)


def _swa_attn(q, kk, vv, sink_tab, batch, seq, kvh, groups, tq=512):
    t = q.shape[0]
    tq = _tile(seq, tq)
    assert tq % WINDOW == 0 and groups % 2 == 0 and 2 * HD_B == LANES
    nq = seq // tq
    per = tq // WINDOW
    gw = groups * HD_B
    cur = pl.BlockSpec((None, None, 2, tq, LANES), lambda b, h, i: (b, h, 0, i, 0))
    prev = pl.BlockSpec((None, None, 2, WINDOW, LANES), lambda b, h, i: (b, h, 0, jnp.maximum(i * per - 1, 0), 0))
    return pl.pallas_call(
        functools.partial(_swa_kernel, groups=groups),
        grid=(batch, kvh, nq),
        in_specs=[pl.BlockSpec((tq, gw), lambda b, h, i: (b * nq + i, h)),
                  cur, prev, cur, prev,
                  pl.BlockSpec((None, groups, LANES), lambda b, h, i: (h, 0, 0))],
        out_specs=pl.BlockSpec((tq, gw), lambda b, h, i: (b * nq + i, h)),
        out_shape=jax.ShapeDtypeStruct((t, kvh * gw), BF16),
        compiler_params=_cparams("arbitrary", "arbitrary", "arbitrary"),
    )(q, kk, kk, vv, vv, sink_tab)


def _lane_pairs(a, batch, seq, kvh):
    a4 = a.reshape(batch, seq, kvh, HD_B).transpose(0, 2, 1, 3)
    z = jnp.zeros_like(a4)
    return jnp.stack([jnp.concatenate([a4, z], axis=-1), jnp.concatenate([z, a4], axis=-1)], axis=2)


def kernel(x, a_w_in, a_b_f, a_w_o, kv_w, b_w_q, b_sinks, b_w_o, router_w, router_bias, moe_w_gu, moe_w_down,
           shared_w_gu, shared_w_down, ln1_g, ln1_b, ln2_g, ln2_b):
    batch, seq, d = x.shape
    n_a, n_b = a_w_in.shape[0], b_w_q.shape[0]
    depth = n_a + n_b
    alpha = float((2 * depth) ** 0.25)
    heads_a = a_b_f.shape[1]
    hd_a = d // heads_a
    heads_b = b_sinks.shape[1]
    kvh = kv_w.shape[1] // (2 * HD_B)
    groups = heads_b // kvh
    t = batch * seq

    x2d = x.reshape(t, d).astype(F32)
    kk = vv = None
    cos_q, sin_q = _rope_tables(seq, 1.0 / math.sqrt(HD_B), LANES // HD_B)
    cos_k, sin_k = _rope_tables(seq, 1.0, kvh, plain_cols=kvh * HD_B)
    for i in range(depth):
        if i < n_a:
            w_in = a_w_in[i].astype(F32)
            qkv = _proj(x2d, w_in, 3 * d, scaled_cols=d, scale=LOG2E / math.sqrt(hd_a))
            cum = _fox_cum(x2d, w_in, 3 * d, a_b_f[i], batch, seq)
            tb = _tile(seq, 512)
            cum_t = cum[:, :heads_a].reshape(batch, seq, heads_a).transpose(0, 2, 1)
            cum_t = cum_t.reshape(batch, heads_a, seq // tb, 1, tb)
            attn = _fox_attn(qkv, cum_t, batch, seq, heads_a, hd_a, tb=tb)
            w_o = a_w_o[i]
        else:
            j = i - n_a
            q = _proj_rope(x2d, b_w_q[j].astype(BF16), cos_q, sin_q, seq)
            sink_tab = jnp.broadcast_to(b_sinks[j].astype(F32).reshape(kvh, groups, 1), (kvh, groups, LANES))
            attn = _swa_attn(q, kk, vv, sink_tab, batch, seq, kvh, groups)
            w_o = b_w_o[j]
        x2d, xbf = _oproj_ln(attn, w_o.astype(BF16), x2d, ln1_g[i], ln1_b[i], alpha)
        x2d = _moe_layer(x2d, xbf, router_w[i], router_bias[i], moe_w_gu[i], moe_w_down[i],
                         shared_w_gu[i], shared_w_down[i], ln2_g[i], ln2_b[i], alpha)
        if i == n_a - 1:
            kv = _proj_rope(x2d, kv_w.astype(BF16), cos_k, sin_k, seq, tn=kv_w.shape[1])
            kk = _lane_pairs(kv[:, :kvh * HD_B], batch, seq, kvh)
            vv = _lane_pairs(kv[:, kvh * HD_B:], batch, seq, kvh)
    return x2d.reshape(batch, seq, d).astype(x.dtype)
```

```python
import functools
import math

import jax
import jax.numpy as jnp
from jax import lax
from jax.experimental import pallas as pl
from jax.experimental.pallas import tpu as pltpu

F32 = jnp.float32
BF16 = jnp.bfloat16
U32 = jnp.uint32
I32 = jnp.int32

LANES = 128
SUBLANES = 8
HD_B = 64
WINDOW = 128
ROPE_THETA = 10000.0
TOP_K = 8
N_GROUPS = 8
TOPK_GROUPS = 4
ROUTED_SCALE = 2.5
LN_EPS = 1e-5
VMEM_LIMIT_BYTES = 56 * 1024 * 1024
HI_MASK = 0xFFFF0000
LOG2E = math.log2(math.e)


def _cparams(*sem):
    return pltpu.CompilerParams(dimension_semantics=sem, vmem_limit_bytes=VMEM_LIMIT_BYTES)


def _tile(dim, pref):
    t = min(dim, pref)
    assert dim % t == 0, (dim, pref)
    return t


def _silu(x):
    return x * jax.nn.sigmoid(x)


def _layer_norm(z, g, b):
    mu = jnp.mean(z, axis=-1, keepdims=True)
    zc = z - mu
    var = jnp.mean(zc * zc, axis=-1, keepdims=True)
    return zc * lax.rsqrt(var + LN_EPS) * g + b


def _pack_halves(lo, hi):
    lo = lax.bitcast_convert_type(lo, U32)
    hi = lax.bitcast_convert_type(hi, U32)
    return (hi & jnp.uint32(HI_MASK)) | (lo >> jnp.uint32(16))


def _pack_rows(y):
    half = y.shape[1] // 2
    yb = y.astype(BF16).astype(F32)
    return _pack_halves(yb[:, :half], yb[:, half:])


def _unpack_rows(pk):
    lo = lax.bitcast_convert_type(pk << jnp.uint32(16), F32).astype(BF16)
    hi = lax.bitcast_convert_type(pk & jnp.uint32(HI_MASK), F32).astype(BF16)
    return lo, hi


def _proj_kernel(x_ref, w_ref, o_ref, w_bf, *, scaled_tiles, scale, cast_rows):
    j = pl.program_id(0)

    @pl.when(pl.program_id(1) == 0)
    def _():
        def cast(c, carry):
            r = pl.ds(pl.multiple_of(c * cast_rows, cast_rows), cast_rows)
            w_bf[r, :] = w_ref[r, :].astype(BF16)
            return carry
        lax.fori_loop(0, w_ref.shape[0] // cast_rows, cast, 0)

    y = jnp.dot(x_ref[...].astype(BF16), w_bf[...], preferred_element_type=F32)
    s = jnp.where(j < scaled_tiles, jnp.float32(scale), jnp.float32(1.0))
    o_ref[...] = (y * s).astype(o_ref.dtype)


def _proj(x2d, w, layer, n, *, scaled_cols, scale, tm=512, tn=1024):
    t, d = x2d.shape
    tm, tn = _tile(t, tm), _tile(scaled_cols, tn)
    assert n % tn == 0
    return pl.pallas_call(
        functools.partial(_proj_kernel, scaled_tiles=scaled_cols // tn, scale=scale, cast_rows=_tile(d, 256)),
        grid=(n // tn, t // tm),
        in_specs=[pl.BlockSpec((tm, d), lambda j, i: (i, 0)),
                  pl.BlockSpec((None, d, tn), lambda j, i: (layer, 0, j))],
        out_specs=pl.BlockSpec((tm, tn), lambda j, i: (i, j)),
        out_shape=jax.ShapeDtypeStruct((t, n), BF16),
        scratch_shapes=[pltpu.VMEM((d, tn), BF16)],
        compiler_params=_cparams("arbitrary", "arbitrary"),
    )(x2d, w)


def _fox_cum_kernel(x_ref, w_ref, b_ref, o_ref, carry_ref, *, heads):
    @pl.when(pl.program_id(1) == 0)
    def _():
        carry_ref[...] = jnp.zeros_like(carry_ref)

    ts = x_ref.shape[0]
    lane = lax.broadcasted_iota(I32, w_ref.shape, 1)
    w = jnp.where(lane < heads, w_ref[...], 0.0).astype(BF16)
    z = jnp.dot(x_ref[...].astype(BF16), w, preferred_element_type=F32) + b_ref[...]
    logf = -(jnp.maximum(-z, 0.0) + jnp.log1p(jnp.exp(-jnp.abs(z))))
    row = lax.broadcasted_iota(I32, (ts, ts), 0)
    col = lax.broadcasted_iota(I32, (ts, ts), 1)
    tril = (row >= col).astype(F32)
    cum = jnp.dot(tril, logf, preferred_element_type=F32, precision=lax.Precision.HIGHEST) + carry_ref[...]
    o_ref[...] = cum * LOG2E
    carry_ref[...] = cum[ts - 1:ts, :]


def _fox_cum(x2d, w_in, layer, col0, b_f, batch, seq, ts=512):
    t, d = x2d.shape
    h = w_in.shape[2] - col0
    assert h <= LANES and col0 % LANES == 0
    ts = _tile(seq, ts)
    b_pad = jnp.zeros((1, LANES), F32).at[0, :h].set(b_f.astype(F32))
    ns = seq // ts
    return pl.pallas_call(
        functools.partial(_fox_cum_kernel, heads=h),
        grid=(batch, ns),
        in_specs=[pl.BlockSpec((ts, d), lambda b, s: (b * ns + s, 0)),
                  pl.BlockSpec((None, d, LANES), lambda b, s: (layer, 0, col0 // LANES)),
                  pl.BlockSpec((1, LANES), lambda b, s: (0, 0))],
        out_specs=pl.BlockSpec((ts, LANES), lambda b, s: (b * ns + s, 0)),
        out_shape=jax.ShapeDtypeStruct((t, LANES), F32),
        scratch_shapes=[pltpu.VMEM((1, LANES), F32)],
        compiler_params=_cparams("arbitrary", "arbitrary"),
    )(x2d, w_in, b_pad)


def _fox_attn_kernel(q_ref, k_ref, v_ref, c_ref, o_ref, *, tb):
    qi = pl.program_id(2)
    q = q_ref[...]
    hd = q.shape[1]

    def scores(j):
        k = k_ref[pl.ds(pl.multiple_of(j * tb, tb), tb), :]
        s = lax.dot_general(q, k, (((1,), (1,)), ((), ())), preferred_element_type=F32)
        return s - c_ref[j]

    def update(j, t, carry):
        m, l, acc = carry
        v = v_ref[pl.ds(pl.multiple_of(j * tb, tb), tb), :]
        m_new = jnp.maximum(m, jnp.max(t, axis=-1, keepdims=True))
        p = jnp.exp2(t - m_new)
        alpha = jnp.exp2(m - m_new)
        l = alpha * l + jnp.sum(p, axis=-1, keepdims=True)
        acc = alpha * acc + jnp.dot(p.astype(BF16), v, preferred_element_type=F32)
        return m_new, l, acc

    init = (jnp.full((tb, 1), -jnp.inf, F32), jnp.zeros((tb, 1), F32), jnp.zeros((tb, hd), F32))
    carry = lax.fori_loop(0, qi, lambda j, c: update(j, scores(j), c), init)
    row = lax.broadcasted_iota(I32, (tb, tb), 0)
    col = lax.broadcasted_iota(I32, (tb, tb), 1)
    t = jnp.where(row >= col, scores(qi), -jnp.inf)
    _, l, acc = update(qi, t, carry)
    o_ref[...] = (acc / l).astype(o_ref.dtype)


def _fox_attn(qkv, cum_t, batch, seq, heads, hd, tb=512):
    t = qkv.shape[0]
    d = heads * hd
    tb = _tile(seq, tb)
    nq = seq // tb
    assert hd % LANES == 0
    return pl.pallas_call(
        functools.partial(_fox_attn_kernel, tb=tb),
        grid=(batch, heads, nq),
        in_specs=[pl.BlockSpec((tb, hd), lambda b, h, i: (b * nq + i, h)),
                  pl.BlockSpec((seq, hd), lambda b, h, i: (b, heads + h)),
                  pl.BlockSpec((seq, hd), lambda b, h, i: (b, 2 * heads + h)),
                  pl.BlockSpec((None, None, nq, 1, tb), lambda b, h, i: (b, h, 0, 0, 0))],
        out_specs=pl.BlockSpec((tb, hd), lambda b, h, i: (b * nq + i, h)),
        out_shape=jax.ShapeDtypeStruct((t, d), BF16),
        compiler_params=_cparams("arbitrary", "arbitrary", "arbitrary"),
    )(qkv, qkv, qkv, cum_t)


def _oproj_ln_kernel(a_ref, w_ref, x_ref, g_ref, b_ref, o_ref, ob_ref, acc_ref, *, alpha):
    k = pl.program_id(1)

    @pl.when(k == 0)
    def _():
        acc_ref[...] = jnp.zeros_like(acc_ref)

    acc_ref[...] += jnp.dot(a_ref[...], w_ref[...], preferred_element_type=F32)

    @pl.when(k == pl.num_programs(1) - 1)
    def _():
        y = _layer_norm(alpha * x_ref[...] + acc_ref[...], g_ref[...], b_ref[...])
        o_ref[...] = y
        ob_ref[...] = y.astype(BF16)


def _oproj_ln(a_bf, w_bf, x2d, g, b, alpha, tm=512, tk=1024):
    t, d = x2d.shape
    kk = a_bf.shape[1]
    tm, tk = _tile(t, tm), _tile(kk, tk)
    return pl.pallas_call(
        functools.partial(_oproj_ln_kernel, alpha=alpha),
        grid=(t // tm, kk // tk),
        in_specs=[pl.BlockSpec((tm, tk), lambda i, k: (i, k)),
                  pl.BlockSpec((tk, d), lambda i, k: (k, 0)),
                  pl.BlockSpec((tm, d), lambda i, k: (i, 0)),
                  pl.BlockSpec((1, d), lambda i, k: (0, 0)),
                  pl.BlockSpec((1, d), lambda i, k: (0, 0))],
        out_specs=[pl.BlockSpec((tm, d), lambda i, k: (i, 0)),
                   pl.BlockSpec((tm, d), lambda i, k: (i, 0))],
        out_shape=[jax.ShapeDtypeStruct((t, d), F32), jax.ShapeDtypeStruct((t, d), BF16)],
        scratch_shapes=[pltpu.VMEM((tm, d), F32)],
        compiler_params=_cparams("arbitrary", "arbitrary"),
    )(a_bf, w_bf, x2d, g.reshape(1, d).astype(F32), b.reshape(1, d).astype(F32))


def _first_max(vals, idx, sentinel):
    m = jnp.max(vals, axis=0, keepdims=True)
    first = jnp.min(jnp.where(vals == m, idx, sentinel), axis=0, keepdims=True)
    return m, first


def _router_kernel(x_ref, rwt_ref, bias_ref, eidx_ref, wts_ref):
    e = rwt_ref.shape[0]
    tm = x_ref.shape[0]
    per = e // N_GROUPS
    logits = lax.dot_general(rwt_ref[...], x_ref[...], (((1,), (1,)), ((), ())),
                             preferred_element_type=F32, precision=lax.Precision.HIGHEST)
    scores = jax.nn.sigmoid(logits)
    choice = scores + bias_ref[...]
    neg = jnp.float32(-jnp.inf)

    ip = lax.broadcasted_iota(I32, (per, tm), 0)
    gs = []
    for g in range(N_GROUPS):
        c = choice[g * per:(g + 1) * per, :]
        m1, f1 = _first_max(c, ip, per)
        m2 = jnp.max(jnp.where(ip == f1, neg, c), axis=0, keepdims=True)
        gs.append(m1 + m2)
    gscore = jnp.concatenate(gs, axis=0)

    ig = lax.broadcasted_iota(I32, (N_GROUPS, tm), 0)
    sel = jnp.zeros((N_GROUPS, tm), F32)
    for _ in range(TOPK_GROUPS):
        _, fg = _first_max(gscore, ig, N_GROUPS)
        hit = ig == fg
        sel = jnp.where(hit, 1.0, sel)
        gscore = jnp.where(hit, neg, gscore)

    masked = jnp.concatenate(
        [jnp.where(sel[g:g + 1, :] > 0.0, choice[g * per:(g + 1) * per, :], neg) for g in range(N_GROUPS)],
        axis=0)
    ie = lax.broadcasted_iota(I32, (e, tm), 0)
    idxs, ws = [], []
    for _ in range(TOP_K):
        _, fe = _first_max(masked, ie, e)
        hit = ie == fe
        idxs.append(fe)
        ws.append(jnp.sum(jnp.where(hit, scores, 0.0), axis=0, keepdims=True))
        masked = jnp.where(hit, neg, masked)
    w = jnp.concatenate(ws, axis=0)
    eidx_ref[...] = jnp.concatenate(idxs, axis=0)
    wts_ref[...] = w / jnp.sum(w, axis=0, keepdims=True) * ROUTED_SCALE


def _router(x2d, router_w, router_bias, tm=512):
    t, d = x2d.shape
    e = router_w.shape[1]
    assert e % N_GROUPS == 0 and (e // N_GROUPS) % 8 == 0
    tm = _tile(t, tm)
    return pl.pallas_call(
        _router_kernel,
        grid=(t // tm,),
        in_specs=[pl.BlockSpec((tm, d), lambda i: (i, 0)),
                  pl.BlockSpec((e, d), lambda i: (0, 0)),
                  pl.BlockSpec((e, 1), lambda i: (0, 0))],
        out_specs=[pl.BlockSpec((TOP_K, tm), lambda i: (0, i)),
                   pl.BlockSpec((TOP_K, tm), lambda i: (0, i))],
        out_shape=[jax.ShapeDtypeStruct((TOP_K, t), I32), jax.ShapeDtypeStruct((TOP_K, t), F32)],
        compiler_params=_cparams("arbitrary"),
    )(x2d, router_w.T.astype(F32), router_bias.reshape(e, 1).astype(F32))


def _seg_copy(src, dst, src_off, dst_off, length, max_len, sem, wait=False):
    done = 0
    b = max_len
    while b >= SUBLANES:
        piece = length & b

        @pl.when(piece != 0)
        def _(b=b, done=done):
            cp = pltpu.make_async_copy(src.at[pl.ds(pl.multiple_of(src_off + done, SUBLANES), b)],
                                       dst.at[pl.ds(pl.multiple_of(dst_off + done, SUBLANES), b)], sem)
            if wait:
                cp.wait()
            else:
                cp.start()
        done = done + piece
        b //= 2


def _pow2_floor(n):
    return 1 << (n.bit_length() - 1)


def _dispatch_kernel(tab_ref, pad_ref, lpt_ref, x_ref, o_hbm, xs, zbuf, sem, *, n_exp, rc):
    i = pl.program_id(0)
    n = pl.num_programs(0)
    slot = i % 2
    tt = x_ref.shape[0]
    r_tot, half = xs.shape[1], xs.shape[2]

    def wait_slot(s, step):
        _seg_copy(xs.at[s], o_hbm, 0, 0, pad_ref[2 * n_exp + step], _pow2_floor(r_tot), sem.at[s], wait=True)

    @pl.when(i >= 2)
    def _():
        wait_slot(slot, i - 2)

    xlo, xhi = x_ref[:, :half], x_ref[:, half:]
    lpt = lpt_ref[...]
    for c in range(r_tot // rc):
        rows = lax.broadcasted_iota(I32, (rc, tt), 0) + c * rc
        hit = rows == lpt[0:1, :]
        for k in range(1, TOP_K):
            hit = hit | (rows == lpt[k:k + 1, :])
        g = jnp.where(hit, 1.0, 0.0).astype(BF16)
        lo = jnp.dot(g, xlo, preferred_element_type=F32)
        hi = jnp.dot(g, xhi, preferred_element_type=F32)
        xs[slot, c * rc:(c + 1) * rc, :] = _pack_halves(lo, hi)

    for e in range(n_exp):
        _seg_copy(xs.at[slot], o_hbm, tab_ref[2 * n_exp + e], tab_ref[e], tab_ref[n_exp + e], tt, sem.at[slot])

    @pl.when(i == 0)
    def _():
        zbuf[...] = jnp.zeros_like(zbuf)
        for w in (False, True):
            for e in range(n_exp):
                _seg_copy(zbuf, o_hbm, 0, pad_ref[e], pad_ref[n_exp + e], zbuf.shape[0] // 2, sem.at[2], wait=w)

    @pl.when(i == n - 1)
    def _():
        @pl.when(n >= 2)
        def _():
            wait_slot(1 - slot, i - 1)
        wait_slot(slot, i)


def _tile_rows(tt, n_exp, rc=512):
    r = tt * TOP_K + (SUBLANES - 1) * n_exp
    rc = min(rc, tt * TOP_K)
    return -(-r // rc) * rc, rc


def _dispatch(xbf, tab, pad_tab, lpos_t, m_pad, n_exp, tt, bm):
    t, d = xbf.shape
    assert tt & (tt - 1) == 0 and bm & (bm - 1) == 0 and d % 2 == 0 and tt % SUBLANES == 0
    r_tot, rc = _tile_rows(tt, n_exp)
    return pl.pallas_call(
        functools.partial(_dispatch_kernel, n_exp=n_exp, rc=rc),
        grid=(t // tt,),
        in_specs=[pl.BlockSpec((4 * n_exp,), lambda i: (i,), memory_space=pltpu.SMEM),
                  pl.BlockSpec(memory_space=pltpu.SMEM),
                  pl.BlockSpec((TOP_K, tt), lambda i: (0, i)),
                  pl.BlockSpec((tt, d), lambda i: (i, 0))],
        out_specs=pl.BlockSpec(memory_space=pl.ANY),
        out_shape=jax.ShapeDtypeStruct((m_pad, d // 2), U32),
        scratch_shapes=[pltpu.VMEM((2, r_tot, d // 2), U32), pltpu.VMEM((bm, d // 2), U32),
                        pltpu.SemaphoreType.DMA((3,))],
        compiler_params=_cparams("arbitrary"),
    )(tab, pad_tab, lpos_t, xbf)


def _expert_kernel(be_ref, nv_ref, xs_ref, wgu_ref, wd_ref, y_ref, wgu_bf, wd_bf, *, cast_rows):
    i = pl.program_id(0)

    @pl.when(i < nv_ref[0])
    def _():
        prev = be_ref[jnp.maximum(i - 1, 0)]

        @pl.when((i == 0) | (be_ref[i] != prev))
        def _():
            def cast_gu(c, carry):
                r = pl.ds(pl.multiple_of(c * cast_rows, cast_rows), cast_rows)
                wgu_bf[r, :] = wgu_ref[r, :].astype(BF16)
                return carry
            lax.fori_loop(0, wgu_ref.shape[0] // cast_rows, cast_gu, 0)
            wd_bf[...] = wd_ref[...].astype(BF16)

        half = xs_ref.shape[1]
        eh = wd_ref.shape[0]
        lo, hi = _unpack_rows(xs_ref[...])
        gu = (jnp.dot(lo, wgu_bf[:half, :], preferred_element_type=F32)
              + jnp.dot(hi, wgu_bf[half:, :], preferred_element_type=F32))
        a = (_silu(gu[:, :eh]) * gu[:, eh:]).astype(BF16)
        y_ref[...] = _pack_rows(jnp.dot(a, wd_bf[...], preferred_element_type=F32))


def _experts(xs, block_e, nvalid, w_gu, w_down, layer, bm):
    m_pad, dh = xs.shape
    _, e, d, eh2 = w_gu.shape
    eh = w_down.shape[2]
    nb = m_pad // bm
    cast_rows = _tile(d, 256)

    def blk(i, be, nv):
        return jnp.minimum(i, nv[0] - 1)

    grid_spec = pltpu.PrefetchScalarGridSpec(
        num_scalar_prefetch=2,
        grid=(nb,),
        in_specs=[pl.BlockSpec((bm, dh), lambda i, be, nv: (blk(i, be, nv), 0)),
                  pl.BlockSpec((None, None, d, eh2), lambda i, be, nv: (layer, be[blk(i, be, nv)], 0, 0)),
                  pl.BlockSpec((None, None, eh, d), lambda i, be, nv: (layer, be[blk(i, be, nv)], 0, 0))],
        out_specs=pl.BlockSpec((bm, dh), lambda i, be, nv: (blk(i, be, nv), 0)),
        scratch_shapes=[pltpu.VMEM((d, eh2), BF16), pltpu.VMEM((eh, d), BF16)],
    )
    return pl.pallas_call(
        functools.partial(_expert_kernel, cast_rows=cast_rows),
        grid_spec=grid_spec,
        out_shape=jax.ShapeDtypeStruct((m_pad, dh), U32),
        compiler_params=_cparams("arbitrary"),
    )(block_e, nvalid, xs, w_gu, w_down)


def _combine_kernel(tabc_ref, tabn_ref, lp_ref, w_ref, x_ref, sgu_ref, sd_ref, g_ref, b_ref, y_hbm,
                    o_ref, ybuf, sem, *, alpha, n_exp):
    i = pl.program_id(0)
    n = pl.num_programs(0)
    tt = x_ref.shape[0]
    r_tot = ybuf.shape[1]
    slot = i % 2

    def issue(tab_ref, s):
        for e in range(n_exp):
            _seg_copy(y_hbm, ybuf.at[s], tab_ref[e], tab_ref[2 * n_exp + e], tab_ref[n_exp + e], tt, sem.at[s])

    @pl.when(i == 0)
    def _():
        ybuf[...] = jnp.zeros_like(ybuf)
        issue(tabc_ref, 0)

    @pl.when(i + 1 < n)
    def _():
        issue(tabn_ref, 1 - slot)

    _seg_copy(y_hbm, ybuf.at[slot], 0, 0, tabc_ref[3 * n_exp], _pow2_floor(r_tot), sem.at[slot], wait=True)

    lp = lp_ref[...]
    w = w_ref[...]
    cols = lax.broadcasted_iota(I32, (tt, r_tot), 1)
    p = jnp.zeros((tt, r_tot), F32)
    for k in range(TOP_K):
        p = p + jnp.where(cols == lp[:, k:k + 1], w[:, k:k + 1], 0.0)
    pb = p.astype(BF16)
    ylo, yhi = _unpack_rows(ybuf[slot])
    routed = jnp.concatenate([jnp.dot(pb, ylo, preferred_element_type=F32),
                              jnp.dot(pb, yhi, preferred_element_type=F32)], axis=1)

    x = x_ref[...]
    sh = sd_ref.shape[0]
    sg = jnp.dot(x.astype(BF16), sgu_ref[...], preferred_element_type=F32)
    a = (_silu(sg[:, :sh]) * sg[:, sh:]).astype(BF16)
    shared = jnp.dot(a, sd_ref[...], preferred_element_type=F32)
    o_ref[...] = _layer_norm(alpha * x + (routed + shared), g_ref[...], b_ref[...])


def _combine(x2d, wts, lpos, tab, y_sorted, s_gu_bf, s_down_bf, g, b, alpha, n_exp, tt):
    t, d = x2d.shape
    n = t // tt
    sh = s_down_bf.shape[0]
    return pl.pallas_call(
        functools.partial(_combine_kernel, alpha=alpha, n_exp=n_exp),
        grid=(n,),
        in_specs=[pl.BlockSpec((4 * n_exp,), lambda i: (i,), memory_space=pltpu.SMEM),
                  pl.BlockSpec((4 * n_exp,), lambda i: (jnp.minimum(i + 1, n - 1),), memory_space=pltpu.SMEM),
                  pl.BlockSpec((tt, TOP_K), lambda i: (i, 0)),
                  pl.BlockSpec((tt, TOP_K), lambda i: (i, 0)),
                  pl.BlockSpec((tt, d), lambda i: (i, 0)),
                  pl.BlockSpec((d, 2 * sh), lambda i: (0, 0)),
                  pl.BlockSpec((sh, d), lambda i: (0, 0)),
                  pl.BlockSpec((1, d), lambda i: (0, 0)),
                  pl.BlockSpec((1, d), lambda i: (0, 0)),
                  pl.BlockSpec(memory_space=pl.ANY)],
        out_specs=pl.BlockSpec((tt, d), lambda i: (i, 0)),
        out_shape=jax.ShapeDtypeStruct((t, d), F32),
        scratch_shapes=[pltpu.VMEM((2, _tile_rows(tt, n_exp)[0], d // 2), U32), pltpu.SemaphoreType.DMA((2,))],
        compiler_params=_cparams("arbitrary"),
    )(tab, tab, lpos, wts, x2d, s_gu_bf, s_down_bf,
      g.reshape(1, d).astype(F32), b.reshape(1, d).astype(F32), y_sorted)


def _moe_layer(x2d, xbf, router_w, router_bias, w_gu, w_down, layer, s_gu, s_down, g, b, alpha, bm=512, tt=256):
    t, d = x2d.shape
    e = router_w.shape[1]
    tt = _tile(t, tt)
    nt = t // tt
    eidx_t, wts_t = _router(x2d, router_w, router_bias)
    eidx, wts = eidx_t.T, wts_t.T

    onehot = jnp.sum((eidx[:, :, None] == jnp.arange(e, dtype=I32)).astype(I32), axis=1)
    oh = onehot.reshape(nt, tt, e)
    rank_in_tile = jnp.cumsum(oh, axis=1) - oh
    cnt = jnp.sum(oh, axis=1)
    run = ((cnt + SUBLANES - 1) // SUBLANES) * SUBLANES
    before = jnp.cumsum(run, axis=0) - run
    off = jnp.cumsum(run, axis=1) - run
    counts = jnp.sum(run, axis=0)
    padded = ((counts + bm - 1) // bm) * bm
    ends = jnp.cumsum(padded)
    starts = ends - padded
    lpos = jnp.take_along_axis((off[:, None, :] + rank_in_tile).reshape(t, e), eidx, axis=1).astype(I32)
    totals = jnp.sum(run, axis=1)
    spare = jnp.zeros_like(cnt).at[:, 0].set(totals)
    tab = jnp.concatenate([starts[None, :] + before, run, off, spare], axis=1).reshape(-1).astype(I32)
    pad_tab = jnp.concatenate([starts + counts, padded - counts, totals]).astype(I32)
    m_pad = -(-(t * TOP_K + (SUBLANES - 1) * nt * e) // bm) * bm + e * bm
    nb = m_pad // bm
    block_start = jnp.arange(nb, dtype=I32) * bm
    block_e = jnp.minimum(jnp.sum((ends[None, :] <= block_start[:, None]).astype(I32), axis=1), e - 1).astype(I32)
    nvalid = (ends[-1:] // bm).astype(I32)

    xs = _dispatch(xbf, tab, pad_tab, lpos.T, m_pad, e, tt, bm)
    y_sorted = _experts(xs, block_e, nvalid, w_gu, w_down, layer, bm)
    return _combine(x2d, wts, lpos, tab, y_sorted, s_gu.astype(BF16), s_down.astype(BF16), g, b, alpha, e, tt)


def _proj_rope_kernel(x_ref, w_ref, cos_ref, sin_ref, o_ref):
    y = jnp.dot(x_ref[...].astype(BF16), w_ref[...], preferred_element_type=F32)
    tm, n = y.shape
    tw = cos_ref.shape[1]
    lane = lax.broadcasted_iota(I32, (tm, LANES), 1)
    first_half = (lane % HD_B) < (HD_B // 2)
    for c in range(n // LANES):
        yc = y[:, c * LANES:(c + 1) * LANES]
        off = (c * LANES) % tw
        rot = jnp.where(first_half, pltpu.roll(yc, LANES - HD_B // 2, axis=1), pltpu.roll(yc, HD_B // 2, axis=1))
        yc = yc * cos_ref[:, off:off + LANES] + rot * sin_ref[:, off:off + LANES]
        o_ref[:, c * LANES:(c + 1) * LANES] = yc.astype(o_ref.dtype)


def _proj_rope(x2d, w_bf, cos, sin, seq, tm=512, tn=1024):
    t, d = x2d.shape
    n = w_bf.shape[1]
    tw = cos.shape[1]
    tm, tn = _tile(seq, tm), _tile(n, tn)
    assert tn % tw == 0 and tw % LANES == 0
    ns = seq // tm
    return pl.pallas_call(
        _proj_rope_kernel,
        grid=(n // tn, t // tm),
        in_specs=[pl.BlockSpec((tm, d), lambda j, i: (i, 0)),
                  pl.BlockSpec((d, tn), lambda j, i: (0, j)),
                  pl.BlockSpec((tm, tw), lambda j, i: (i % ns, 0)),
                  pl.BlockSpec((tm, tw), lambda j, i: (i % ns, 0))],
        out_specs=pl.BlockSpec((tm, tn), lambda j, i: (i, j)),
        out_shape=jax.ShapeDtypeStruct((t, n), BF16),
        compiler_params=_cparams("arbitrary", "arbitrary"),
    )(x2d, w_bf, cos, sin)


def _rope_tables(seq, scale, heads, plain_cols=0):
    half = HD_B // 2
    inv = 1.0 / (ROPE_THETA ** (jnp.arange(half, dtype=F32) / half))
    ang = jnp.arange(seq, dtype=F32)[:, None] * inv[None, :]
    cos, sin = jnp.cos(ang), jnp.sin(ang)
    cos_t = jnp.tile(jnp.concatenate([cos, cos], axis=1), (1, heads)) * scale
    sin_t = jnp.tile(jnp.concatenate([-sin, sin], axis=1), (1, heads)) * scale
    cos_t = jnp.concatenate([cos_t, jnp.ones((seq, plain_cols), F32)], axis=1)
    sin_t = jnp.concatenate([sin_t, jnp.zeros((seq, plain_cols), F32)], axis=1)
    return cos_t, sin_t


def _swa_kernel(q_ref, kc_ref, kp_ref, vc_ref, vp_ref, sink_ref, o_ref, *, groups):
    qi = pl.program_id(2)
    tq = q_ref.shape[0]
    w = WINDOW
    row = lax.broadcasted_iota(I32, (w, 2 * w), 0)
    col = lax.broadcasted_iota(I32, (w, 2 * w), 1)
    band = (col > row) & (col <= row + w)
    neg = jnp.float32(-jnp.inf)
    sink = jnp.stack([sink_ref[g:g + 1, 0:1] for g in range(groups)])
    for j in range(tq // w):
        if j == 0:
            keys = [jnp.concatenate([kp_ref[s], kc_ref[s, :w, :]], axis=0) for s in range(2)]
            vals = [jnp.concatenate([vp_ref[s], vc_ref[s, :w, :]], axis=0) for s in range(2)]
            mask = band & ((col >= w) | (qi > 0))
        else:
            keys = [kc_ref[s, (j - 1) * w:(j + 1) * w, :] for s in range(2)]
            vals = [vc_ref[s, (j - 1) * w:(j + 1) * w, :] for s in range(2)]
            mask = band
        scs = []
        for p in range(groups // 2):
            qs = q_ref[j * w:(j + 1) * w, p * LANES:(p + 1) * LANES]
            for s in range(2):
                scs.append(lax.dot_general(qs, keys[s], (((1,), (1,)), ((), ())), preferred_element_type=F32))
        sc = jnp.where(mask[None], jnp.stack(scs), neg)
        m = jnp.maximum(jnp.max(jnp.maximum(sc[..., :w], sc[..., w:]), axis=-1, keepdims=True), sink)
        pr = jnp.exp(sc - m)
        den = jnp.sum(pr[..., :w] + pr[..., w:], axis=-1, keepdims=True) + jnp.exp(sink - m)
        pr = (pr * (1.0 / den)).astype(BF16)
        for p in range(groups // 2):
            out = (jnp.dot(pr[2 * p], vals[0], preferred_element_type=F32)
                   + jnp.dot(pr[2 * p + 1], vals[1], preferred_element_type=F32))
            o_ref[j * w:(j + 1) * w, p * LANES:(p + 1) * LANES] = out.astype(o_ref.dtype)


def _swa_attn(q, kk, vv, sink_tab, batch, seq, kvh, groups, tq=512):
    t = q.shape[0]
    tq = _tile(seq, tq)
    assert tq % WINDOW == 0 and groups % 2 == 0 and 2 * HD_B == LANES
    nq = seq // tq
    per = tq // WINDOW
    gw = groups * HD_B
    cur = pl.BlockSpec((None, None, 2, tq, LANES), lambda b, h, i: (b, h, 0, i, 0))
    prev = pl.BlockSpec((None, None, 2, WINDOW, LANES), lambda b, h, i: (b, h, 0, jnp.maximum(i * per - 1, 0), 0))
    return pl.pallas_call(
        functools.partial(_swa_kernel, groups=groups),
        grid=(batch, kvh, nq),
        in_specs=[pl.BlockSpec((tq, gw), lambda b, h, i: (b * nq + i, h)),
                  cur, prev, cur, prev,
                  pl.BlockSpec((None, groups, LANES), lambda b, h, i: (h, 0, 0))],
        out_specs=pl.BlockSpec((tq, gw), lambda b, h, i: (b * nq + i, h)),
        out_shape=jax.ShapeDtypeStruct((t, kvh * gw), BF16),
        compiler_params=_cparams("arbitrary", "arbitrary", "arbitrary"),
    )(q, kk, kk, vv, vv, sink_tab)


def _lane_pairs(a, batch, seq, kvh):
    a4 = a.reshape(batch, seq, kvh, HD_B).transpose(0, 2, 1, 3)
    z = jnp.zeros_like(a4)
    return jnp.stack([jnp.concatenate([a4, z], axis=-1), jnp.concatenate([z, a4], axis=-1)], axis=2)


def kernel(x, a_w_in, a_b_f, a_w_o, kv_w, b_w_q, b_sinks, b_w_o, router_w, router_bias, moe_w_gu, moe_w_down,
           shared_w_gu, shared_w_down, ln1_g, ln1_b, ln2_g, ln2_b):
    batch, seq, d = x.shape
    n_a, n_b = a_w_in.shape[0], b_w_q.shape[0]
    depth = n_a + n_b
    alpha = float((2 * depth) ** 0.25)
    heads_a = a_b_f.shape[1]
    hd_a = d // heads_a
    heads_b = b_sinks.shape[1]
    kvh = kv_w.shape[1] // (2 * HD_B)
    groups = heads_b // kvh
    t = batch * seq

    x2d = x.reshape(t, d).astype(F32)
    kk = vv = None
    cos_q, sin_q = _rope_tables(seq, 1.0 / math.sqrt(HD_B), LANES // HD_B)
    cos_k, sin_k = _rope_tables(seq, 1.0, kvh, plain_cols=kvh * HD_B)
    for i in range(depth):
        if i < n_a:
            qkv = _proj(x2d, a_w_in, i, 3 * d, scaled_cols=d, scale=LOG2E / math.sqrt(hd_a))
            cum = _fox_cum(x2d, a_w_in, i, 3 * d, a_b_f[i], batch, seq)
            tb = _tile(seq, 512)
            cum_t = cum[:, :heads_a].reshape(batch, seq, heads_a).transpose(0, 2, 1)
            cum_t = cum_t.reshape(batch, heads_a, seq // tb, 1, tb)
            attn = _fox_attn(qkv, cum_t, batch, seq, heads_a, hd_a, tb=tb)
            w_o = a_w_o[i]
        else:
            j = i - n_a
            q = _proj_rope(x2d, b_w_q[j].astype(BF16), cos_q, sin_q, seq)
            sink_tab = jnp.broadcast_to(b_sinks[j].astype(F32).reshape(kvh, groups, 1), (kvh, groups, LANES))
            attn = _swa_attn(q, kk, vv, sink_tab, batch, seq, kvh, groups)
            w_o = b_w_o[j]
        x2d, xbf = _oproj_ln(attn, w_o.astype(BF16), x2d, ln1_g[i], ln1_b[i], alpha)
        x2d = _moe_layer(x2d, xbf, router_w[i], router_bias[i], moe_w_gu, moe_w_down, i,
                         shared_w_gu[i], shared_w_down[i], ln2_g[i], ln2_b[i], alpha)
        if i == n_a - 1:
            kv = _proj_rope(x2d, kv_w.astype(BF16), cos_k, sin_k, seq, tn=kv_w.shape[1])
            kk = _lane_pairs(kv[:, :kvh * HD_B], batch, seq, kvh)
            vv = _lane_pairs(kv[:, kvh * HD_B:], batch, seq, kvh)
    return x2d.reshape(batch, seq, d).astype(x.dtype)
```

```python
import functools
import math

import jax
import jax.numpy as jnp
from jax import lax
from jax.experimental import pallas as pl
from jax.experimental.pallas import tpu as pltpu

F32 = jnp.float32
BF16 = jnp.bfloat16
U32 = jnp.uint32
I32 = jnp.int32

LANES = 128
SUBLANES = 8
HD_B = 64
WINDOW = 128
ROPE_THETA = 10000.0
TOP_K = 8
N_GROUPS = 8
TOPK_GROUPS = 4
ROUTED_SCALE = 2.5
LN_EPS = 1e-5
VMEM_LIMIT_BYTES = 56 * 1024 * 1024
HI_MASK = 0xFFFF0000
LOG2E = math.log2(math.e)


def _cparams(*sem):
    return pltpu.CompilerParams(dimension_semantics=sem, vmem_limit_bytes=VMEM_LIMIT_BYTES)


def _tile(dim, pref):
    t = min(dim, pref)
    assert dim % t == 0, (dim, pref)
    return t


def _silu(x):
    return x * jax.nn.sigmoid(x)


def _layer_norm(z, g, b):
    mu = jnp.mean(z, axis=-1, keepdims=True)
    zc = z - mu
    var = jnp.mean(zc * zc, axis=-1, keepdims=True)
    return zc * lax.rsqrt(var + LN_EPS) * g + b


def _pack_halves(lo, hi):
    lo = lax.bitcast_convert_type(lo, U32)
    hi = lax.bitcast_convert_type(hi, U32)
    return (hi & jnp.uint32(HI_MASK)) | (lo >> jnp.uint32(16))


def _pack_rows(y):
    half = y.shape[1] // 2
    yb = y.astype(BF16).astype(F32)
    return _pack_halves(yb[:, :half], yb[:, half:])


def _unpack_rows(pk):
    lo = lax.bitcast_convert_type(pk << jnp.uint32(16), F32).astype(BF16)
    hi = lax.bitcast_convert_type(pk & jnp.uint32(HI_MASK), F32).astype(BF16)
    return lo, hi


def _proj_kernel(x_ref, w_ref, o_ref, w_bf, *, scaled_tiles, scale, cast_rows):
    j = pl.program_id(0)

    @pl.when(pl.program_id(1) == 0)
    def _():
        def cast(c, carry):
            r = pl.ds(pl.multiple_of(c * cast_rows, cast_rows), cast_rows)
            w_bf[r, :] = w_ref[r, :].astype(BF16)
            return carry
        lax.fori_loop(0, w_ref.shape[0] // cast_rows, cast, 0)

    y = jnp.dot(x_ref[...].astype(BF16), w_bf[...], preferred_element_type=F32)
    s = jnp.where(j < scaled_tiles, jnp.float32(scale), jnp.float32(1.0))
    o_ref[...] = (y * s).astype(o_ref.dtype)


def _proj(x2d, w, layer, n, *, scaled_cols, scale, tm=512, tn=1024):
    t, d = x2d.shape
    tm, tn = _tile(t, tm), _tile(scaled_cols, tn)
    assert n % tn == 0
    return pl.pallas_call(
        functools.partial(_proj_kernel, scaled_tiles=scaled_cols // tn, scale=scale, cast_rows=_tile(d, 256)),
        grid=(n // tn, t // tm),
        in_specs=[pl.BlockSpec((tm, d), lambda j, i: (i, 0)),
                  pl.BlockSpec((None, d, tn), lambda j, i: (layer, 0, j))],
        out_specs=pl.BlockSpec((tm, tn), lambda j, i: (i, j)),
        out_shape=jax.ShapeDtypeStruct((t, n), BF16),
        scratch_shapes=[pltpu.VMEM((d, tn), BF16)],
        compiler_params=_cparams("arbitrary", "arbitrary"),
    )(x2d, w)


def _fox_cum_kernel(x_ref, w_ref, b_ref, o_ref, carry_ref, *, heads):
    @pl.when(pl.program_id(1) == 0)
    def _():
        carry_ref[...] = jnp.zeros_like(carry_ref)

    ts = x_ref.shape[0]
    lane = lax.broadcasted_iota(I32, w_ref.shape, 1)
    w = jnp.where(lane < heads, w_ref[...], 0.0).astype(BF16)
    z = jnp.dot(x_ref[...].astype(BF16), w, preferred_element_type=F32) + b_ref[...]
    logf = -(jnp.maximum(-z, 0.0) + jnp.log1p(jnp.exp(-jnp.abs(z))))
    row = lax.broadcasted_iota(I32, (ts, ts), 0)
    col = lax.broadcasted_iota(I32, (ts, ts), 1)
    tril = (row >= col).astype(F32)
    cum = jnp.dot(tril, logf, preferred_element_type=F32, precision=lax.Precision.HIGHEST) + carry_ref[...]
    o_ref[...] = cum * LOG2E
    carry_ref[...] = cum[ts - 1:ts, :]


def _fox_cum(x2d, w_in, layer, col0, b_f, batch, seq, ts=512):
    t, d = x2d.shape
    h = w_in.shape[2] - col0
    assert h <= LANES and col0 % LANES == 0
    ts = _tile(seq, ts)
    b_pad = jnp.zeros((1, LANES), F32).at[0, :h].set(b_f.astype(F32))
    ns = seq // ts
    return pl.pallas_call(
        functools.partial(_fox_cum_kernel, heads=h),
        grid=(batch, ns),
        in_specs=[pl.BlockSpec((ts, d), lambda b, s: (b * ns + s, 0)),
                  pl.BlockSpec((None, d, LANES), lambda b, s: (layer, 0, col0 // LANES)),
                  pl.BlockSpec((1, LANES), lambda b, s: (0, 0))],
        out_specs=pl.BlockSpec((ts, LANES), lambda b, s: (b * ns + s, 0)),
        out_shape=jax.ShapeDtypeStruct((t, LANES), F32),
        scratch_shapes=[pltpu.VMEM((1, LANES), F32)],
        compiler_params=_cparams("arbitrary", "arbitrary"),
    )(x2d, w_in, b_pad)


def _fox_attn_kernel(q_ref, k_ref, v_ref, c_ref, o_ref, *, tb):
    qi = pl.program_id(2)
    q = q_ref[...]
    hd = q.shape[1]

    def scores(j):
        k = k_ref[pl.ds(pl.multiple_of(j * tb, tb), tb), :]
        s = lax.dot_general(q, k, (((1,), (1,)), ((), ())), preferred_element_type=F32)
        return s - c_ref[j]

    def update(j, t, carry):
        m, l, acc = carry
        v = v_ref[pl.ds(pl.multiple_of(j * tb, tb), tb), :]
        m_new = jnp.maximum(m, jnp.max(t, axis=-1, keepdims=True))
        p = jnp.exp2(t - m_new)
        alpha = jnp.exp2(m - m_new)
        l = alpha * l + jnp.sum(p, axis=-1, keepdims=True)
        acc = alpha * acc + jnp.dot(p.astype(BF16), v, preferred_element_type=F32)
        return m_new, l, acc

    init = (jnp.full((tb, 1), -jnp.inf, F32), jnp.zeros((tb, 1), F32), jnp.zeros((tb, hd), F32))
    carry = lax.fori_loop(0, qi, lambda j, c: update(j, scores(j), c), init)
    row = lax.broadcasted_iota(I32, (tb, tb), 0)
    col = lax.broadcasted_iota(I32, (tb, tb), 1)
    t = jnp.where(row >= col, scores(qi), -jnp.inf)
    _, l, acc = update(qi, t, carry)
    o_ref[...] = (acc / l).astype(o_ref.dtype)


def _fox_attn(qkv, cum_t, batch, seq, heads, hd, tb=512):
    t = qkv.shape[0]
    d = heads * hd
    tb = _tile(seq, tb)
    nq = seq // tb
    assert hd % LANES == 0
    return pl.pallas_call(
        functools.partial(_fox_attn_kernel, tb=tb),
        grid=(batch, heads, nq),
        in_specs=[pl.BlockSpec((tb, hd), lambda b, h, i: (b * nq + i, h)),
                  pl.BlockSpec((seq, hd), lambda b, h, i: (b, heads + h)),
                  pl.BlockSpec((seq, hd), lambda b, h, i: (b, 2 * heads + h)),
                  pl.BlockSpec((None, None, nq, 1, tb), lambda b, h, i: (b, h, 0, 0, 0))],
        out_specs=pl.BlockSpec((tb, hd), lambda b, h, i: (b * nq + i, h)),
        out_shape=jax.ShapeDtypeStruct((t, d), BF16),
        compiler_params=_cparams("arbitrary", "arbitrary", "arbitrary"),
    )(qkv, qkv, qkv, cum_t)


def _oproj_ln_kernel(a_ref, w_ref, x_ref, g_ref, b_ref, o_ref, ob_ref, *, alpha):
    h = jnp.dot(a_ref[...], w_ref[...], preferred_element_type=F32)
    y = _layer_norm(alpha * x_ref[...] + h, g_ref[...], b_ref[...])
    o_ref[...] = y
    ob_ref[...] = y.astype(BF16)


def _oproj_ln(a_bf, w_bf, x2d, g, b, alpha, tm=512):
    t, d = x2d.shape
    kk = a_bf.shape[1]
    tm = _tile(t, tm)
    return pl.pallas_call(
        functools.partial(_oproj_ln_kernel, alpha=alpha),
        grid=(t // tm,),
        in_specs=[pl.BlockSpec((tm, kk), lambda i: (i, 0)),
                  pl.BlockSpec((kk, d), lambda i: (0, 0)),
                  pl.BlockSpec((tm, d), lambda i: (i, 0)),
                  pl.BlockSpec((1, d), lambda i: (0, 0)),
                  pl.BlockSpec((1, d), lambda i: (0, 0))],
        out_specs=[pl.BlockSpec((tm, d), lambda i: (i, 0)),
                   pl.BlockSpec((tm, d), lambda i: (i, 0))],
        out_shape=[jax.ShapeDtypeStruct((t, d), F32), jax.ShapeDtypeStruct((t, d), BF16)],
        compiler_params=_cparams("arbitrary"),
    )(a_bf, w_bf, x2d, g.reshape(1, d).astype(F32), b.reshape(1, d).astype(F32))


def _first_max(vals, idx, sentinel):
    m = jnp.max(vals, axis=0, keepdims=True)
    first = jnp.min(jnp.where(vals == m, idx, sentinel), axis=0, keepdims=True)
    return m, first


def _router_kernel(x_ref, rw_ref, bias_ref, eidx_ref, wts_ref):
    e = bias_ref.shape[0]
    tm = x_ref.shape[0]
    per = e // N_GROUPS
    logits = jnp.dot(x_ref[...], rw_ref[...], preferred_element_type=F32, precision=lax.Precision.HIGHEST)
    scores = jax.nn.sigmoid(logits.T[:e, :])
    choice = scores + bias_ref[...]
    neg = jnp.float32(-jnp.inf)

    ip = lax.broadcasted_iota(I32, (per, tm), 0)
    gs = []
    for g in range(N_GROUPS):
        c = choice[g * per:(g + 1) * per, :]
        m1, f1 = _first_max(c, ip, per)
        m2 = jnp.max(jnp.where(ip == f1, neg, c), axis=0, keepdims=True)
        gs.append(m1 + m2)
    gscore = jnp.concatenate(gs, axis=0)

    ig = lax.broadcasted_iota(I32, (N_GROUPS, tm), 0)
    sel = jnp.zeros((N_GROUPS, tm), F32)
    for _ in range(TOPK_GROUPS):
        _, fg = _first_max(gscore, ig, N_GROUPS)
        hit = ig == fg
        sel = jnp.where(hit, 1.0, sel)
        gscore = jnp.where(hit, neg, gscore)

    masked = jnp.concatenate(
        [jnp.where(sel[g:g + 1, :] > 0.0, choice[g * per:(g + 1) * per, :], neg) for g in range(N_GROUPS)],
        axis=0)
    ie = lax.broadcasted_iota(I32, (e, tm), 0)
    idxs, ws = [], []
    for _ in range(TOP_K):
        _, fe = _first_max(masked, ie, e)
        hit = ie == fe
        idxs.append(fe)
        ws.append(jnp.sum(jnp.where(hit, scores, 0.0), axis=0, keepdims=True))
        masked = jnp.where(hit, neg, masked)
    w = jnp.concatenate(ws, axis=0)
    eidx_ref[...] = jnp.concatenate(idxs, axis=0)
    wts_ref[...] = w / jnp.sum(w, axis=0, keepdims=True) * ROUTED_SCALE


def _router(x2d, router_w, router_bias, tm=512):
    t, d = x2d.shape
    e = router_w.shape[1]
    assert e % N_GROUPS == 0 and (e // N_GROUPS) % SUBLANES == 0
    tm = _tile(t, tm)
    ep = -(-e // LANES) * LANES
    rw_pad = jnp.zeros((d, ep), F32).at[:, :e].set(router_w.astype(F32))
    return pl.pallas_call(
        _router_kernel,
        grid=(t // tm,),
        in_specs=[pl.BlockSpec((tm, d), lambda i: (i, 0)),
                  pl.BlockSpec((d, ep), lambda i: (0, 0)),
                  pl.BlockSpec((e, 1), lambda i: (0, 0))],
        out_specs=[pl.BlockSpec((TOP_K, tm), lambda i: (0, i)),
                   pl.BlockSpec((TOP_K, tm), lambda i: (0, i))],
        out_shape=[jax.ShapeDtypeStruct((TOP_K, t), I32), jax.ShapeDtypeStruct((TOP_K, t), F32)],
        compiler_params=_cparams("arbitrary"),
    )(x2d, rw_pad, router_bias.reshape(e, 1).astype(F32))


def _seg_copy(src, dst, src_off, dst_off, length, max_len, sem, wait=False):
    done = 0
    b = max_len
    while b >= SUBLANES:
        piece = length & b

        @pl.when(piece != 0)
        def _(b=b, done=done):
            cp = pltpu.make_async_copy(src.at[pl.ds(pl.multiple_of(src_off + done, SUBLANES), b)],
                                       dst.at[pl.ds(pl.multiple_of(dst_off + done, SUBLANES), b)], sem)
            if wait:
                cp.wait()
            else:
                cp.start()
        done = done + piece
        b //= 2


def _pow2_floor(n):
    return 1 << (n.bit_length() - 1)


def _dispatch_kernel(tab_ref, pad_ref, lpt_ref, x_ref, o_hbm, xs, zbuf, sem, *, n_exp, rc):
    i = pl.program_id(0)
    n = pl.num_programs(0)
    slot = i % 2
    tt = x_ref.shape[0]
    r_tot, half = xs.shape[1], xs.shape[2]

    def wait_slot(s, step):
        _seg_copy(xs.at[s], o_hbm, 0, 0, pad_ref[2 * n_exp + step], _pow2_floor(r_tot), sem.at[s], wait=True)

    @pl.when(i >= 2)
    def _():
        wait_slot(slot, i - 2)

    xlo, xhi = x_ref[:, :half], x_ref[:, half:]
    lpt = lpt_ref[...]
    for c in range(r_tot // rc):
        rows = lax.broadcasted_iota(I32, (rc, tt), 0) + c * rc
        hit = rows == lpt[0:1, :]
        for k in range(1, TOP_K):
            hit = hit | (rows == lpt[k:k + 1, :])
        g = jnp.where(hit, 1.0, 0.0).astype(BF16)
        lo = jnp.dot(g, xlo, preferred_element_type=F32)
        hi = jnp.dot(g, xhi, preferred_element_type=F32)
        xs[slot, c * rc:(c + 1) * rc, :] = _pack_halves(lo, hi)

    for e in range(n_exp):
        _seg_copy(xs.at[slot], o_hbm, tab_ref[2 * n_exp + e], tab_ref[e], tab_ref[n_exp + e], tt, sem.at[slot])

    @pl.when(i == 0)
    def _():
        zbuf[...] = jnp.zeros_like(zbuf)
        for w in (False, True):
            for e in range(n_exp):
                _seg_copy(zbuf, o_hbm, 0, pad_ref[e], pad_ref[n_exp + e], zbuf.shape[0] // 2, sem.at[2], wait=w)

    @pl.when(i == n - 1)
    def _():
        @pl.when(n >= 2)
        def _():
            wait_slot(1 - slot, i - 1)
        wait_slot(slot, i)


def _tile_rows(tt, n_exp, rc=512):
    r = tt * TOP_K + (SUBLANES - 1) * n_exp
    rc = min(rc, tt * TOP_K)
    return -(-r // rc) * rc, rc


def _dispatch(xbf, tab, pad_tab, lpos_t, m_pad, n_exp, tt, bm):
    t, d = xbf.shape
    assert tt & (tt - 1) == 0 and bm & (bm - 1) == 0 and d % 2 == 0 and tt % SUBLANES == 0
    r_tot, rc = _tile_rows(tt, n_exp)
    return pl.pallas_call(
        functools.partial(_dispatch_kernel, n_exp=n_exp, rc=rc),
        grid=(t // tt,),
        in_specs=[pl.BlockSpec((4 * n_exp,), lambda i: (i,), memory_space=pltpu.SMEM),
                  pl.BlockSpec(memory_space=pltpu.SMEM),
                  pl.BlockSpec((TOP_K, tt), lambda i: (0, i)),
                  pl.BlockSpec((tt, d), lambda i: (i, 0))],
        out_specs=pl.BlockSpec(memory_space=pl.ANY),
        out_shape=jax.ShapeDtypeStruct((m_pad, d // 2), U32),
        scratch_shapes=[pltpu.VMEM((2, r_tot, d // 2), U32), pltpu.VMEM((bm, d // 2), U32),
                        pltpu.SemaphoreType.DMA((3,))],
        compiler_params=_cparams("arbitrary"),
    )(tab, pad_tab, lpos_t, xbf)


def _expert_kernel(be_ref, nv_ref, xs_ref, wgu_ref, wd_ref, y_ref, wgu_bf, wd_bf, *, cast_rows):
    i = pl.program_id(0)

    @pl.when(i < nv_ref[0])
    def _():
        prev = be_ref[jnp.maximum(i - 1, 0)]

        @pl.when((i == 0) | (be_ref[i] != prev))
        def _():
            def cast_gu(c, carry):
                r = pl.ds(pl.multiple_of(c * cast_rows, cast_rows), cast_rows)
                wgu_bf[r, :] = wgu_ref[r, :].astype(BF16)
                return carry
            lax.fori_loop(0, wgu_ref.shape[0] // cast_rows, cast_gu, 0)
            wd_bf[...] = wd_ref[...].astype(BF16)

        half = xs_ref.shape[1]
        eh = wd_ref.shape[0]
        lo, hi = _unpack_rows(xs_ref[...])
        gu = (jnp.dot(lo, wgu_bf[:half, :], preferred_element_type=F32)
              + jnp.dot(hi, wgu_bf[half:, :], preferred_element_type=F32))
        a = (_silu(gu[:, :eh]) * gu[:, eh:]).astype(BF16)
        y_ref[...] = _pack_rows(jnp.dot(a, wd_bf[...], preferred_element_type=F32))


def _experts(xs, block_e, nvalid, w_gu, w_down, layer, bm):
    m_pad, dh = xs.shape
    _, e, d, eh2 = w_gu.shape
    eh = w_down.shape[2]
    nb = m_pad // bm
    cast_rows = _tile(d, 256)

    def blk(i, be, nv):
        return jnp.minimum(i, nv[0] - 1)

    grid_spec = pltpu.PrefetchScalarGridSpec(
        num_scalar_prefetch=2,
        grid=(nb,),
        in_specs=[pl.BlockSpec((bm, dh), lambda i, be, nv: (blk(i, be, nv), 0)),
                  pl.BlockSpec((None, None, d, eh2), lambda i, be, nv: (layer, be[blk(i, be, nv)], 0, 0)),
                  pl.BlockSpec((None, None, eh, d), lambda i, be, nv: (layer, be[blk(i, be, nv)], 0, 0))],
        out_specs=pl.BlockSpec((bm, dh), lambda i, be, nv: (blk(i, be, nv), 0)),
        scratch_shapes=[pltpu.VMEM((d, eh2), BF16), pltpu.VMEM((eh, d), BF16)],
    )
    return pl.pallas_call(
        functools.partial(_expert_kernel, cast_rows=cast_rows),
        grid_spec=grid_spec,
        out_shape=jax.ShapeDtypeStruct((m_pad, dh), U32),
        compiler_params=_cparams("arbitrary"),
    )(block_e, nvalid, xs, w_gu, w_down)


def _combine_kernel(tabc_ref, tabn_ref, lp_ref, w_ref, x_ref, sgu_ref, sd_ref, g_ref, b_ref, y_hbm,
                    o_ref, ybuf, sem, *, alpha, n_exp):
    i = pl.program_id(0)
    n = pl.num_programs(0)
    tt = x_ref.shape[0]
    r_tot = ybuf.shape[1]
    slot = i % 2

    def issue(tab_ref, s):
        for e in range(n_exp):
            _seg_copy(y_hbm, ybuf.at[s], tab_ref[e], tab_ref[2 * n_exp + e], tab_ref[n_exp + e], tt, sem.at[s])

    @pl.when(i == 0)
    def _():
        ybuf[...] = jnp.zeros_like(ybuf)
        issue(tabc_ref, 0)

    @pl.when(i + 1 < n)
    def _():
        issue(tabn_ref, 1 - slot)

    _seg_copy(y_hbm, ybuf.at[slot], 0, 0, tabc_ref[3 * n_exp], _pow2_floor(r_tot), sem.at[slot], wait=True)

    lp = lp_ref[...]
    w = w_ref[...]
    cols = lax.broadcasted_iota(I32, (tt, r_tot), 1)
    p = jnp.zeros((tt, r_tot), F32)
    for k in range(TOP_K):
        p = jnp.where(cols == lp[:, k:k + 1], w[:, k:k + 1], p)
    pb = p.astype(BF16)
    ylo, yhi = _unpack_rows(ybuf[slot])
    routed = jnp.concatenate([jnp.dot(pb, ylo, preferred_element_type=F32),
                              jnp.dot(pb, yhi, preferred_element_type=F32)], axis=1)

    x = x_ref[...]
    sh = sd_ref.shape[0]
    sg = jnp.dot(x.astype(BF16), sgu_ref[...], preferred_element_type=F32)
    a = (_silu(sg[:, :sh]) * sg[:, sh:]).astype(BF16)
    shared = jnp.dot(a, sd_ref[...], preferred_element_type=F32)
    o_ref[...] = _layer_norm(alpha * x + (routed + shared), g_ref[...], b_ref[...])


def _combine(x2d, wts, lpos, tab, y_sorted, s_gu_bf, s_down_bf, g, b, alpha, n_exp, tt):
    t, d = x2d.shape
    n = t // tt
    sh = s_down_bf.shape[0]
    return pl.pallas_call(
        functools.partial(_combine_kernel, alpha=alpha, n_exp=n_exp),
        grid=(n,),
        in_specs=[pl.BlockSpec((4 * n_exp,), lambda i: (i,), memory_space=pltpu.SMEM),
                  pl.BlockSpec((4 * n_exp,), lambda i: (jnp.minimum(i + 1, n - 1),), memory_space=pltpu.SMEM),
                  pl.BlockSpec((tt, TOP_K), lambda i: (i, 0)),
                  pl.BlockSpec((tt, TOP_K), lambda i: (i, 0)),
                  pl.BlockSpec((tt, d), lambda i: (i, 0)),
                  pl.BlockSpec((d, 2 * sh), lambda i: (0, 0)),
                  pl.BlockSpec((sh, d), lambda i: (0, 0)),
                  pl.BlockSpec((1, d), lambda i: (0, 0)),
                  pl.BlockSpec((1, d), lambda i: (0, 0)),
                  pl.BlockSpec(memory_space=pl.ANY)],
        out_specs=pl.BlockSpec((tt, d), lambda i: (i, 0)),
        out_shape=jax.ShapeDtypeStruct((t, d), F32),
        scratch_shapes=[pltpu.VMEM((2, _tile_rows(tt, n_exp)[0], d // 2), U32), pltpu.SemaphoreType.DMA((2,))],
        compiler_params=_cparams("arbitrary"),
    )(tab, tab, lpos, wts, x2d, s_gu_bf, s_down_bf,
      g.reshape(1, d).astype(F32), b.reshape(1, d).astype(F32), y_sorted)


def _moe_layer(x2d, xbf, router_w, router_bias, w_gu, w_down, layer, s_gu, s_down, g, b, alpha, bm=512, tt=256):
    t, d = x2d.shape
    e = router_w.shape[1]
    tt = _tile(t, tt)
    nt = t // tt
    eidx_t, wts_t = _router(x2d, router_w, router_bias)
    eidx, wts = eidx_t.T, wts_t.T

    onehot = jnp.sum((eidx[:, :, None] == jnp.arange(e, dtype=I32)).astype(I32), axis=1)
    oh = onehot.reshape(nt, tt, e)
    rank_in_tile = jnp.cumsum(oh, axis=1) - oh
    cnt = jnp.sum(oh, axis=1)
    run = ((cnt + SUBLANES - 1) // SUBLANES) * SUBLANES
    before = jnp.cumsum(run, axis=0) - run
    off = jnp.cumsum(run, axis=1) - run
    counts = jnp.sum(run, axis=0)
    padded = ((counts + bm - 1) // bm) * bm
    ends = jnp.cumsum(padded)
    starts = ends - padded
    lpos = jnp.take_along_axis((off[:, None, :] + rank_in_tile).reshape(t, e), eidx, axis=1).astype(I32)
    totals = jnp.sum(run, axis=1)
    spare = jnp.zeros_like(cnt).at[:, 0].set(totals)
    tab = jnp.concatenate([starts[None, :] + before, run, off, spare], axis=1).reshape(-1).astype(I32)
    pad_tab = jnp.concatenate([starts + counts, padded - counts, totals]).astype(I32)
    m_pad = -(-(t * TOP_K + (SUBLANES - 1) * nt * e) // bm) * bm + e * bm
    nb = m_pad // bm
    block_start = jnp.arange(nb, dtype=I32) * bm
    block_e = jnp.minimum(jnp.sum((ends[None, :] <= block_start[:, None]).astype(I32), axis=1), e - 1).astype(I32)
    nvalid = (ends[-1:] // bm).astype(I32)

    xs = _dispatch(xbf, tab, pad_tab, lpos.T, m_pad, e, tt, bm)
    y_sorted = _experts(xs, block_e, nvalid, w_gu, w_down, layer, bm)
    return _combine(x2d, wts, lpos, tab, y_sorted, s_gu.astype(BF16), s_down.astype(BF16), g, b, alpha, e, tt)


def _proj_rope_kernel(x_ref, w_ref, cos_ref, sin_ref, o_ref):
    y = jnp.dot(x_ref[...].astype(BF16), w_ref[...], preferred_element_type=F32)
    tm, n = y.shape
    tw = cos_ref.shape[1]
    lane = lax.broadcasted_iota(I32, (tm, LANES), 1)
    first_half = (lane % HD_B) < (HD_B // 2)
    for c in range(n // LANES):
        yc = y[:, c * LANES:(c + 1) * LANES]
        off = (c * LANES) % tw
        rot = jnp.where(first_half, pltpu.roll(yc, LANES - HD_B // 2, axis=1), pltpu.roll(yc, HD_B // 2, axis=1))
        yc = yc * cos_ref[:, off:off + LANES] + rot * sin_ref[:, off:off + LANES]
        o_ref[:, c * LANES:(c + 1) * LANES] = yc.astype(o_ref.dtype)


def _proj_rope(x2d, w_bf, cos, sin, seq, tm=512, tn=1024):
    t, d = x2d.shape
    n = w_bf.shape[1]
    tw = cos.shape[1]
    tm, tn = _tile(seq, tm), _tile(n, tn)
    assert tn % tw == 0 and tw % LANES == 0
    ns = seq // tm
    return pl.pallas_call(
        _proj_rope_kernel,
        grid=(n // tn, t // tm),
        in_specs=[pl.BlockSpec((tm, d), lambda j, i: (i, 0)),
                  pl.BlockSpec((d, tn), lambda j, i: (0, j)),
                  pl.BlockSpec((tm, tw), lambda j, i: (i % ns, 0)),
                  pl.BlockSpec((tm, tw), lambda j, i: (i % ns, 0))],
        out_specs=pl.BlockSpec((tm, tn), lambda j, i: (i, j)),
        out_shape=jax.ShapeDtypeStruct((t, n), BF16),
        compiler_params=_cparams("arbitrary", "arbitrary"),
    )(x2d, w_bf, cos, sin)


def _rope_tables(seq, scale, heads, plain_cols=0):
    half = HD_B // 2
    inv = 1.0 / (ROPE_THETA ** (jnp.arange(half, dtype=F32) / half))
    ang = jnp.arange(seq, dtype=F32)[:, None] * inv[None, :]
    cos, sin = jnp.cos(ang), jnp.sin(ang)
    cos_t = jnp.tile(jnp.concatenate([cos, cos], axis=1), (1, heads)) * scale
    sin_t = jnp.tile(jnp.concatenate([-sin, sin], axis=1), (1, heads)) * scale
    cos_t = jnp.concatenate([cos_t, jnp.ones((seq, plain_cols), F32)], axis=1)
    sin_t = jnp.concatenate([sin_t, jnp.zeros((seq, plain_cols), F32)], axis=1)
    return cos_t, sin_t


def _swa_kernel(q_ref, kc_ref, kp_ref, vc_ref, vp_ref, sink_ref, o_ref, *, groups):
    qi = pl.program_id(2)
    tq = q_ref.shape[0]
    w = WINDOW
    row = lax.broadcasted_iota(I32, (w, 2 * w), 0)
    col = lax.broadcasted_iota(I32, (w, 2 * w), 1)
    band = (col > row) & (col <= row + w)
    neg = jnp.float32(-jnp.inf)
    sink = jnp.stack([sink_ref[g:g + 1, 0:1] for g in range(groups)])
    for j in range(tq // w):
        if j == 0:
            keys = [jnp.concatenate([kp_ref[s], kc_ref[s, :w, :]], axis=0) for s in range(2)]
            vals = [jnp.concatenate([vp_ref[s], vc_ref[s, :w, :]], axis=0) for s in range(2)]
            mask = band & ((col >= w) | (qi > 0))
        else:
            keys = [kc_ref[s, (j - 1) * w:(j + 1) * w, :] for s in range(2)]
            vals = [vc_ref[s, (j - 1) * w:(j + 1) * w, :] for s in range(2)]
            mask = band
        scs = []
        for p in range(groups // 2):
            qs = q_ref[j * w:(j + 1) * w, p * LANES:(p + 1) * LANES]
            for s in range(2):
                scs.append(lax.dot_general(qs, keys[s], (((1,), (1,)), ((), ())), preferred_element_type=F32))
        sc = jnp.where(mask[None], jnp.stack(scs), neg)
        m = jnp.maximum(jnp.max(jnp.maximum(sc[..., :w], sc[..., w:]), axis=-1, keepdims=True), sink)
        pr = jnp.exp(sc - m)
        den = jnp.sum(pr[..., :w] + pr[..., w:], axis=-1, keepdims=True) + jnp.exp(sink - m)
        pr = (pr * (1.0 / den)).astype(BF16)
        for p in range(groups // 2):
            out = (jnp.dot(pr[2 * p], vals[0], preferred_element_type=F32)
                   + jnp.dot(pr[2 * p + 1], vals[1], preferred_element_type=F32))
            o_ref[j * w:(j + 1) * w, p * LANES:(p + 1) * LANES] = out.astype(o_ref.dtype)


def _swa_attn(q, kk, vv, sink_tab, batch, seq, kvh, groups, tq=512):
    t = q.shape[0]
    tq = _tile(seq, tq)
    assert tq % WINDOW == 0 and groups % 2 == 0 and 2 * HD_B == LANES
    nq = seq // tq
    per = tq // WINDOW
    gw = groups * HD_B
    cur = pl.BlockSpec((None, None, 2, tq, LANES), lambda b, h, i: (b, h, 0, i, 0))
    prev = pl.BlockSpec((None, None, 2, WINDOW, LANES), lambda b, h, i: (b, h, 0, jnp.maximum(i * per - 1, 0), 0))
    return pl.pallas_call(
        functools.partial(_swa_kernel, groups=groups),
        grid=(batch, kvh, nq),
        in_specs=[pl.BlockSpec((tq, gw), lambda b, h, i: (b * nq + i, h)),
                  cur, prev, cur, prev,
                  pl.BlockSpec((None, groups, LANES), lambda b, h, i: (h, 0, 0))],
        out_specs=pl.BlockSpec((tq, gw), lambda b, h, i: (b * nq + i, h)),
        out_shape=jax.ShapeDtypeStruct((t, kvh * gw), BF16),
        compiler_params=_cparams("arbitrary", "arbitrary", "arbitrary"),
    )(q, kk, kk, vv, vv, sink_tab)


def _lane_pairs(a, batch, seq, kvh):
    a4 = a.reshape(batch, seq, kvh, HD_B).transpose(0, 2, 1, 3)
    z = jnp.zeros_like(a4)
    return jnp.stack([jnp.concatenate([a4, z], axis=-1), jnp.concatenate([z, a4], axis=-1)], axis=2)


def kernel(x, a_w_in, a_b_f, a_w_o, kv_w, b_w_q, b_sinks, b_w_o, router_w, router_bias, moe_w_gu, moe_w_down,
           shared_w_gu, shared_w_down, ln1_g, ln1_b, ln2_g, ln2_b):
    batch, seq, d = x.shape
    n_a, n_b = a_w_in.shape[0], b_w_q.shape[0]
    depth = n_a + n_b
    alpha = float((2 * depth) ** 0.25)
    heads_a = a_b_f.shape[1]
    hd_a = d // heads_a
    heads_b = b_sinks.shape[1]
    kvh = kv_w.shape[1] // (2 * HD_B)
    groups = heads_b // kvh
    t = batch * seq

    x2d = x.reshape(t, d).astype(F32)
    kk = vv = None
    cos_q, sin_q = _rope_tables(seq, 1.0 / math.sqrt(HD_B), LANES // HD_B)
    cos_k, sin_k = _rope_tables(seq, 1.0, kvh, plain_cols=kvh * HD_B)
    for i in range(depth):
        if i < n_a:
            qkv = _proj(x2d, a_w_in, i, 3 * d, scaled_cols=d, scale=LOG2E / math.sqrt(hd_a))
            cum = _fox_cum(x2d, a_w_in, i, 3 * d, a_b_f[i], batch, seq)
            tb = _tile(seq, 512)
            cum_t = cum[:, :heads_a].reshape(batch, seq, heads_a).transpose(0, 2, 1)
            cum_t = cum_t.reshape(batch, heads_a, seq // tb, 1, tb)
            attn = _fox_attn(qkv, cum_t, batch, seq, heads_a, hd_a, tb=tb)
            w_o = a_w_o[i]
        else:
            j = i - n_a
            q = _proj_rope(x2d, b_w_q[j].astype(BF16), cos_q, sin_q, seq)
            sink_tab = jnp.broadcast_to(b_sinks[j].astype(F32).reshape(kvh, groups, 1), (kvh, groups, LANES))
            attn = _swa_attn(q, kk, vv, sink_tab, batch, seq, kvh, groups)
            w_o = b_w_o[j]
        x2d, xbf = _oproj_ln(attn, w_o.astype(BF16), x2d, ln1_g[i], ln1_b[i], alpha)
        x2d = _moe_layer(x2d, xbf, router_w[i], router_bias[i], moe_w_gu, moe_w_down, i,
                         shared_w_gu[i], shared_w_down[i], ln2_g[i], ln2_b[i], alpha)
        if i == n_a - 1:
            kv = _proj_rope(x2d, kv_w.astype(BF16), cos_k, sin_k, seq, tn=kv_w.shape[1])
            kk = _lane_pairs(kv[:, :kvh * HD_B], batch, seq, kvh)
            vv = _lane_pairs(kv[:, kvh * HD_B:], batch, seq, kvh)
    return x2d.reshape(batch, seq, d).astype(x.dtype)
```

```python
import functools
import math

import jax
import jax.numpy as jnp
from jax import lax
from jax.experimental import pallas as pl
from jax.experimental.pallas import tpu as pltpu

F32 = jnp.float32
BF16 = jnp.bfloat16
U32 = jnp.uint32
I32 = jnp.int32

LANES = 128
SUBLANES = 8
HD_B = 64
WINDOW = 128
ROPE_THETA = 10000.0
TOP_K = 8
N_GROUPS = 8
TOPK_GROUPS = 4
ROUTED_SCALE = 2.5
LN_EPS = 1e-5
VMEM_LIMIT_BYTES = 56 * 1024 * 1024
HI_MASK = 0xFFFF0000
LOG2E = math.log2(math.e)


def _cparams(*sem):
    return pltpu.CompilerParams(dimension_semantics=sem, vmem_limit_bytes=VMEM_LIMIT_BYTES)


def _tile(dim, pref):
    t = min(dim, pref)
    assert dim % t == 0, (dim, pref)
    return t


def _silu(x):
    return x * jax.nn.sigmoid(x)


def _layer_norm(z, g, b):
    mu = jnp.mean(z, axis=-1, keepdims=True)
    zc = z - mu
    var = jnp.mean(zc * zc, axis=-1, keepdims=True)
    return zc * lax.rsqrt(var + LN_EPS) * g + b


def _pack_halves(lo, hi):
    lo = lax.bitcast_convert_type(lo, U32)
    hi = lax.bitcast_convert_type(hi, U32)
    return (hi & jnp.uint32(HI_MASK)) | (lo >> jnp.uint32(16))


def _pack_rows(y):
    half = y.shape[1] // 2
    yb = y.astype(BF16).astype(F32)
    return _pack_halves(yb[:, :half], yb[:, half:])


def _unpack_rows(pk):
    lo = lax.bitcast_convert_type(pk << jnp.uint32(16), F32).astype(BF16)
    hi = lax.bitcast_convert_type(pk & jnp.uint32(HI_MASK), F32).astype(BF16)
    return lo, hi


def _proj_kernel(x_ref, w_ref, o_ref, w_bf, *, scaled_tiles, scale, cast_rows):
    j = pl.program_id(0)

    @pl.when(pl.program_id(1) == 0)
    def _():
        def cast(c, carry):
            r = pl.ds(pl.multiple_of(c * cast_rows, cast_rows), cast_rows)
            w_bf[r, :] = w_ref[r, :].astype(BF16)
            return carry
        lax.fori_loop(0, w_ref.shape[0] // cast_rows, cast, 0)

    y = jnp.dot(x_ref[...].astype(BF16), w_bf[...], preferred_element_type=F32)
    s = jnp.where(j < scaled_tiles, jnp.float32(scale), jnp.float32(1.0))
    o_ref[...] = (y * s).astype(o_ref.dtype)


def _proj(x2d, w, layer, n, *, scaled_cols, scale, tm=512, tn=1024):
    t, d = x2d.shape
    tm, tn = _tile(t, tm), _tile(scaled_cols, tn)
    assert n % tn == 0
    return pl.pallas_call(
        functools.partial(_proj_kernel, scaled_tiles=scaled_cols // tn, scale=scale, cast_rows=_tile(d, 256)),
        grid=(n // tn, t // tm),
        in_specs=[pl.BlockSpec((tm, d), lambda j, i: (i, 0)),
                  pl.BlockSpec((None, d, tn), lambda j, i: (layer, 0, j))],
        out_specs=pl.BlockSpec((tm, tn), lambda j, i: (i, j)),
        out_shape=jax.ShapeDtypeStruct((t, n), BF16),
        scratch_shapes=[pltpu.VMEM((d, tn), BF16)],
        compiler_params=_cparams("arbitrary", "arbitrary"),
    )(x2d, w)


def _fox_cum_kernel(x_ref, w_ref, b_ref, o_ref, carry_ref, *, heads):
    @pl.when(pl.program_id(1) == 0)
    def _():
        carry_ref[...] = jnp.zeros_like(carry_ref)

    ts = x_ref.shape[0]
    lane = lax.broadcasted_iota(I32, w_ref.shape, 1)
    w = jnp.where(lane < heads, w_ref[...], 0.0).astype(BF16)
    z = jnp.dot(x_ref[...].astype(BF16), w, preferred_element_type=F32) + b_ref[...]
    logf = -(jnp.maximum(-z, 0.0) + jnp.log1p(jnp.exp(-jnp.abs(z))))
    row = lax.broadcasted_iota(I32, (ts, ts), 0)
    col = lax.broadcasted_iota(I32, (ts, ts), 1)
    tril = (row >= col).astype(F32)
    cum = jnp.dot(tril, logf, preferred_element_type=F32, precision=lax.Precision.HIGHEST) + carry_ref[...]
    o_ref[...] = cum * LOG2E
    carry_ref[...] = cum[ts - 1:ts, :]


def _fox_cum(x2d, w_in, layer, col0, b_f, batch, seq, ts=512):
    t, d = x2d.shape
    h = w_in.shape[2] - col0
    assert h <= LANES and col0 % LANES == 0
    ts = _tile(seq, ts)
    b_pad = jnp.zeros((1, LANES), F32).at[0, :h].set(b_f.astype(F32))
    ns = seq // ts
    return pl.pallas_call(
        functools.partial(_fox_cum_kernel, heads=h),
        grid=(batch, ns),
        in_specs=[pl.BlockSpec((ts, d), lambda b, s: (b * ns + s, 0)),
                  pl.BlockSpec((None, d, LANES), lambda b, s: (layer, 0, col0 // LANES)),
                  pl.BlockSpec((1, LANES), lambda b, s: (0, 0))],
        out_specs=pl.BlockSpec((ts, LANES), lambda b, s: (b * ns + s, 0)),
        out_shape=jax.ShapeDtypeStruct((t, LANES), F32),
        scratch_shapes=[pltpu.VMEM((1, LANES), F32)],
        compiler_params=_cparams("arbitrary", "arbitrary"),
    )(x2d, w_in, b_pad)


def _fox_attn_kernel(q_ref, k_ref, v_ref, c_ref, o_ref, *, tb):
    qi = pl.program_id(2)
    q = q_ref[...]
    hd = q.shape[1]

    def scores(j):
        k = k_ref[pl.ds(pl.multiple_of(j * tb, tb), tb), :]
        s = lax.dot_general(q, k, (((1,), (1,)), ((), ())), preferred_element_type=F32)
        return s - c_ref[j]

    def update(j, t, carry):
        m, l, acc = carry
        v = v_ref[pl.ds(pl.multiple_of(j * tb, tb), tb), :]
        m_new = jnp.maximum(m, jnp.max(t, axis=-1, keepdims=True))
        p = jnp.exp2(t - m_new)
        alpha = jnp.exp2(m - m_new)
        l = alpha * l + jnp.sum(p, axis=-1, keepdims=True)
        acc = alpha * acc + jnp.dot(p.astype(BF16), v, preferred_element_type=F32)
        return m_new, l, acc

    init = (jnp.full((tb, 1), -jnp.inf, F32), jnp.zeros((tb, 1), F32), jnp.zeros((tb, hd), F32))
    carry = lax.fori_loop(0, qi, lambda j, c: update(j, scores(j), c), init)
    row = lax.broadcasted_iota(I32, (tb, tb), 0)
    col = lax.broadcasted_iota(I32, (tb, tb), 1)
    t = jnp.where(row >= col, scores(qi), -jnp.inf)
    _, l, acc = update(qi, t, carry)
    o_ref[...] = (acc / l).astype(o_ref.dtype)


def _fox_attn(qkv, cum_t, batch, seq, heads, hd, tb=512):
    t = qkv.shape[0]
    d = heads * hd
    tb = _tile(seq, tb)
    nq = seq // tb
    assert hd % LANES == 0
    return pl.pallas_call(
        functools.partial(_fox_attn_kernel, tb=tb),
        grid=(batch, heads, nq),
        in_specs=[pl.BlockSpec((tb, hd), lambda b, h, i: (b * nq + i, h)),
                  pl.BlockSpec((seq, hd), lambda b, h, i: (b, heads + h)),
                  pl.BlockSpec((seq, hd), lambda b, h, i: (b, 2 * heads + h)),
                  pl.BlockSpec((None, None, nq, 1, tb), lambda b, h, i: (b, h, 0, 0, 0))],
        out_specs=pl.BlockSpec((tb, hd), lambda b, h, i: (b * nq + i, h)),
        out_shape=jax.ShapeDtypeStruct((t, d), BF16),
        compiler_params=_cparams("arbitrary", "arbitrary", "arbitrary"),
    )(qkv, qkv, qkv, cum_t)


def _oproj_ln_kernel(a_ref, w_ref, x_ref, g_ref, b_ref, o_ref, ob_ref, *, alpha):
    h = jnp.dot(a_ref[...], w_ref[...], preferred_element_type=F32)
    y = _layer_norm(alpha * x_ref[...] + h, g_ref[...], b_ref[...])
    o_ref[...] = y
    ob_ref[...] = y.astype(BF16)


def _oproj_ln(a_bf, w_bf, x2d, g, b, alpha, tm=512):
    t, d = x2d.shape
    kk = a_bf.shape[1]
    tm = _tile(t, tm)
    return pl.pallas_call(
        functools.partial(_oproj_ln_kernel, alpha=alpha),
        grid=(t // tm,),
        in_specs=[pl.BlockSpec((tm, kk), lambda i: (i, 0)),
                  pl.BlockSpec((kk, d), lambda i: (0, 0)),
                  pl.BlockSpec((tm, d), lambda i: (i, 0)),
                  pl.BlockSpec((1, d), lambda i: (0, 0)),
                  pl.BlockSpec((1, d), lambda i: (0, 0))],
        out_specs=[pl.BlockSpec((tm, d), lambda i: (i, 0)),
                   pl.BlockSpec((tm, d), lambda i: (i, 0))],
        out_shape=[jax.ShapeDtypeStruct((t, d), F32), jax.ShapeDtypeStruct((t, d), BF16)],
        compiler_params=_cparams("arbitrary"),
    )(a_bf, w_bf, x2d, g.reshape(1, d).astype(F32), b.reshape(1, d).astype(F32))


def _first_max(vals, idx, sentinel):
    m = jnp.max(vals, axis=0, keepdims=True)
    first = jnp.min(jnp.where(vals == m, idx, sentinel), axis=0, keepdims=True)
    return m, first


def _router_kernel(x_ref, rw_ref, bias_ref, eidx_ref, wts_ref):
    e = bias_ref.shape[0]
    tm = x_ref.shape[0]
    per = e // N_GROUPS
    logits = jnp.dot(x_ref[...], rw_ref[...], preferred_element_type=F32, precision=lax.Precision.HIGHEST)
    scores = jax.nn.sigmoid(logits.T[:e, :])
    choice = scores + bias_ref[...]
    neg = jnp.float32(-jnp.inf)

    ip = lax.broadcasted_iota(I32, (per, tm), 0)
    gs = []
    for g in range(N_GROUPS):
        c = choice[g * per:(g + 1) * per, :]
        m1, f1 = _first_max(c, ip, per)
        m2 = jnp.max(jnp.where(ip == f1, neg, c), axis=0, keepdims=True)
        gs.append(m1 + m2)
    gscore = jnp.concatenate(gs, axis=0)

    ig = lax.broadcasted_iota(I32, (N_GROUPS, tm), 0)
    sel = jnp.zeros((N_GROUPS, tm), F32)
    for _ in range(TOPK_GROUPS):
        _, fg = _first_max(gscore, ig, N_GROUPS)
        hit = ig == fg
        sel = jnp.where(hit, 1.0, sel)
        gscore = jnp.where(hit, neg, gscore)

    masked = jnp.concatenate(
        [jnp.where(sel[g:g + 1, :] > 0.0, choice[g * per:(g + 1) * per, :], neg) for g in range(N_GROUPS)],
        axis=0)
    ie = lax.broadcasted_iota(I32, (e, tm), 0)
    idxs, ws = [], []
    for _ in range(TOP_K):
        _, fe = _first_max(masked, ie, e)
        hit = ie == fe
        idxs.append(fe)
        ws.append(jnp.sum(jnp.where(hit, scores, 0.0), axis=0, keepdims=True))
        masked = jnp.where(hit, neg, masked)
    w = jnp.concatenate(ws, axis=0)
    eidx_ref[...] = jnp.concatenate(idxs, axis=0)
    wts_ref[...] = w / jnp.sum(w, axis=0, keepdims=True) * ROUTED_SCALE


def _router(x2d, router_w, router_bias, tm=512):
    t, d = x2d.shape
    e = router_w.shape[1]
    assert e % N_GROUPS == 0 and (e // N_GROUPS) % SUBLANES == 0
    tm = _tile(t, tm)
    ep = -(-e // LANES) * LANES
    rw_pad = jnp.zeros((d, ep), F32).at[:, :e].set(router_w.astype(F32))
    return pl.pallas_call(
        _router_kernel,
        grid=(t // tm,),
        in_specs=[pl.BlockSpec((tm, d), lambda i: (i, 0)),
                  pl.BlockSpec((d, ep), lambda i: (0, 0)),
                  pl.BlockSpec((e, 1), lambda i: (0, 0))],
        out_specs=[pl.BlockSpec((TOP_K, tm), lambda i: (0, i)),
                   pl.BlockSpec((TOP_K, tm), lambda i: (0, i))],
        out_shape=[jax.ShapeDtypeStruct((TOP_K, t), I32), jax.ShapeDtypeStruct((TOP_K, t), F32)],
        compiler_params=_cparams("arbitrary"),
    )(x2d, rw_pad, router_bias.reshape(e, 1).astype(F32))


def _seg_copy(src, dst, src_off, dst_off, length, max_len, sem, wait=False):
    done = 0
    b = max_len
    while b >= SUBLANES:
        piece = length & b

        @pl.when(piece != 0)
        def _(b=b, done=done):
            cp = pltpu.make_async_copy(src.at[pl.ds(pl.multiple_of(src_off + done, SUBLANES), b)],
                                       dst.at[pl.ds(pl.multiple_of(dst_off + done, SUBLANES), b)], sem)
            if wait:
                cp.wait()
            else:
                cp.start()
        done = done + piece
        b //= 2


def _pow2_floor(n):
    return 1 << (n.bit_length() - 1)


def _dispatch_kernel(tab_ref, pad_ref, lpt_ref, x_ref, o_hbm, xs, zbuf, sem, *, n_exp, rc):
    i = pl.program_id(0)
    n = pl.num_programs(0)
    slot = i % 2
    tt = x_ref.shape[0]
    r_tot, half = xs.shape[1], xs.shape[2]

    def wait_slot(s, step):
        _seg_copy(xs.at[s], o_hbm, 0, 0, pad_ref[2 * n_exp + step], _pow2_floor(r_tot), sem.at[s], wait=True)

    @pl.when(i >= 2)
    def _():
        wait_slot(slot, i - 2)

    xlo, xhi = x_ref[:, :half], x_ref[:, half:]
    lpt = lpt_ref[...]
    for c in range(r_tot // rc):
        rows = lax.broadcasted_iota(I32, (rc, tt), 0) + c * rc
        hit = rows == lpt[0:1, :]
        for k in range(1, TOP_K):
            hit = hit | (rows == lpt[k:k + 1, :])
        g = jnp.where(hit, 1.0, 0.0).astype(BF16)
        lo = jnp.dot(g, xlo, preferred_element_type=F32)
        hi = jnp.dot(g, xhi, preferred_element_type=F32)
        xs[slot, c * rc:(c + 1) * rc, :] = _pack_halves(lo, hi)

    def copy_group(gi, carry):
        pltpu.make_async_copy(xs.at[slot, pl.ds(pl.multiple_of(gi * SUBLANES, SUBLANES), SUBLANES)],
                              o_hbm.at[pl.ds(pl.multiple_of(tab_ref[gi], SUBLANES), SUBLANES)], sem.at[slot]).start()
        return carry
    lax.fori_loop(0, pad_ref[2 * n_exp + i] // SUBLANES, copy_group, 0)

    @pl.when(i == 0)
    def _():
        zbuf[...] = jnp.zeros_like(zbuf)
        for w in (False, True):
            for e in range(n_exp):
                _seg_copy(zbuf, o_hbm, 0, pad_ref[e], pad_ref[n_exp + e], zbuf.shape[0] // 2, sem.at[2], wait=w)

    @pl.when(i == n - 1)
    def _():
        @pl.when(n >= 2)
        def _():
            wait_slot(1 - slot, i - 1)
        wait_slot(slot, i)


def _tile_rows(tt, n_exp, rc=512):
    r = tt * TOP_K + (SUBLANES - 1) * n_exp
    rc = min(rc, tt * TOP_K)
    return -(-r // rc) * rc, rc


def _dispatch(xbf, tab, pad_tab, lpos_t, m_pad, n_exp, tt, bm):
    t, d = xbf.shape
    assert tt & (tt - 1) == 0 and bm & (bm - 1) == 0 and d % 2 == 0 and tt % SUBLANES == 0
    r_tot, rc = _tile_rows(tt, n_exp)
    return pl.pallas_call(
        functools.partial(_dispatch_kernel, n_exp=n_exp, rc=rc),
        grid=(t // tt,),
        in_specs=[pl.BlockSpec((tab.size // (t // tt),), lambda i: (i,), memory_space=pltpu.SMEM),
                  pl.BlockSpec(memory_space=pltpu.SMEM),
                  pl.BlockSpec((TOP_K, tt), lambda i: (0, i)),
                  pl.BlockSpec((tt, d), lambda i: (i, 0))],
        out_specs=pl.BlockSpec(memory_space=pl.ANY),
        out_shape=jax.ShapeDtypeStruct((m_pad, d // 2), U32),
        scratch_shapes=[pltpu.VMEM((2, r_tot, d // 2), U32), pltpu.VMEM((bm, d // 2), U32),
                        pltpu.SemaphoreType.DMA((3,))],
        compiler_params=_cparams("arbitrary"),
    )(tab, pad_tab, lpos_t, xbf)


def _expert_kernel(be_ref, nv_ref, xs_ref, wgu_ref, wd_ref, y_ref, wgu_bf, wd_bf, *, cast_rows):
    i = pl.program_id(0)

    @pl.when(i < nv_ref[0])
    def _():
        prev = be_ref[jnp.maximum(i - 1, 0)]

        @pl.when((i == 0) | (be_ref[i] != prev))
        def _():
            def cast_gu(c, carry):
                r = pl.ds(pl.multiple_of(c * cast_rows, cast_rows), cast_rows)
                wgu_bf[r, :] = wgu_ref[r, :].astype(BF16)
                return carry
            lax.fori_loop(0, wgu_ref.shape[0] // cast_rows, cast_gu, 0)
            wd_bf[...] = wd_ref[...].astype(BF16)

        half = xs_ref.shape[1]
        eh = wd_ref.shape[0]
        lo, hi = _unpack_rows(xs_ref[...])
        gu = (jnp.dot(lo, wgu_bf[:half, :], preferred_element_type=F32)
              + jnp.dot(hi, wgu_bf[half:, :], preferred_element_type=F32))
        a = (_silu(gu[:, :eh]) * gu[:, eh:]).astype(BF16)
        y_ref[...] = _pack_rows(jnp.dot(a, wd_bf[...], preferred_element_type=F32))


def _experts(xs, block_e, nvalid, w_gu, w_down, layer, bm):
    m_pad, dh = xs.shape
    _, e, d, eh2 = w_gu.shape
    eh = w_down.shape[2]
    nb = m_pad // bm
    cast_rows = _tile(d, 256)

    def blk(i, be, nv):
        return jnp.minimum(i, nv[0] - 1)

    grid_spec = pltpu.PrefetchScalarGridSpec(
        num_scalar_prefetch=2,
        grid=(nb,),
        in_specs=[pl.BlockSpec((bm, dh), lambda i, be, nv: (blk(i, be, nv), 0)),
                  pl.BlockSpec((None, None, d, eh2), lambda i, be, nv: (layer, be[blk(i, be, nv)], 0, 0)),
                  pl.BlockSpec((None, None, eh, d), lambda i, be, nv: (layer, be[blk(i, be, nv)], 0, 0))],
        out_specs=pl.BlockSpec((bm, dh), lambda i, be, nv: (blk(i, be, nv), 0)),
        scratch_shapes=[pltpu.VMEM((d, eh2), BF16), pltpu.VMEM((eh, d), BF16)],
    )
    return pl.pallas_call(
        functools.partial(_expert_kernel, cast_rows=cast_rows),
        grid_spec=grid_spec,
        out_shape=jax.ShapeDtypeStruct((m_pad, dh), U32),
        compiler_params=_cparams("arbitrary"),
    )(block_e, nvalid, xs, w_gu, w_down)


def _combine_kernel(tabc_ref, tabn_ref, lp_ref, w_ref, x_ref, sgu_ref, sd_ref, g_ref, b_ref, y_hbm,
                    o_ref, ybuf, sem, *, alpha):
    i = pl.program_id(0)
    n = pl.num_programs(0)
    tt = x_ref.shape[0]
    r_tot = ybuf.shape[1]
    slot = i % 2

    n_groups = r_tot // SUBLANES

    def issue(tab_ref, s):
        def fetch_group(gi, carry):
            pltpu.make_async_copy(y_hbm.at[pl.ds(pl.multiple_of(tab_ref[gi], SUBLANES), SUBLANES)],
                                  ybuf.at[s, pl.ds(pl.multiple_of(gi * SUBLANES, SUBLANES), SUBLANES)],
                                  sem.at[s]).start()
            return carry
        lax.fori_loop(0, tab_ref[n_groups] // SUBLANES, fetch_group, 0)

    @pl.when(i == 0)
    def _():
        ybuf[...] = jnp.zeros_like(ybuf)
        issue(tabc_ref, 0)

    @pl.when(i + 1 < n)
    def _():
        issue(tabn_ref, 1 - slot)

    _seg_copy(y_hbm, ybuf.at[slot], 0, 0, tabc_ref[n_groups], _pow2_floor(r_tot), sem.at[slot], wait=True)

    lp = lp_ref[...]
    w = w_ref[...]
    cols = lax.broadcasted_iota(I32, (tt, r_tot), 1)
    p = jnp.zeros((tt, r_tot), F32)
    for k in range(TOP_K):
        p = jnp.where(cols == lp[:, k:k + 1], w[:, k:k + 1], p)
    pb = p.astype(BF16)
    ylo, yhi = _unpack_rows(ybuf[slot])
    routed = jnp.concatenate([jnp.dot(pb, ylo, preferred_element_type=F32),
                              jnp.dot(pb, yhi, preferred_element_type=F32)], axis=1)

    x = x_ref[...]
    sh = sd_ref.shape[0]
    sg = jnp.dot(x.astype(BF16), sgu_ref[...], preferred_element_type=F32)
    a = (_silu(sg[:, :sh]) * sg[:, sh:]).astype(BF16)
    shared = jnp.dot(a, sd_ref[...], preferred_element_type=F32)
    o_ref[...] = _layer_norm(alpha * x + (routed + shared), g_ref[...], b_ref[...])


def _combine(x2d, wts, lpos, tab, y_sorted, s_gu_bf, s_down_bf, g, b, alpha, n_exp, tt):
    t, d = x2d.shape
    n = t // tt
    sh = s_down_bf.shape[0]
    return pl.pallas_call(
        functools.partial(_combine_kernel, alpha=alpha),
        grid=(n,),
        in_specs=[pl.BlockSpec((tab.size // n,), lambda i: (i,), memory_space=pltpu.SMEM),
                  pl.BlockSpec((tab.size // n,), lambda i: (jnp.minimum(i + 1, n - 1),), memory_space=pltpu.SMEM),
                  pl.BlockSpec((tt, TOP_K), lambda i: (i, 0)),
                  pl.BlockSpec((tt, TOP_K), lambda i: (i, 0)),
                  pl.BlockSpec((tt, d), lambda i: (i, 0)),
                  pl.BlockSpec((d, 2 * sh), lambda i: (0, 0)),
                  pl.BlockSpec((sh, d), lambda i: (0, 0)),
                  pl.BlockSpec((1, d), lambda i: (0, 0)),
                  pl.BlockSpec((1, d), lambda i: (0, 0)),
                  pl.BlockSpec(memory_space=pl.ANY)],
        out_specs=pl.BlockSpec((tt, d), lambda i: (i, 0)),
        out_shape=jax.ShapeDtypeStruct((t, d), F32),
        scratch_shapes=[pltpu.VMEM((2, _tile_rows(tt, n_exp)[0], d // 2), U32), pltpu.SemaphoreType.DMA((2,))],
        compiler_params=_cparams("arbitrary"),
    )(tab, tab, lpos, wts, x2d, s_gu_bf, s_down_bf,
      g.reshape(1, d).astype(F32), b.reshape(1, d).astype(F32), y_sorted)


def _moe_layer(x2d, xbf, router_w, router_bias, w_gu, w_down, layer, s_gu, s_down, g, b, alpha, bm=512, tt=256):
    t, d = x2d.shape
    e = router_w.shape[1]
    tt = _tile(t, tt)
    nt = t // tt
    eidx_t, wts_t = _router(x2d, router_w, router_bias)
    eidx, wts = eidx_t.T, wts_t.T

    onehot = jnp.sum((eidx[:, :, None] == jnp.arange(e, dtype=I32)).astype(I32), axis=1)
    oh = onehot.reshape(nt, tt, e)
    rank_in_tile = jnp.cumsum(oh, axis=1) - oh
    cnt = jnp.sum(oh, axis=1)
    run = ((cnt + SUBLANES - 1) // SUBLANES) * SUBLANES
    before = jnp.cumsum(run, axis=0) - run
    off = jnp.cumsum(run, axis=1) - run
    counts = jnp.sum(run, axis=0)
    padded = ((counts + bm - 1) // bm) * bm
    ends = jnp.cumsum(padded)
    starts = ends - padded
    lpos = jnp.take_along_axis((off[:, None, :] + rank_in_tile).reshape(t, e), eidx, axis=1).astype(I32)
    totals = jnp.sum(run, axis=1)
    n_groups = _tile_rows(tt, e)[0] // SUBLANES
    grow = jnp.arange(n_groups, dtype=I32) * SUBLANES
    gexp = jnp.sum(((off + run)[:, None, :] <= grow[None, :, None]).astype(I32), axis=2)
    gexp = jnp.minimum(gexp, e - 1)
    gdst = (jnp.take_along_axis(starts[None, :] + before, gexp, axis=1)
            + grow[None, :] - jnp.take_along_axis(off, gexp, axis=1))
    gdst = jnp.where(grow[None, :] < totals[:, None], gdst, 0)
    tab_w = max(LANES, 2 * _pow2_floor(n_groups))
    tab = jnp.concatenate([gdst, totals[:, None], jnp.zeros((nt, tab_w - n_groups - 1), I32)], axis=1)
    tab = tab.reshape(-1).astype(I32)
    pad_tab = jnp.concatenate([starts + counts, padded - counts, totals]).astype(I32)
    m_pad = -(-(t * TOP_K + (SUBLANES - 1) * nt * e) // bm) * bm + e * bm
    nb = m_pad // bm
    block_start = jnp.arange(nb, dtype=I32) * bm
    block_e = jnp.minimum(jnp.sum((ends[None, :] <= block_start[:, None]).astype(I32), axis=1), e - 1).astype(I32)
    nvalid = (ends[-1:] // bm).astype(I32)

    xs = _dispatch(xbf, tab, pad_tab, lpos.T, m_pad, e, tt, bm)
    y_sorted = _experts(xs, block_e, nvalid, w_gu, w_down, layer, bm)
    return _combine(x2d, wts, lpos, tab, y_sorted, s_gu.astype(BF16), s_down.astype(BF16), g, b, alpha, e, tt)


def _proj_rope_kernel(x_ref, w_ref, cos_ref, sin_ref, o_ref):
    y = jnp.dot(x_ref[...].astype(BF16), w_ref[...], preferred_element_type=F32)
    tm, n = y.shape
    tw = cos_ref.shape[1]
    lane = lax.broadcasted_iota(I32, (tm, LANES), 1)
    first_half = (lane % HD_B) < (HD_B // 2)
    for c in range(n // LANES):
        yc = y[:, c * LANES:(c + 1) * LANES]
        off = (c * LANES) % tw
        rot = jnp.where(first_half, pltpu.roll(yc, LANES - HD_B // 2, axis=1), pltpu.roll(yc, HD_B // 2, axis=1))
        yc = yc * cos_ref[:, off:off + LANES] + rot * sin_ref[:, off:off + LANES]
        o_ref[:, c * LANES:(c + 1) * LANES] = yc.astype(o_ref.dtype)


def _proj_rope(x2d, w_bf, cos, sin, seq, tm=512, tn=1024):
    t, d = x2d.shape
    n = w_bf.shape[1]
    tw = cos.shape[1]
    tm, tn = _tile(seq, tm), _tile(n, tn)
    assert tn % tw == 0 and tw % LANES == 0
    ns = seq // tm
    return pl.pallas_call(
        _proj_rope_kernel,
        grid=(n // tn, t // tm),
        in_specs=[pl.BlockSpec((tm, d), lambda j, i: (i, 0)),
                  pl.BlockSpec((d, tn), lambda j, i: (0, j)),
                  pl.BlockSpec((tm, tw), lambda j, i: (i % ns, 0)),
                  pl.BlockSpec((tm, tw), lambda j, i: (i % ns, 0))],
        out_specs=pl.BlockSpec((tm, tn), lambda j, i: (i, j)),
        out_shape=jax.ShapeDtypeStruct((t, n), BF16),
        compiler_params=_cparams("arbitrary", "arbitrary"),
    )(x2d, w_bf, cos, sin)


def _rope_tables(seq, scale, heads, plain_cols=0):
    half = HD_B // 2
    inv = 1.0 / (ROPE_THETA ** (jnp.arange(half, dtype=F32) / half))
    ang = jnp.arange(seq, dtype=F32)[:, None] * inv[None, :]
    cos, sin = jnp.cos(ang), jnp.sin(ang)
    cos_t = jnp.tile(jnp.concatenate([cos, cos], axis=1), (1, heads)) * scale
    sin_t = jnp.tile(jnp.concatenate([-sin, sin], axis=1), (1, heads)) * scale
    cos_t = jnp.concatenate([cos_t, jnp.ones((seq, plain_cols), F32)], axis=1)
    sin_t = jnp.concatenate([sin_t, jnp.zeros((seq, plain_cols), F32)], axis=1)
    return cos_t, sin_t


def _swa_kernel(q_ref, kc_ref, kp_ref, vc_ref, vp_ref, sink_ref, o_ref, *, groups):
    qi = pl.program_id(2)
    tq = q_ref.shape[0]
    w = WINDOW
    row = lax.broadcasted_iota(I32, (w, 2 * w), 0)
    col = lax.broadcasted_iota(I32, (w, 2 * w), 1)
    band = (col > row) & (col <= row + w)
    neg = jnp.float32(-jnp.inf)
    sink = jnp.stack([sink_ref[g:g + 1, 0:1] for g in range(groups)])
    for j in range(tq // w):
        if j == 0:
            keys = [jnp.concatenate([kp_ref[s], kc_ref[s, :w, :]], axis=0) for s in range(2)]
            vals = [jnp.concatenate([vp_ref[s], vc_ref[s, :w, :]], axis=0) for s in range(2)]
            mask = band & ((col >= w) | (qi > 0))
        else:
            keys = [kc_ref[s, (j - 1) * w:(j + 1) * w, :] for s in range(2)]
            vals = [vc_ref[s, (j - 1) * w:(j + 1) * w, :] for s in range(2)]
            mask = band
        scs = []
        for p in range(groups // 2):
            qs = q_ref[j * w:(j + 1) * w, p * LANES:(p + 1) * LANES]
            for s in range(2):
                scs.append(lax.dot_general(qs, keys[s], (((1,), (1,)), ((), ())), preferred_element_type=F32))
        sc = jnp.where(mask[None], jnp.stack(scs), neg)
        m = jnp.maximum(jnp.max(jnp.maximum(sc[..., :w], sc[..., w:]), axis=-1, keepdims=True), sink)
        pr = jnp.exp(sc - m)
        den = jnp.sum(pr[..., :w] + pr[..., w:], axis=-1, keepdims=True) + jnp.exp(sink - m)
        pr = (pr * (1.0 / den)).astype(BF16)
        for p in range(groups // 2):
            out = (jnp.dot(pr[2 * p], vals[0], preferred_element_type=F32)
                   + jnp.dot(pr[2 * p + 1], vals[1], preferred_element_type=F32))
            o_ref[j * w:(j + 1) * w, p * LANES:(p + 1) * LANES] = out.astype(o_ref.dtype)


def _swa_attn(q, kk, vv, sink_tab, batch, seq, kvh, groups, tq=512):
    t = q.shape[0]
    tq = _tile(seq, tq)
    assert tq % WINDOW == 0 and groups % 2 == 0 and 2 * HD_B == LANES
    nq = seq // tq
    per = tq // WINDOW
    gw = groups * HD_B
    cur = pl.BlockSpec((None, None, 2, tq, LANES), lambda b, h, i: (b, h, 0, i, 0))
    prev = pl.BlockSpec((None, None, 2, WINDOW, LANES), lambda b, h, i: (b, h, 0, jnp.maximum(i * per - 1, 0), 0))
    return pl.pallas_call(
        functools.partial(_swa_kernel, groups=groups),
        grid=(batch, kvh, nq),
        in_specs=[pl.BlockSpec((tq, gw), lambda b, h, i: (b * nq + i, h)),
                  cur, prev, cur, prev,
                  pl.BlockSpec((None, groups, LANES), lambda b, h, i: (h, 0, 0))],
        out_specs=pl.BlockSpec((tq, gw), lambda b, h, i: (b * nq + i, h)),
        out_shape=jax.ShapeDtypeStruct((t, kvh * gw), BF16),
        compiler_params=_cparams("arbitrary", "arbitrary", "arbitrary"),
    )(q, kk, kk, vv, vv, sink_tab)


def _lane_pairs(a, batch, seq, kvh):
    a4 = a.reshape(batch, seq, kvh, HD_B).transpose(0, 2, 1, 3)
    z = jnp.zeros_like(a4)
    return jnp.stack([jnp.concatenate([a4, z], axis=-1), jnp.concatenate([z, a4], axis=-1)], axis=2)


def kernel(x, a_w_in, a_b_f, a_w_o, kv_w, b_w_q, b_sinks, b_w_o, router_w, router_bias, moe_w_gu, moe_w_down,
           shared_w_gu, shared_w_down, ln1_g, ln1_b, ln2_g, ln2_b):
    batch, seq, d = x.shape
    n_a, n_b = a_w_in.shape[0], b_w_q.shape[0]
    depth = n_a + n_b
    alpha = float((2 * depth) ** 0.25)
    heads_a = a_b_f.shape[1]
    hd_a = d // heads_a
    heads_b = b_sinks.shape[1]
    kvh = kv_w.shape[1] // (2 * HD_B)
    groups = heads_b // kvh
    t = batch * seq

    x2d = x.reshape(t, d).astype(F32)
    kk = vv = None
    cos_q, sin_q = _rope_tables(seq, 1.0 / math.sqrt(HD_B), LANES // HD_B)
    cos_k, sin_k = _rope_tables(seq, 1.0, kvh, plain_cols=kvh * HD_B)
    for i in range(depth):
        if i < n_a:
            qkv = _proj(x2d, a_w_in, i, 3 * d, scaled_cols=d, scale=LOG2E / math.sqrt(hd_a))
            cum = _fox_cum(x2d, a_w_in, i, 3 * d, a_b_f[i], batch, seq)
            tb = _tile(seq, 512)
            cum_t = cum[:, :heads_a].reshape(batch, seq, heads_a).transpose(0, 2, 1)
            cum_t = cum_t.reshape(batch, heads_a, seq // tb, 1, tb)
            attn = _fox_attn(qkv, cum_t, batch, seq, heads_a, hd_a, tb=tb)
            w_o = a_w_o[i]
        else:
            j = i - n_a
            q = _proj_rope(x2d, b_w_q[j].astype(BF16), cos_q, sin_q, seq)
            sink_tab = jnp.broadcast_to(b_sinks[j].astype(F32).reshape(kvh, groups, 1), (kvh, groups, LANES))
            attn = _swa_attn(q, kk, vv, sink_tab, batch, seq, kvh, groups)
            w_o = b_w_o[j]
        x2d, xbf = _oproj_ln(attn, w_o.astype(BF16), x2d, ln1_g[i], ln1_b[i], alpha)
        x2d = _moe_layer(x2d, xbf, router_w[i], router_bias[i], moe_w_gu, moe_w_down, i,
                         shared_w_gu[i], shared_w_down[i], ln2_g[i], ln2_b[i], alpha)
        if i == n_a - 1:
            kv = _proj_rope(x2d, kv_w.astype(BF16), cos_k, sin_k, seq, tn=kv_w.shape[1])
            kk = _lane_pairs(kv[:, :kvh * HD_B], batch, seq, kvh)
            vv = _lane_pairs(kv[:, kvh * HD_B:], batch, seq, kvh)
    return x2d.reshape(batch, seq, d).astype(x.dtype)
```

```python
import functools
import math

import jax
import jax.numpy as jnp
from jax import lax
from jax.experimental import pallas as pl
from jax.experimental.pallas import tpu as pltpu

F32 = jnp.float32
BF16 = jnp.bfloat16
U32 = jnp.uint32
I32 = jnp.int32

LANES = 128
SUBLANES = 8
HD_B = 64
WINDOW = 128
ROPE_THETA = 10000.0
TOP_K = 8
N_GROUPS = 8
TOPK_GROUPS = 4
ROUTED_SCALE = 2.5
LN_EPS = 1e-5
VMEM_LIMIT_BYTES = 56 * 1024 * 1024
HI_MASK = 0xFFFF0000
LOG2E = math.log2(math.e)


def _cparams(*sem):
    return pltpu.CompilerParams(dimension_semantics=sem, vmem_limit_bytes=VMEM_LIMIT_BYTES)


def _tile(dim, pref):
    t = min(dim, pref)
    assert dim % t == 0, (dim, pref)
    return t


def _silu(x):
    return x * jax.nn.sigmoid(x)


def _layer_norm(z, g, b):
    mu = jnp.mean(z, axis=-1, keepdims=True)
    zc = z - mu
    var = jnp.mean(zc * zc, axis=-1, keepdims=True)
    return zc * lax.rsqrt(var + LN_EPS) * g + b


def _pack_halves(lo, hi):
    lo = lax.bitcast_convert_type(lo, U32)
    hi = lax.bitcast_convert_type(hi, U32)
    return (hi & jnp.uint32(HI_MASK)) | (lo >> jnp.uint32(16))


def _pack_rows(y):
    half = y.shape[1] // 2
    yb = y.astype(BF16).astype(F32)
    return _pack_halves(yb[:, :half], yb[:, half:])


def _unpack_rows(pk):
    lo = lax.bitcast_convert_type(pk << jnp.uint32(16), F32).astype(BF16)
    hi = lax.bitcast_convert_type(pk & jnp.uint32(HI_MASK), F32).astype(BF16)
    return lo, hi


def _proj_kernel(x_ref, w_ref, o_ref, w_bf, *, scaled_tiles, scale, cast_rows):
    j = pl.program_id(0)

    @pl.when(pl.program_id(1) == 0)
    def _():
        def cast(c, carry):
            r = pl.ds(pl.multiple_of(c * cast_rows, cast_rows), cast_rows)
            w_bf[r, :] = w_ref[r, :].astype(BF16)
            return carry
        lax.fori_loop(0, w_ref.shape[0] // cast_rows, cast, 0)

    y = jnp.dot(x_ref[...].astype(BF16), w_bf[...], preferred_element_type=F32)
    s = jnp.where(j < scaled_tiles, jnp.float32(scale), jnp.float32(1.0))
    o_ref[...] = (y * s).astype(o_ref.dtype)


def _proj(x2d, w, layer, n, *, scaled_cols, scale, tm=512, tn=1024):
    t, d = x2d.shape
    tm, tn = _tile(t, tm), _tile(scaled_cols, tn)
    assert n % tn == 0
    return pl.pallas_call(
        functools.partial(_proj_kernel, scaled_tiles=scaled_cols // tn, scale=scale, cast_rows=_tile(d, 256)),
        grid=(n // tn, t // tm),
        in_specs=[pl.BlockSpec((tm, d), lambda j, i: (i, 0)),
                  pl.BlockSpec((None, d, tn), lambda j, i: (layer, 0, j))],
        out_specs=pl.BlockSpec((tm, tn), lambda j, i: (i, j)),
        out_shape=jax.ShapeDtypeStruct((t, n), BF16),
        scratch_shapes=[pltpu.VMEM((d, tn), BF16)],
        compiler_params=_cparams("arbitrary", "arbitrary"),
    )(x2d, w)


def _fox_cum_kernel(x_ref, w_ref, b_ref, o_ref, carry_ref, *, heads):
    @pl.when(pl.program_id(1) == 0)
    def _():
        carry_ref[...] = jnp.zeros_like(carry_ref)

    ts = x_ref.shape[0]
    lane = lax.broadcasted_iota(I32, w_ref.shape, 1)
    w = jnp.where(lane < heads, w_ref[...], 0.0).astype(BF16)
    z = jnp.dot(x_ref[...].astype(BF16), w, preferred_element_type=F32) + b_ref[...]
    logf = -(jnp.maximum(-z, 0.0) + jnp.log1p(jnp.exp(-jnp.abs(z))))
    row = lax.broadcasted_iota(I32, (ts, ts), 0)
    col = lax.broadcasted_iota(I32, (ts, ts), 1)
    tril = (row >= col).astype(F32)
    cum = jnp.dot(tril, logf, preferred_element_type=F32, precision=lax.Precision.HIGHEST) + carry_ref[...]
    o_ref[...] = cum * LOG2E
    carry_ref[...] = cum[ts - 1:ts, :]


def _fox_cum(x2d, w_in, layer, col0, b_f, batch, seq, ts=512):
    t, d = x2d.shape
    h = w_in.shape[2] - col0
    assert h <= LANES and col0 % LANES == 0
    ts = _tile(seq, ts)
    b_pad = jnp.zeros((1, LANES), F32).at[0, :h].set(b_f.astype(F32))
    ns = seq // ts
    return pl.pallas_call(
        functools.partial(_fox_cum_kernel, heads=h),
        grid=(batch, ns),
        in_specs=[pl.BlockSpec((ts, d), lambda b, s: (b * ns + s, 0)),
                  pl.BlockSpec((None, d, LANES), lambda b, s: (layer, 0, col0 // LANES)),
                  pl.BlockSpec((1, LANES), lambda b, s: (0, 0))],
        out_specs=pl.BlockSpec((ts, LANES), lambda b, s: (b * ns + s, 0)),
        out_shape=jax.ShapeDtypeStruct((t, LANES), F32),
        scratch_shapes=[pltpu.VMEM((1, LANES), F32)],
        compiler_params=_cparams("arbitrary", "arbitrary"),
    )(x2d, w_in, b_pad)


def _fox_attn_kernel(q_ref, k_ref, v_ref, c_ref, o_ref, *, tb, hd):
    qi = pl.program_id(2)
    n_heads = q_ref.shape[1] // hd
    cols = [slice(h * hd, (h + 1) * hd) for h in range(n_heads)]
    qs = [q_ref[:, c] for c in cols]

    def scores(h, j):
        k = k_ref[pl.ds(pl.multiple_of(j * tb, tb), tb), cols[h]]
        s = lax.dot_general(qs[h], k, (((1,), (1,)), ((), ())), preferred_element_type=F32)
        return s - c_ref[h, j]

    def update(h, j, t, carry):
        m, l, acc = carry
        v = v_ref[pl.ds(pl.multiple_of(j * tb, tb), tb), cols[h]]
        m_new = jnp.maximum(m, jnp.max(t, axis=-1, keepdims=True))
        p = jnp.exp2(t - m_new)
        alpha = jnp.exp2(m - m_new)
        l = alpha * l + jnp.sum(p, axis=-1, keepdims=True)
        acc = alpha * acc + jnp.dot(p.astype(BF16), v, preferred_element_type=F32)
        return m_new, l, acc

    def body(j, carries):
        return tuple(update(h, j, scores(h, j), carries[h]) for h in range(n_heads))

    init = (jnp.full((tb, 1), -jnp.inf, F32), jnp.zeros((tb, 1), F32), jnp.zeros((tb, hd), F32))
    carries = lax.fori_loop(0, qi, body, (init,) * n_heads)
    row = lax.broadcasted_iota(I32, (tb, tb), 0)
    col = lax.broadcasted_iota(I32, (tb, tb), 1)
    for h in range(n_heads):
        t = jnp.where(row >= col, scores(h, qi), -jnp.inf)
        _, l, acc = update(h, qi, t, carries[h])
        o_ref[:, cols[h]] = (acc / l).astype(o_ref.dtype)


def _fox_attn(qkv, cum_t, batch, seq, heads, hd, tb=512, hp=4):
    t = qkv.shape[0]
    d = heads * hd
    tb = _tile(seq, tb)
    nq = seq // tb
    hp = _tile(heads, hp)
    hg = heads // hp
    assert hd % LANES == 0
    return pl.pallas_call(
        functools.partial(_fox_attn_kernel, tb=tb, hd=hd),
        grid=(batch, hg, nq),
        in_specs=[pl.BlockSpec((tb, hp * hd), lambda b, h, i: (b * nq + i, h)),
                  pl.BlockSpec((seq, hp * hd), lambda b, h, i: (b, hg + h)),
                  pl.BlockSpec((seq, hp * hd), lambda b, h, i: (b, 2 * hg + h)),
                  pl.BlockSpec((None, hp, nq, 1, tb), lambda b, h, i: (b, h, 0, 0, 0))],
        out_specs=pl.BlockSpec((tb, hp * hd), lambda b, h, i: (b * nq + i, h)),
        out_shape=jax.ShapeDtypeStruct((t, d), BF16),
        compiler_params=_cparams("arbitrary", "arbitrary", "arbitrary"),
    )(qkv, qkv, qkv, cum_t)


def _oproj_ln_kernel(a_ref, w_ref, x_ref, g_ref, b_ref, o_ref, ob_ref, *, alpha):
    h = jnp.dot(a_ref[...], w_ref[...], preferred_element_type=F32)
    y = _layer_norm(alpha * x_ref[...] + h, g_ref[...], b_ref[...])
    o_ref[...] = y
    ob_ref[...] = y.astype(BF16)


def _oproj_ln(a_bf, w_bf, x2d, g, b, alpha, tm=512):
    t, d = x2d.shape
    kk = a_bf.shape[1]
    tm = _tile(t, tm)
    return pl.pallas_call(
        functools.partial(_oproj_ln_kernel, alpha=alpha),
        grid=(t // tm,),
        in_specs=[pl.BlockSpec((tm, kk), lambda i: (i, 0)),
                  pl.BlockSpec((kk, d), lambda i: (0, 0)),
                  pl.BlockSpec((tm, d), lambda i: (i, 0)),
                  pl.BlockSpec((1, d), lambda i: (0, 0)),
                  pl.BlockSpec((1, d), lambda i: (0, 0))],
        out_specs=[pl.BlockSpec((tm, d), lambda i: (i, 0)),
                   pl.BlockSpec((tm, d), lambda i: (i, 0))],
        out_shape=[jax.ShapeDtypeStruct((t, d), F32), jax.ShapeDtypeStruct((t, d), BF16)],
        compiler_params=_cparams("arbitrary"),
    )(a_bf, w_bf, x2d, g.reshape(1, d).astype(F32), b.reshape(1, d).astype(F32))


def _first_max(vals, idx, sentinel):
    m = jnp.max(vals, axis=0, keepdims=True)
    first = jnp.min(jnp.where(vals == m, idx, sentinel), axis=0, keepdims=True)
    return m, first


def _router_kernel(x_ref, rw_ref, bias_ref, eidx_ref, wts_ref):
    e = bias_ref.shape[0]
    tm = x_ref.shape[0]
    per = e // N_GROUPS
    logits = jnp.dot(x_ref[...], rw_ref[...], preferred_element_type=F32, precision=lax.Precision.HIGHEST)
    scores = jax.nn.sigmoid(logits.T[:e, :])
    choice = scores + bias_ref[...]
    neg = jnp.float32(-jnp.inf)

    ip = lax.broadcasted_iota(I32, (per, tm), 0)
    gs = []
    for g in range(N_GROUPS):
        c = choice[g * per:(g + 1) * per, :]
        m1, f1 = _first_max(c, ip, per)
        m2 = jnp.max(jnp.where(ip == f1, neg, c), axis=0, keepdims=True)
        gs.append(m1 + m2)
    gscore = jnp.concatenate(gs, axis=0)

    ig = lax.broadcasted_iota(I32, (N_GROUPS, tm), 0)
    sel = jnp.zeros((N_GROUPS, tm), F32)
    for _ in range(TOPK_GROUPS):
        _, fg = _first_max(gscore, ig, N_GROUPS)
        hit = ig == fg
        sel = jnp.where(hit, 1.0, sel)
        gscore = jnp.where(hit, neg, gscore)

    masked = jnp.concatenate(
        [jnp.where(sel[g:g + 1, :] > 0.0, choice[g * per:(g + 1) * per, :], neg) for g in range(N_GROUPS)],
        axis=0)
    ie = lax.broadcasted_iota(I32, (e, tm), 0)
    idxs, ws = [], []
    for _ in range(TOP_K):
        _, fe = _first_max(masked, ie, e)
        hit = ie == fe
        idxs.append(fe)
        ws.append(jnp.sum(jnp.where(hit, scores, 0.0), axis=0, keepdims=True))
        masked = jnp.where(hit, neg, masked)
    w = jnp.concatenate(ws, axis=0)
    eidx_ref[...] = jnp.concatenate(idxs, axis=0)
    wts_ref[...] = w / jnp.sum(w, axis=0, keepdims=True) * ROUTED_SCALE


def _router(x2d, router_w, router_bias, tm=512):
    t, d = x2d.shape
    e = router_w.shape[1]
    assert e % N_GROUPS == 0 and (e // N_GROUPS) % SUBLANES == 0
    tm = _tile(t, tm)
    ep = -(-e // LANES) * LANES
    rw_pad = jnp.zeros((d, ep), F32).at[:, :e].set(router_w.astype(F32))
    return pl.pallas_call(
        _router_kernel,
        grid=(t // tm,),
        in_specs=[pl.BlockSpec((tm, d), lambda i: (i, 0)),
                  pl.BlockSpec((d, ep), lambda i: (0, 0)),
                  pl.BlockSpec((e, 1), lambda i: (0, 0))],
        out_specs=[pl.BlockSpec((TOP_K, tm), lambda i: (0, i)),
                   pl.BlockSpec((TOP_K, tm), lambda i: (0, i))],
        out_shape=[jax.ShapeDtypeStruct((TOP_K, t), I32), jax.ShapeDtypeStruct((TOP_K, t), F32)],
        compiler_params=_cparams("arbitrary"),
    )(x2d, rw_pad, router_bias.reshape(e, 1).astype(F32))


def _seg_copy(src, dst, src_off, dst_off, length, max_len, sem, wait=False):
    done = 0
    b = max_len
    while b >= SUBLANES:
        piece = length & b

        @pl.when(piece != 0)
        def _(b=b, done=done):
            cp = pltpu.make_async_copy(src.at[pl.ds(pl.multiple_of(src_off + done, SUBLANES), b)],
                                       dst.at[pl.ds(pl.multiple_of(dst_off + done, SUBLANES), b)], sem)
            if wait:
                cp.wait()
            else:
                cp.start()
        done = done + piece
        b //= 2


def _pow2_floor(n):
    return 1 << (n.bit_length() - 1)


def _dispatch_kernel(tab_ref, pad_ref, lpt_ref, x_ref, o_hbm, xs, zbuf, sem, *, n_exp, rc):
    i = pl.program_id(0)
    n = pl.num_programs(0)
    slot = i % 2
    tt = x_ref.shape[0]
    r_tot, half = xs.shape[1], xs.shape[2]

    def wait_slot(s, step):
        _seg_copy(xs.at[s], o_hbm, 0, 0, pad_ref[2 * n_exp + step], _pow2_floor(r_tot), sem.at[s], wait=True)

    @pl.when(i >= 2)
    def _():
        wait_slot(slot, i - 2)

    xlo, xhi = x_ref[:, :half], x_ref[:, half:]
    lpt = lpt_ref[...]
    for c in range(r_tot // rc):
        rows = lax.broadcasted_iota(I32, (rc, tt), 0) + c * rc
        hit = rows == lpt[0:1, :]
        for k in range(1, TOP_K):
            hit = hit | (rows == lpt[k:k + 1, :])
        g = jnp.where(hit, 1.0, 0.0).astype(BF16)
        lo = jnp.dot(g, xlo, preferred_element_type=F32)
        hi = jnp.dot(g, xhi, preferred_element_type=F32)
        xs[slot, c * rc:(c + 1) * rc, :] = _pack_halves(lo, hi)

    for e in range(n_exp):
        _seg_copy(xs.at[slot], o_hbm, tab_ref[2 * n_exp + e], tab_ref[e], tab_ref[n_exp + e], tt, sem.at[slot])

    @pl.when(i == 0)
    def _():
        zbuf[...] = jnp.zeros_like(zbuf)
        for w in (False, True):
            for e in range(n_exp):
                _seg_copy(zbuf, o_hbm, 0, pad_ref[e], pad_ref[n_exp + e], zbuf.shape[0] // 2, sem.at[2], wait=w)

    @pl.when(i == n - 1)
    def _():
        @pl.when(n >= 2)
        def _():
            wait_slot(1 - slot, i - 1)
        wait_slot(slot, i)


def _tile_rows(tt, n_exp, rc=512):
    r = tt * TOP_K + (SUBLANES - 1) * n_exp
    rc = min(rc, tt * TOP_K)
    return -(-r // rc) * rc, rc


def _dispatch(xbf, tab, pad_tab, lpos_t, m_pad, n_exp, tt, bm):
    t, d = xbf.shape
    assert tt & (tt - 1) == 0 and bm & (bm - 1) == 0 and d % 2 == 0 and tt % SUBLANES == 0
    r_tot, rc = _tile_rows(tt, n_exp)
    return pl.pallas_call(
        functools.partial(_dispatch_kernel, n_exp=n_exp, rc=rc),
        grid=(t // tt,),
        in_specs=[pl.BlockSpec((4 * n_exp,), lambda i: (i,), memory_space=pltpu.SMEM),
                  pl.BlockSpec(memory_space=pltpu.SMEM),
                  pl.BlockSpec((TOP_K, tt), lambda i: (0, i)),
                  pl.BlockSpec((tt, d), lambda i: (i, 0))],
        out_specs=pl.BlockSpec(memory_space=pl.ANY),
        out_shape=jax.ShapeDtypeStruct((m_pad, d // 2), U32),
        scratch_shapes=[pltpu.VMEM((2, r_tot, d // 2), U32), pltpu.VMEM((bm, d // 2), U32),
                        pltpu.SemaphoreType.DMA((3,))],
        compiler_params=_cparams("arbitrary"),
    )(tab, pad_tab, lpos_t, xbf)


def _expert_kernel(be_ref, nv_ref, br_ref, xs_ref, wgu_ref, wd_ref, y_ref, wgu_bf, wd_bf, *, cast_rows):
    i = pl.program_id(0)

    @pl.when(i < nv_ref[0])
    def _():
        prev = be_ref[jnp.maximum(i - 1, 0)]

        @pl.when((i == 0) | (be_ref[i] != prev))
        def _():
            def cast_gu(c, carry):
                r = pl.ds(pl.multiple_of(c * cast_rows, cast_rows), cast_rows)
                wgu_bf[r, :] = wgu_ref[r, :].astype(BF16)
                return carry
            lax.fori_loop(0, wgu_ref.shape[0] // cast_rows, cast_gu, 0)
            wd_bf[...] = wd_ref[...].astype(BF16)

        half = xs_ref.shape[1]
        eh = wd_ref.shape[0]
        bm = xs_ref.shape[0]

        def mlp(rows):
            lo, hi = _unpack_rows(xs_ref[:rows, :])
            gu = (jnp.dot(lo, wgu_bf[:half, :], preferred_element_type=F32)
                  + jnp.dot(hi, wgu_bf[half:, :], preferred_element_type=F32))
            a = (_silu(gu[:, :eh]) * gu[:, eh:]).astype(BF16)
            y_ref[:rows, :] = _pack_rows(jnp.dot(a, wd_bf[...], preferred_element_type=F32))

        @pl.when(br_ref[i] > bm // 2)
        def _():
            mlp(bm)

        @pl.when(br_ref[i] <= bm // 2)
        def _():
            mlp(bm // 2)
            y_ref[bm // 2:, :] = jnp.zeros((bm - bm // 2, half), U32)


def _experts(xs, block_e, nvalid, block_rows, w_gu, w_down, layer, bm):
    m_pad, dh = xs.shape
    _, e, d, eh2 = w_gu.shape
    eh = w_down.shape[2]
    nb = m_pad // bm
    cast_rows = _tile(d, 256)

    def blk(i, be, nv, br):
        return jnp.minimum(i, nv[0] - 1)

    grid_spec = pltpu.PrefetchScalarGridSpec(
        num_scalar_prefetch=3,
        grid=(nb,),
        in_specs=[pl.BlockSpec((bm, dh), lambda i, be, nv, br: (blk(i, be, nv, br), 0)),
                  pl.BlockSpec((None, None, d, eh2), lambda i, be, nv, br: (layer, be[blk(i, be, nv, br)], 0, 0)),
                  pl.BlockSpec((None, None, eh, d), lambda i, be, nv, br: (layer, be[blk(i, be, nv, br)], 0, 0))],
        out_specs=pl.BlockSpec((bm, dh), lambda i, be, nv, br: (blk(i, be, nv, br), 0)),
        scratch_shapes=[pltpu.VMEM((d, eh2), BF16), pltpu.VMEM((eh, d), BF16)],
    )
    return pl.pallas_call(
        functools.partial(_expert_kernel, cast_rows=cast_rows),
        grid_spec=grid_spec,
        out_shape=jax.ShapeDtypeStruct((m_pad, dh), U32),
        compiler_params=_cparams("arbitrary"),
    )(block_e, nvalid, block_rows, xs, w_gu, w_down)


def _combine_kernel(tabc_ref, tabn_ref, lp_ref, w_ref, x_ref, sgu_ref, sd_ref, g_ref, b_ref, y_hbm,
                    o_ref, ybuf, sem, *, alpha, n_exp):
    i = pl.program_id(0)
    n = pl.num_programs(0)
    tt = x_ref.shape[0]
    r_tot = ybuf.shape[1]
    slot = i % 2

    def issue(tab_ref, s):
        for e in range(n_exp):
            _seg_copy(y_hbm, ybuf.at[s], tab_ref[e], tab_ref[2 * n_exp + e], tab_ref[n_exp + e], tt, sem.at[s])

    @pl.when(i == 0)
    def _():
        ybuf[...] = jnp.zeros_like(ybuf)
        issue(tabc_ref, 0)

    @pl.when(i + 1 < n)
    def _():
        issue(tabn_ref, 1 - slot)

    _seg_copy(y_hbm, ybuf.at[slot], 0, 0, tabc_ref[3 * n_exp], _pow2_floor(r_tot), sem.at[slot], wait=True)

    lp = lp_ref[...]
    w = w_ref[...]
    cols = lax.broadcasted_iota(I32, (tt, r_tot), 1)
    p = jnp.zeros((tt, r_tot), F32)
    for k in range(TOP_K):
        p = jnp.where(cols == lp[:, k:k + 1], w[:, k:k + 1], p)
    pb = p.astype(BF16)
    ylo, yhi = _unpack_rows(ybuf[slot])
    routed = jnp.concatenate([jnp.dot(pb, ylo, preferred_element_type=F32),
                              jnp.dot(pb, yhi, preferred_element_type=F32)], axis=1)

    x = x_ref[...]
    sh = sd_ref.shape[0]
    sg = jnp.dot(x.astype(BF16), sgu_ref[...], preferred_element_type=F32)
    a = (_silu(sg[:, :sh]) * sg[:, sh:]).astype(BF16)
    shared = jnp.dot(a, sd_ref[...], preferred_element_type=F32)
    o_ref[...] = _layer_norm(alpha * x + (routed + shared), g_ref[...], b_ref[...])


def _combine(x2d, wts, lpos, tab, y_sorted, s_gu_bf, s_down_bf, g, b, alpha, n_exp, tt):
    t, d = x2d.shape
    n = t // tt
    sh = s_down_bf.shape[0]
    return pl.pallas_call(
        functools.partial(_combine_kernel, alpha=alpha, n_exp=n_exp),
        grid=(n,),
        in_specs=[pl.BlockSpec((4 * n_exp,), lambda i: (i,), memory_space=pltpu.SMEM),
                  pl.BlockSpec((4 * n_exp,), lambda i: (jnp.minimum(i + 1, n - 1),), memory_space=pltpu.SMEM),
                  pl.BlockSpec((tt, TOP_K), lambda i: (i, 0)),
                  pl.BlockSpec((tt, TOP_K), lambda i: (i, 0)),
                  pl.BlockSpec((tt, d), lambda i: (i, 0)),
                  pl.BlockSpec((d, 2 * sh), lambda i: (0, 0)),
                  pl.BlockSpec((sh, d), lambda i: (0, 0)),
                  pl.BlockSpec((1, d), lambda i: (0, 0)),
                  pl.BlockSpec((1, d), lambda i: (0, 0)),
                  pl.BlockSpec(memory_space=pl.ANY)],
        out_specs=pl.BlockSpec((tt, d), lambda i: (i, 0)),
        out_shape=jax.ShapeDtypeStruct((t, d), F32),
        scratch_shapes=[pltpu.VMEM((2, _tile_rows(tt, n_exp)[0], d // 2), U32), pltpu.SemaphoreType.DMA((2,))],
        compiler_params=_cparams("arbitrary"),
    )(tab, tab, lpos, wts, x2d, s_gu_bf, s_down_bf,
      g.reshape(1, d).astype(F32), b.reshape(1, d).astype(F32), y_sorted)


def _moe_layer(x2d, xbf, router_w, router_bias, w_gu, w_down, layer, s_gu, s_down, g, b, alpha, bm=512, tt=256):
    t, d = x2d.shape
    e = router_w.shape[1]
    tt = _tile(t, tt)
    nt = t // tt
    eidx_t, wts_t = _router(x2d, router_w, router_bias)
    eidx, wts = eidx_t.T, wts_t.T

    onehot = jnp.sum((eidx[:, :, None] == jnp.arange(e, dtype=I32)).astype(I32), axis=1)
    oh = onehot.reshape(nt, tt, e)
    rank_in_tile = jnp.cumsum(oh, axis=1) - oh
    cnt = jnp.sum(oh, axis=1)
    run = ((cnt + SUBLANES - 1) // SUBLANES) * SUBLANES
    before = jnp.cumsum(run, axis=0) - run
    off = jnp.cumsum(run, axis=1) - run
    counts = jnp.sum(run, axis=0)
    padded = ((counts + bm - 1) // bm) * bm
    ends = jnp.cumsum(padded)
    starts = ends - padded
    lpos = jnp.take_along_axis((off[:, None, :] + rank_in_tile).reshape(t, e), eidx, axis=1).astype(I32)
    totals = jnp.sum(run, axis=1)
    spare = jnp.zeros_like(cnt).at[:, 0].set(totals)
    tab = jnp.concatenate([starts[None, :] + before, run, off, spare], axis=1).reshape(-1).astype(I32)
    pad_tab = jnp.concatenate([starts + counts, padded - counts, totals]).astype(I32)
    m_pad = -(-(t * TOP_K + (SUBLANES - 1) * nt * e) // bm) * bm + e * bm
    nb = m_pad // bm
    block_start = jnp.arange(nb, dtype=I32) * bm
    block_e = jnp.minimum(jnp.sum((ends[None, :] <= block_start[:, None]).astype(I32), axis=1), e - 1).astype(I32)
    nvalid = (ends[-1:] // bm).astype(I32)
    block_rows = jnp.sum(jnp.clip(jnp.minimum((starts + counts)[None, :], block_start[:, None] + bm)
                                  - jnp.maximum(starts[None, :], block_start[:, None]), 0, bm), axis=1).astype(I32)

    xs = _dispatch(xbf, tab, pad_tab, lpos.T, m_pad, e, tt, bm)
    y_sorted = _experts(xs, block_e, nvalid, block_rows, w_gu, w_down, layer, bm)
    return _combine(x2d, wts, lpos, tab, y_sorted, s_gu.astype(BF16), s_down.astype(BF16), g, b, alpha, e, tt)


def _proj_rope_kernel(x_ref, w_ref, cos_ref, sin_ref, o_ref):
    y = jnp.dot(x_ref[...].astype(BF16), w_ref[...], preferred_element_type=F32)
    tm, n = y.shape
    tw = cos_ref.shape[1]
    lane = lax.broadcasted_iota(I32, (tm, LANES), 1)
    first_half = (lane % HD_B) < (HD_B // 2)
    for c in range(n // LANES):
        yc = y[:, c * LANES:(c + 1) * LANES]
        off = (c * LANES) % tw
        rot = jnp.where(first_half, pltpu.roll(yc, LANES - HD_B // 2, axis=1), pltpu.roll(yc, HD_B // 2, axis=1))
        yc = yc * cos_ref[:, off:off + LANES] + rot * sin_ref[:, off:off + LANES]
        o_ref[:, c * LANES:(c + 1) * LANES] = yc.astype(o_ref.dtype)


def _proj_rope(x2d, w_bf, cos, sin, seq, tm=512, tn=1024):
    t, d = x2d.shape
    n = w_bf.shape[1]
    tw = cos.shape[1]
    tm, tn = _tile(seq, tm), _tile(n, tn)
    assert tn % tw == 0 and tw % LANES == 0
    ns = seq // tm
    return pl.pallas_call(
        _proj_rope_kernel,
        grid=(n // tn, t // tm),
        in_specs=[pl.BlockSpec((tm, d), lambda j, i: (i, 0)),
                  pl.BlockSpec((d, tn), lambda j, i: (0, j)),
                  pl.BlockSpec((tm, tw), lambda j, i: (i % ns, 0)),
                  pl.BlockSpec((tm, tw), lambda j, i: (i % ns, 0))],
        out_specs=pl.BlockSpec((tm, tn), lambda j, i: (i, j)),
        out_shape=jax.ShapeDtypeStruct((t, n), BF16),
        compiler_params=_cparams("arbitrary", "arbitrary"),
    )(x2d, w_bf, cos, sin)


def _rope_tables(seq, scale, heads, plain_cols=0):
    half = HD_B // 2
    inv = 1.0 / (ROPE_THETA ** (jnp.arange(half, dtype=F32) / half))
    ang = jnp.arange(seq, dtype=F32)[:, None] * inv[None, :]
    cos, sin = jnp.cos(ang), jnp.sin(ang)
    cos_t = jnp.tile(jnp.concatenate([cos, cos], axis=1), (1, heads)) * scale
    sin_t = jnp.tile(jnp.concatenate([-sin, sin], axis=1), (1, heads)) * scale
    cos_t = jnp.concatenate([cos_t, jnp.ones((seq, plain_cols), F32)], axis=1)
    sin_t = jnp.concatenate([sin_t, jnp.zeros((seq, plain_cols), F32)], axis=1)
    return cos_t, sin_t


def _swa_kernel(q_ref, kc_ref, kp_ref, vc_ref, vp_ref, sink_ref, o_ref, *, groups):
    qi = pl.program_id(2)
    tq = q_ref.shape[0]
    w = WINDOW
    row = lax.broadcasted_iota(I32, (w, 2 * w), 0)
    col = lax.broadcasted_iota(I32, (w, 2 * w), 1)
    band = (col > row) & (col <= row + w)
    neg = jnp.float32(-jnp.inf)
    sink = jnp.stack([sink_ref[g:g + 1, 0:1] for g in range(groups)])
    for j in range(tq // w):
        if j == 0:
            keys = [jnp.concatenate([kp_ref[s], kc_ref[s, :w, :]], axis=0) for s in range(2)]
            vals = [jnp.concatenate([vp_ref[s], vc_ref[s, :w, :]], axis=0) for s in range(2)]
            mask = band & ((col >= w) | (qi > 0))
        else:
            keys = [kc_ref[s, (j - 1) * w:(j + 1) * w, :] for s in range(2)]
            vals = [vc_ref[s, (j - 1) * w:(j + 1) * w, :] for s in range(2)]
            mask = band
        scs = []
        for p in range(groups // 2):
            qs = q_ref[j * w:(j + 1) * w, p * LANES:(p + 1) * LANES]
            for s in range(2):
                scs.append(lax.dot_general(qs, keys[s], (((1,), (1,)), ((), ())), preferred_element_type=F32))
        sc = jnp.where(mask[None], jnp.stack(scs), neg)
        m = jnp.maximum(jnp.max(jnp.maximum(sc[..., :w], sc[..., w:]), axis=-1, keepdims=True), sink)
        pr = jnp.exp(sc - m)
        den = jnp.sum(pr[..., :w] + pr[..., w:], axis=-1, keepdims=True) + jnp.exp(sink - m)
        pr = (pr * (1.0 / den)).astype(BF16)
        for p in range(groups // 2):
            out = (jnp.dot(pr[2 * p], vals[0], preferred_element_type=F32)
                   + jnp.dot(pr[2 * p + 1], vals[1], preferred_element_type=F32))
            o_ref[j * w:(j + 1) * w, p * LANES:(p + 1) * LANES] = out.astype(o_ref.dtype)


def _swa_attn(q, kk, vv, sink_tab, batch, seq, kvh, groups, tq=512):
    t = q.shape[0]
    tq = _tile(seq, tq)
    assert tq % WINDOW == 0 and groups % 2 == 0 and 2 * HD_B == LANES
    nq = seq // tq
    per = tq // WINDOW
    gw = groups * HD_B
    cur = pl.BlockSpec((None, None, 2, tq, LANES), lambda b, h, i: (b, h, 0, i, 0))
    prev = pl.BlockSpec((None, None, 2, WINDOW, LANES), lambda b, h, i: (b, h, 0, jnp.maximum(i * per - 1, 0), 0))
    return pl.pallas_call(
        functools.partial(_swa_kernel, groups=groups),
        grid=(batch, kvh, nq),
        in_specs=[pl.BlockSpec((tq, gw), lambda b, h, i: (b * nq + i, h)),
                  cur, prev, cur, prev,
                  pl.BlockSpec((None, groups, LANES), lambda b, h, i: (h, 0, 0))],
        out_specs=pl.BlockSpec((tq, gw), lambda b, h, i: (b * nq + i, h)),
        out_shape=jax.ShapeDtypeStruct((t, kvh * gw), BF16),
        compiler_params=_cparams("arbitrary", "arbitrary", "arbitrary"),
    )(q, kk, kk, vv, vv, sink_tab)


def _lane_pairs(a, batch, seq, kvh):
    a4 = a.reshape(batch, seq, kvh, HD_B).transpose(0, 2, 1, 3)
    z = jnp.zeros_like(a4)
    return jnp.stack([jnp.concatenate([a4, z], axis=-1), jnp.concatenate([z, a4], axis=-1)], axis=2)


def kernel(x, a_w_in, a_b_f, a_w_o, kv_w, b_w_q, b_sinks, b_w_o, router_w, router_bias, moe_w_gu, moe_w_down,
           shared_w_gu, shared_w_down, ln1_g, ln1_b, ln2_g, ln2_b):
    batch, seq, d = x.shape
    n_a, n_b = a_w_in.shape[0], b_w_q.shape[0]
    depth = n_a + n_b
    alpha = float((2 * depth) ** 0.25)
    heads_a = a_b_f.shape[1]
    hd_a = d // heads_a
    heads_b = b_sinks.shape[1]
    kvh = kv_w.shape[1] // (2 * HD_B)
    groups = heads_b // kvh
    t = batch * seq

    x2d = x.reshape(t, d).astype(F32)
    kk = vv = None
    cos_q, sin_q = _rope_tables(seq, 1.0 / math.sqrt(HD_B), LANES // HD_B)
    cos_k, sin_k = _rope_tables(seq, 1.0, kvh, plain_cols=kvh * HD_B)
    for i in range(depth):
        if i < n_a:
            qkv = _proj(x2d, a_w_in, i, 3 * d, scaled_cols=d, scale=LOG2E / math.sqrt(hd_a))
            cum = _fox_cum(x2d, a_w_in, i, 3 * d, a_b_f[i], batch, seq)
            tb = _tile(seq, 512)
            cum_t = cum[:, :heads_a].reshape(batch, seq, heads_a).transpose(0, 2, 1)
            cum_t = cum_t.reshape(batch, heads_a, seq // tb, 1, tb)
            attn = _fox_attn(qkv, cum_t, batch, seq, heads_a, hd_a, tb=tb)
            w_o = a_w_o[i]
        else:
            j = i - n_a
            q = _proj_rope(x2d, b_w_q[j].astype(BF16), cos_q, sin_q, seq)
            sink_tab = jnp.broadcast_to(b_sinks[j].astype(F32).reshape(kvh, groups, 1), (kvh, groups, LANES))
            attn = _swa_attn(q, kk, vv, sink_tab, batch, seq, kvh, groups)
            w_o = b_w_o[j]
        x2d, xbf = _oproj_ln(attn, w_o.astype(BF16), x2d, ln1_g[i], ln1_b[i], alpha)
        x2d = _moe_layer(x2d, xbf, router_w[i], router_bias[i], moe_w_gu, moe_w_down, i,
                         shared_w_gu[i], shared_w_down[i], ln2_g[i], ln2_b[i], alpha)
        if i == n_a - 1:
            kv = _proj_rope(x2d, kv_w.astype(BF16), cos_k, sin_k, seq, tn=kv_w.shape[1])
            kk = _lane_pairs(kv[:, :kvh * HD_B], batch, seq, kvh)
            vv = _lane_pairs(kv[:, kvh * HD_B:], batch, seq, kvh)
    return x2d.reshape(batch, seq, d).astype(x.dtype)
```

```python
import functools
import math

import jax
import jax.numpy as jnp
from jax import lax
from jax.experimental import pallas as pl
from jax.experimental.pallas import tpu as pltpu

F32 = jnp.float32
BF16 = jnp.bfloat16
U32 = jnp.uint32
I32 = jnp.int32

LANES = 128
SUBLANES = 8
HD_B = 64
WINDOW = 128
ROPE_THETA = 10000.0
TOP_K = 8
N_GROUPS = 8
TOPK_GROUPS = 4
ROUTED_SCALE = 2.5
LN_EPS = 1e-5
VMEM_LIMIT_BYTES = 56 * 1024 * 1024
HI_MASK = 0xFFFF0000
LOG2E = math.log2(math.e)


def _cparams(*sem):
    return pltpu.CompilerParams(dimension_semantics=sem, vmem_limit_bytes=VMEM_LIMIT_BYTES)


def _tile(dim, pref):
    t = min(dim, pref)
    assert dim % t == 0, (dim, pref)
    return t


def _silu(x):
    return x * jax.nn.sigmoid(x)


def _layer_norm(z, g, b):
    mu = jnp.mean(z, axis=-1, keepdims=True)
    zc = z - mu
    var = jnp.mean(zc * zc, axis=-1, keepdims=True)
    return zc * lax.rsqrt(var + LN_EPS) * g + b


def _pack_halves(lo, hi):
    lo = lax.bitcast_convert_type(lo, U32)
    hi = lax.bitcast_convert_type(hi, U32)
    return (hi & jnp.uint32(HI_MASK)) | (lo >> jnp.uint32(16))


def _pack_rows(y):
    half = y.shape[1] // 2
    yb = y.astype(BF16).astype(F32)
    return _pack_halves(yb[:, :half], yb[:, half:])


def _unpack_rows(pk):
    lo = lax.bitcast_convert_type(pk << jnp.uint32(16), F32).astype(BF16)
    hi = lax.bitcast_convert_type(pk & jnp.uint32(HI_MASK), F32).astype(BF16)
    return lo, hi


def _proj_kernel(x_ref, w_ref, o_ref, w_bf, *, scaled_tiles, scale, cast_rows):
    j = pl.program_id(0)

    @pl.when(pl.program_id(1) == 0)
    def _():
        def cast(c, carry):
            r = pl.ds(pl.multiple_of(c * cast_rows, cast_rows), cast_rows)
            w_bf[r, :] = w_ref[r, :].astype(BF16)
            return carry
        lax.fori_loop(0, w_ref.shape[0] // cast_rows, cast, 0)

    y = jnp.dot(x_ref[...].astype(BF16), w_bf[...], preferred_element_type=F32)
    s = jnp.where(j < scaled_tiles, jnp.float32(scale), jnp.float32(1.0))
    o_ref[...] = (y * s).astype(o_ref.dtype)


def _proj(x2d, w, layer, n, *, scaled_cols, scale, tm=512, tn=1024):
    t, d = x2d.shape
    tm, tn = _tile(t, tm), _tile(scaled_cols, tn)
    assert n % tn == 0
    return pl.pallas_call(
        functools.partial(_proj_kernel, scaled_tiles=scaled_cols // tn, scale=scale, cast_rows=_tile(d, 256)),
        grid=(n // tn, t // tm),
        in_specs=[pl.BlockSpec((tm, d), lambda j, i: (i, 0)),
                  pl.BlockSpec((None, d, tn), lambda j, i: (layer, 0, j))],
        out_specs=pl.BlockSpec((tm, tn), lambda j, i: (i, j)),
        out_shape=jax.ShapeDtypeStruct((t, n), BF16),
        scratch_shapes=[pltpu.VMEM((d, tn), BF16)],
        compiler_params=_cparams("arbitrary", "arbitrary"),
    )(x2d, w)


def _fox_cum_kernel(x_ref, w_ref, b_ref, o_ref, carry_ref, *, heads):
    @pl.when(pl.program_id(1) == 0)
    def _():
        carry_ref[...] = jnp.zeros_like(carry_ref)

    ts = x_ref.shape[0]
    lane = lax.broadcasted_iota(I32, w_ref.shape, 1)
    w = jnp.where(lane < heads, w_ref[...], 0.0).astype(BF16)
    z = jnp.dot(x_ref[...].astype(BF16), w, preferred_element_type=F32) + b_ref[...]
    logf = -(jnp.maximum(-z, 0.0) + jnp.log1p(jnp.exp(-jnp.abs(z))))
    row = lax.broadcasted_iota(I32, (ts, ts), 0)
    col = lax.broadcasted_iota(I32, (ts, ts), 1)
    tril = (row >= col).astype(F32)
    cum = jnp.dot(tril, logf, preferred_element_type=F32, precision=lax.Precision.HIGHEST) + carry_ref[...]
    o_ref[...] = cum * LOG2E
    carry_ref[...] = cum[ts - 1:ts, :]


def _fox_cum(x2d, w_in, layer, col0, b_f, batch, seq, ts=512):
    t, d = x2d.shape
    h = w_in.shape[2] - col0
    assert h <= LANES and col0 % LANES == 0
    ts = _tile(seq, ts)
    b_pad = jnp.zeros((1, LANES), F32).at[0, :h].set(b_f.astype(F32))
    ns = seq // ts
    return pl.pallas_call(
        functools.partial(_fox_cum_kernel, heads=h),
        grid=(batch, ns),
        in_specs=[pl.BlockSpec((ts, d), lambda b, s: (b * ns + s, 0)),
                  pl.BlockSpec((None, d, LANES), lambda b, s: (layer, 0, col0 // LANES)),
                  pl.BlockSpec((1, LANES), lambda b, s: (0, 0))],
        out_specs=pl.BlockSpec((ts, LANES), lambda b, s: (b * ns + s, 0)),
        out_shape=jax.ShapeDtypeStruct((t, LANES), F32),
        scratch_shapes=[pltpu.VMEM((1, LANES), F32)],
        compiler_params=_cparams("arbitrary", "arbitrary"),
    )(x2d, w_in, b_pad)


def _fox_attn_kernel(q_ref, k_ref, v_ref, c_ref, o_ref, *, tb, hd):
    qi = pl.program_id(2)
    n_heads = q_ref.shape[1] // hd
    cols = [slice(h * hd, (h + 1) * hd) for h in range(n_heads)]
    qs = [q_ref[:, c] for c in cols]

    def scores(h, j):
        k = k_ref[pl.ds(pl.multiple_of(j * tb, tb), tb), cols[h]]
        s = lax.dot_general(qs[h], k, (((1,), (1,)), ((), ())), preferred_element_type=F32)
        return s - c_ref[h, j]

    def update(h, j, t, carry):
        m, l, acc = carry
        v = v_ref[pl.ds(pl.multiple_of(j * tb, tb), tb), cols[h]]
        m_new = jnp.maximum(m, jnp.max(t, axis=-1, keepdims=True))
        p = jnp.exp2(t - m_new)
        alpha = jnp.exp2(m - m_new)
        l = alpha * l + jnp.sum(p, axis=-1, keepdims=True)
        acc = alpha * acc + jnp.dot(p.astype(BF16), v, preferred_element_type=F32)
        return m_new, l, acc

    def body(j, carries):
        return tuple(update(h, j, scores(h, j), carries[h]) for h in range(n_heads))

    init = (jnp.full((tb, 1), -jnp.inf, F32), jnp.zeros((tb, 1), F32), jnp.zeros((tb, hd), F32))
    carries = lax.fori_loop(0, qi, body, (init,) * n_heads)
    row = lax.broadcasted_iota(I32, (tb, tb), 0)
    col = lax.broadcasted_iota(I32, (tb, tb), 1)
    for h in range(n_heads):
        t = jnp.where(row >= col, scores(h, qi), -jnp.inf)
        _, l, acc = update(h, qi, t, carries[h])
        o_ref[:, cols[h]] = (acc / l).astype(o_ref.dtype)


def _fox_attn(qkv, cum_t, batch, seq, heads, hd, tb=512, hp=4):
    t = qkv.shape[0]
    d = heads * hd
    tb = _tile(seq, tb)
    nq = seq // tb
    hp = _tile(heads, hp)
    hg = heads // hp
    assert hd % LANES == 0
    return pl.pallas_call(
        functools.partial(_fox_attn_kernel, tb=tb, hd=hd),
        grid=(batch, hg, nq),
        in_specs=[pl.BlockSpec((tb, hp * hd), lambda b, h, i: (b * nq + i, h)),
                  pl.BlockSpec((seq, hp * hd), lambda b, h, i: (b, hg + h)),
                  pl.BlockSpec((seq, hp * hd), lambda b, h, i: (b, 2 * hg + h)),
                  pl.BlockSpec((None, hp, nq, 1, tb), lambda b, h, i: (b, h, 0, 0, 0))],
        out_specs=pl.BlockSpec((tb, hp * hd), lambda b, h, i: (b * nq + i, h)),
        out_shape=jax.ShapeDtypeStruct((t, d), BF16),
        compiler_params=_cparams("arbitrary", "arbitrary", "arbitrary"),
    )(qkv, qkv, qkv, cum_t)


def _oproj_ln_kernel(a_ref, w_ref, x_ref, g_ref, b_ref, o_ref, ob_ref, *, alpha):
    h = jnp.dot(a_ref[...], w_ref[...], preferred_element_type=F32)
    y = _layer_norm(alpha * x_ref[...] + h, g_ref[...], b_ref[...])
    o_ref[...] = y
    ob_ref[...] = y.astype(BF16)


def _oproj_ln(a_bf, w_bf, x2d, g, b, alpha, tm=512):
    t, d = x2d.shape
    kk = a_bf.shape[1]
    tm = _tile(t, tm)
    return pl.pallas_call(
        functools.partial(_oproj_ln_kernel, alpha=alpha),
        grid=(t // tm,),
        in_specs=[pl.BlockSpec((tm, kk), lambda i: (i, 0)),
                  pl.BlockSpec((kk, d), lambda i: (0, 0)),
                  pl.BlockSpec((tm, d), lambda i: (i, 0)),
                  pl.BlockSpec((1, d), lambda i: (0, 0)),
                  pl.BlockSpec((1, d), lambda i: (0, 0))],
        out_specs=[pl.BlockSpec((tm, d), lambda i: (i, 0)),
                   pl.BlockSpec((tm, d), lambda i: (i, 0))],
        out_shape=[jax.ShapeDtypeStruct((t, d), F32), jax.ShapeDtypeStruct((t, d), BF16)],
        compiler_params=_cparams("arbitrary"),
    )(a_bf, w_bf, x2d, g.reshape(1, d).astype(F32), b.reshape(1, d).astype(F32))


def _first_max(vals, idx, sentinel):
    m = jnp.max(vals, axis=0, keepdims=True)
    first = jnp.min(jnp.where(vals == m, idx, sentinel), axis=0, keepdims=True)
    return m, first


def _router_kernel(x_ref, rw_ref, bias_ref, wts_ref, lpos_ref, run_ref, *, tt):
    e = bias_ref.shape[0]
    tm = x_ref.shape[0]
    per = e // N_GROUPS
    logits = jnp.dot(x_ref[...], rw_ref[...], preferred_element_type=F32, precision=lax.Precision.HIGHEST)
    scores = jax.nn.sigmoid(logits.T[:e, :])
    choice = scores + bias_ref[...]
    neg = jnp.float32(-jnp.inf)

    ip = lax.broadcasted_iota(I32, (per, tm), 0)
    gs = []
    for g in range(N_GROUPS):
        c = choice[g * per:(g + 1) * per, :]
        m1, f1 = _first_max(c, ip, per)
        m2 = jnp.max(jnp.where(ip == f1, neg, c), axis=0, keepdims=True)
        gs.append(m1 + m2)
    gscore = jnp.concatenate(gs, axis=0)

    ig = lax.broadcasted_iota(I32, (N_GROUPS, tm), 0)
    sel = jnp.zeros((N_GROUPS, tm), F32)
    for _ in range(TOPK_GROUPS):
        _, fg = _first_max(gscore, ig, N_GROUPS)
        hit = ig == fg
        sel = jnp.where(hit, 1.0, sel)
        gscore = jnp.where(hit, neg, gscore)

    masked = jnp.concatenate(
        [jnp.where(sel[g:g + 1, :] > 0.0, choice[g * per:(g + 1) * per, :], neg) for g in range(N_GROUPS)],
        axis=0)
    ie = lax.broadcasted_iota(I32, (e, tm), 0)
    hits, ws = [], []
    for _ in range(TOP_K):
        _, fe = _first_max(masked, ie, e)
        hit = ie == fe
        hits.append(hit)
        ws.append(jnp.sum(jnp.where(hit, scores, 0.0), axis=0, keepdims=True))
        masked = jnp.where(hit, neg, masked)
    w = jnp.concatenate(ws, axis=0)
    wts_ref[...] = w / jnp.sum(w, axis=0, keepdims=True) * ROUTED_SCALE

    earlier_tok = (lax.broadcasted_iota(I32, (tt, tt), 0) < lax.broadcasted_iota(I32, (tt, tt), 1))
    earlier_tok = jnp.where(earlier_tok, 1.0, 0.0).astype(BF16)
    earlier_exp = (lax.broadcasted_iota(I32, (e, e), 1) < lax.broadcasted_iota(I32, (e, e), 0))
    earlier_exp = jnp.where(earlier_exp, 1.0, 0.0).astype(BF16)
    lane = lax.broadcasted_iota(I32, (e, LANES), 1)
    run_out = jnp.zeros((e, LANES), I32)
    lpos = []
    for s in range(tm // tt):
        sl = slice(s * tt, (s + 1) * tt)
        chosen = jnp.zeros((e, tt), F32)
        for hit in hits:
            chosen = jnp.where(hit[:, sl], 1.0, chosen)
        rank = jnp.dot(chosen.astype(BF16), earlier_tok, preferred_element_type=F32)
        cnt = (rank[:, tt - 1:tt] + chosen[:, tt - 1:tt]).astype(I32)
        shift = SUBLANES.bit_length() - 1
        run = jnp.left_shift(jnp.right_shift(cnt + (SUBLANES - 1), shift), shift)
        run_b = jnp.broadcast_to(run.astype(F32), (e, LANES)).astype(BF16)
        off = jnp.dot(earlier_exp, run_b, preferred_element_type=F32)[:, 0:1]
        pos = (off + rank).astype(I32)
        lpos.append(jnp.concatenate(
            [jnp.sum(jnp.where(hit[:, sl], pos, 0), axis=0, keepdims=True) for hit in hits], axis=0))
        run_out = jnp.where(lane == s, run, run_out)
    lpos_ref[...] = jnp.concatenate(lpos, axis=1)
    run_ref[...] = run_out


def _router(x2d, router_w, router_bias, tt, tm=512):
    t, d = x2d.shape
    e = router_w.shape[1]
    assert e % N_GROUPS == 0 and (e // N_GROUPS) % SUBLANES == 0
    tm = max(_tile(t, tm), tt)
    assert tm % tt == 0 and tm // tt <= LANES and tt <= 256
    ep = -(-e // LANES) * LANES
    rw_pad = jnp.zeros((d, ep), F32).at[:, :e].set(router_w.astype(F32))
    wts_t, lpos_t, run_tab = pl.pallas_call(
        functools.partial(_router_kernel, tt=tt),
        grid=(t // tm,),
        in_specs=[pl.BlockSpec((tm, d), lambda i: (i, 0)),
                  pl.BlockSpec((d, ep), lambda i: (0, 0)),
                  pl.BlockSpec((e, 1), lambda i: (0, 0))],
        out_specs=[pl.BlockSpec((TOP_K, tm), lambda i: (0, i)),
                   pl.BlockSpec((TOP_K, tm), lambda i: (0, i)),
                   pl.BlockSpec((None, e, LANES), lambda i: (i, 0, 0))],
        out_shape=[jax.ShapeDtypeStruct((TOP_K, t), F32), jax.ShapeDtypeStruct((TOP_K, t), I32),
                   jax.ShapeDtypeStruct((t // tm, e, LANES), I32)],
        compiler_params=_cparams("arbitrary"),
    )(x2d, rw_pad, router_bias.reshape(e, 1).astype(F32))
    run = run_tab[:, :, :tm // tt].transpose(0, 2, 1).reshape(t // tt, e)
    return wts_t, lpos_t, run


def _seg_copy(src, dst, src_off, dst_off, length, max_len, sem, wait=False):
    done = 0
    b = max_len
    while b >= SUBLANES:
        piece = length & b

        @pl.when(piece != 0)
        def _(b=b, done=done):
            cp = pltpu.make_async_copy(src.at[pl.ds(pl.multiple_of(src_off + done, SUBLANES), b)],
                                       dst.at[pl.ds(pl.multiple_of(dst_off + done, SUBLANES), b)], sem)
            if wait:
                cp.wait()
            else:
                cp.start()
        done = done + piece
        b //= 2


def _pow2_floor(n):
    return 1 << (n.bit_length() - 1)


def _dispatch_kernel(tab_ref, pad_ref, lpt_ref, x_ref, o_hbm, xs, zbuf, sem, *, n_exp, rc):
    i = pl.program_id(0)
    n = pl.num_programs(0)
    slot = i % 2
    tt = x_ref.shape[0]
    r_tot, half = xs.shape[1], xs.shape[2]

    def wait_slot(s, step):
        _seg_copy(xs.at[s], o_hbm, 0, 0, pad_ref[2 * n_exp + step], _pow2_floor(r_tot), sem.at[s], wait=True)

    @pl.when(i >= 2)
    def _():
        wait_slot(slot, i - 2)

    xlo, xhi = x_ref[:, :half], x_ref[:, half:]
    lpt = lpt_ref[...]
    for c in range(r_tot // rc):
        rows = lax.broadcasted_iota(I32, (rc, tt), 0) + c * rc
        hit = rows == lpt[0:1, :]
        for k in range(1, TOP_K):
            hit = hit | (rows == lpt[k:k + 1, :])
        g = jnp.where(hit, 1.0, 0.0).astype(BF16)
        lo = jnp.dot(g, xlo, preferred_element_type=F32)
        hi = jnp.dot(g, xhi, preferred_element_type=F32)
        xs[slot, c * rc:(c + 1) * rc, :] = _pack_halves(lo, hi)

    for e in range(n_exp):
        _seg_copy(xs.at[slot], o_hbm, tab_ref[2 * n_exp + e], tab_ref[e], tab_ref[n_exp + e], tt, sem.at[slot])

    @pl.when(i == 0)
    def _():
        zbuf[...] = jnp.zeros_like(zbuf)
        for w in (False, True):
            for e in range(n_exp):
                _seg_copy(zbuf, o_hbm, 0, pad_ref[e], pad_ref[n_exp + e], zbuf.shape[0] // 2, sem.at[2], wait=w)

    @pl.when(i == n - 1)
    def _():
        @pl.when(n >= 2)
        def _():
            wait_slot(1 - slot, i - 1)
        wait_slot(slot, i)


def _tile_rows(tt, n_exp, rc=512):
    r = tt * TOP_K + (SUBLANES - 1) * n_exp
    rc = min(rc, tt * TOP_K)
    return -(-r // rc) * rc, rc


def _dispatch(xbf, tab, pad_tab, lpos_t, m_pad, n_exp, tt, bm):
    t, d = xbf.shape
    assert tt & (tt - 1) == 0 and bm & (bm - 1) == 0 and d % 2 == 0 and tt % SUBLANES == 0
    r_tot, rc = _tile_rows(tt, n_exp)
    return pl.pallas_call(
        functools.partial(_dispatch_kernel, n_exp=n_exp, rc=rc),
        grid=(t // tt,),
        in_specs=[pl.BlockSpec((4 * n_exp,), lambda i: (i,), memory_space=pltpu.SMEM),
                  pl.BlockSpec(memory_space=pltpu.SMEM),
                  pl.BlockSpec((TOP_K, tt), lambda i: (0, i)),
                  pl.BlockSpec((tt, d), lambda i: (i, 0))],
        out_specs=pl.BlockSpec(memory_space=pl.ANY),
        out_shape=jax.ShapeDtypeStruct((m_pad, d // 2), U32),
        scratch_shapes=[pltpu.VMEM((2, r_tot, d // 2), U32), pltpu.VMEM((bm, d // 2), U32),
                        pltpu.SemaphoreType.DMA((3,))],
        compiler_params=_cparams("arbitrary"),
    )(tab, pad_tab, lpos_t, xbf)


def _expert_kernel(be_ref, nv_ref, br_ref, xs_ref, wgu_ref, wd_ref, y_ref, wgu_bf, wd_bf, *, cast_rows):
    i = pl.program_id(0)

    @pl.when(i < nv_ref[0])
    def _():
        prev = be_ref[jnp.maximum(i - 1, 0)]

        @pl.when((i == 0) | (be_ref[i] != prev))
        def _():
            def cast_gu(c, carry):
                r = pl.ds(pl.multiple_of(c * cast_rows, cast_rows), cast_rows)
                wgu_bf[r, :] = wgu_ref[r, :].astype(BF16)
                return carry
            lax.fori_loop(0, wgu_ref.shape[0] // cast_rows, cast_gu, 0)
            wd_bf[...] = wd_ref[...].astype(BF16)

        half = xs_ref.shape[1]
        eh = wd_ref.shape[0]
        bm = xs_ref.shape[0]

        def mlp(rows):
            lo, hi = _unpack_rows(xs_ref[:rows, :])
            gu = (jnp.dot(lo, wgu_bf[:half, :], preferred_element_type=F32)
                  + jnp.dot(hi, wgu_bf[half:, :], preferred_element_type=F32))
            a = (_silu(gu[:, :eh]) * gu[:, eh:]).astype(BF16)
            y_ref[:rows, :] = _pack_rows(jnp.dot(a, wd_bf[...], preferred_element_type=F32))

        @pl.when(br_ref[i] > bm // 2)
        def _():
            mlp(bm)

        @pl.when(br_ref[i] <= bm // 2)
        def _():
            mlp(bm // 2)
            y_ref[bm // 2:, :] = jnp.zeros((bm - bm // 2, half), U32)


def _experts(xs, block_e, nvalid, block_rows, w_gu, w_down, layer, bm):
    m_pad, dh = xs.shape
    _, e, d, eh2 = w_gu.shape
    eh = w_down.shape[2]
    nb = m_pad // bm
    cast_rows = _tile(d, 256)

    def blk(i, be, nv, br):
        return jnp.minimum(i, nv[0] - 1)

    grid_spec = pltpu.PrefetchScalarGridSpec(
        num_scalar_prefetch=3,
        grid=(nb,),
        in_specs=[pl.BlockSpec((bm, dh), lambda i, be, nv, br: (blk(i, be, nv, br), 0)),
                  pl.BlockSpec((None, None, d, eh2), lambda i, be, nv, br: (layer, be[blk(i, be, nv, br)], 0, 0)),
                  pl.BlockSpec((None, None, eh, d), lambda i, be, nv, br: (layer, be[blk(i, be, nv, br)], 0, 0))],
        out_specs=pl.BlockSpec((bm, dh), lambda i, be, nv, br: (blk(i, be, nv, br), 0)),
        scratch_shapes=[pltpu.VMEM((d, eh2), BF16), pltpu.VMEM((eh, d), BF16)],
    )
    return pl.pallas_call(
        functools.partial(_expert_kernel, cast_rows=cast_rows),
        grid_spec=grid_spec,
        out_shape=jax.ShapeDtypeStruct((m_pad, dh), U32),
        compiler_params=_cparams("arbitrary"),
    )(block_e, nvalid, block_rows, xs, w_gu, w_down)


def _combine_kernel(tabc_ref, tabn_ref, lp_ref, w_ref, x_ref, sgu_ref, sd_ref, g_ref, b_ref, y_hbm,
                    o_ref, ybuf, sem, *, alpha, n_exp):
    i = pl.program_id(0)
    n = pl.num_programs(0)
    tt = x_ref.shape[0]
    r_tot = ybuf.shape[1]
    slot = i % 2

    def issue(tab_ref, s):
        for e in range(n_exp):
            _seg_copy(y_hbm, ybuf.at[s], tab_ref[e], tab_ref[2 * n_exp + e], tab_ref[n_exp + e], tt, sem.at[s])

    @pl.when(i == 0)
    def _():
        ybuf[...] = jnp.zeros_like(ybuf)
        issue(tabc_ref, 0)

    @pl.when(i + 1 < n)
    def _():
        issue(tabn_ref, 1 - slot)

    _seg_copy(y_hbm, ybuf.at[slot], 0, 0, tabc_ref[3 * n_exp], _pow2_floor(r_tot), sem.at[slot], wait=True)

    lp = lp_ref[...]
    w = w_ref[...]
    cols = lax.broadcasted_iota(I32, (tt, r_tot), 1)
    p = jnp.zeros((tt, r_tot), F32)
    for k in range(TOP_K):
        p = jnp.where(cols == lp[:, k:k + 1], w[:, k:k + 1], p)
    pb = p.astype(BF16)
    ylo, yhi = _unpack_rows(ybuf[slot])
    routed = jnp.concatenate([jnp.dot(pb, ylo, preferred_element_type=F32),
                              jnp.dot(pb, yhi, preferred_element_type=F32)], axis=1)

    x = x_ref[...]
    sh = sd_ref.shape[0]
    sg = jnp.dot(x.astype(BF16), sgu_ref[...], preferred_element_type=F32)
    a = (_silu(sg[:, :sh]) * sg[:, sh:]).astype(BF16)
    shared = jnp.dot(a, sd_ref[...], preferred_element_type=F32)
    o_ref[...] = _layer_norm(alpha * x + (routed + shared), g_ref[...], b_ref[...])


def _combine(x2d, wts, lpos, tab, y_sorted, s_gu_bf, s_down_bf, g, b, alpha, n_exp, tt):
    t, d = x2d.shape
    n = t // tt
    sh = s_down_bf.shape[0]
    return pl.pallas_call(
        functools.partial(_combine_kernel, alpha=alpha, n_exp=n_exp),
        grid=(n,),
        in_specs=[pl.BlockSpec((4 * n_exp,), lambda i: (i,), memory_space=pltpu.SMEM),
                  pl.BlockSpec((4 * n_exp,), lambda i: (jnp.minimum(i + 1, n - 1),), memory_space=pltpu.SMEM),
                  pl.BlockSpec((tt, TOP_K), lambda i: (i, 0)),
                  pl.BlockSpec((tt, TOP_K), lambda i: (i, 0)),
                  pl.BlockSpec((tt, d), lambda i: (i, 0)),
                  pl.BlockSpec((d, 2 * sh), lambda i: (0, 0)),
                  pl.BlockSpec((sh, d), lambda i: (0, 0)),
                  pl.BlockSpec((1, d), lambda i: (0, 0)),
                  pl.BlockSpec((1, d), lambda i: (0, 0)),
                  pl.BlockSpec(memory_space=pl.ANY)],
        out_specs=pl.BlockSpec((tt, d), lambda i: (i, 0)),
        out_shape=jax.ShapeDtypeStruct((t, d), F32),
        scratch_shapes=[pltpu.VMEM((2, _tile_rows(tt, n_exp)[0], d // 2), U32), pltpu.SemaphoreType.DMA((2,))],
        compiler_params=_cparams("arbitrary"),
    )(tab, tab, lpos, wts, x2d, s_gu_bf, s_down_bf,
      g.reshape(1, d).astype(F32), b.reshape(1, d).astype(F32), y_sorted)


def _moe_layer(x2d, xbf, router_w, router_bias, w_gu, w_down, layer, s_gu, s_down, g, b, alpha, bm=512, tt=256):
    t, d = x2d.shape
    e = router_w.shape[1]
    tt = _tile(t, tt)
    nt = t // tt
    wts_t, lpos_t, run = _router(x2d, router_w, router_bias, tt)

    before = jnp.cumsum(run, axis=0) - run
    off = jnp.cumsum(run, axis=1) - run
    counts = jnp.sum(run, axis=0)
    padded = ((counts + bm - 1) // bm) * bm
    ends = jnp.cumsum(padded)
    starts = ends - padded
    totals = jnp.sum(run, axis=1)
    spare = jnp.zeros_like(run).at[:, 0].set(totals)
    tab = jnp.concatenate([starts[None, :] + before, run, off, spare], axis=1).reshape(-1).astype(I32)
    pad_tab = jnp.concatenate([starts + counts, padded - counts, totals]).astype(I32)
    m_pad = -(-(t * TOP_K + (SUBLANES - 1) * nt * e) // bm) * bm + e * bm
    nb = m_pad // bm
    block_start = jnp.arange(nb, dtype=I32) * bm
    block_e = jnp.minimum(jnp.sum((ends[None, :] <= block_start[:, None]).astype(I32), axis=1), e - 1).astype(I32)
    nvalid = (ends[-1:] // bm).astype(I32)
    block_rows = jnp.sum(jnp.clip(jnp.minimum((starts + counts)[None, :], block_start[:, None] + bm)
                                  - jnp.maximum(starts[None, :], block_start[:, None]), 0, bm), axis=1).astype(I32)

    xs = _dispatch(xbf, tab, pad_tab, lpos_t, m_pad, e, tt, bm)
    y_sorted = _experts(xs, block_e, nvalid, block_rows, w_gu, w_down, layer, bm)
    return _combine(x2d, wts_t.T, lpos_t.T, tab, y_sorted, s_gu.astype(BF16), s_down.astype(BF16), g, b, alpha, e, tt)


def _proj_rope_kernel(x_ref, w_ref, cos_ref, sin_ref, o_ref):
    y = jnp.dot(x_ref[...].astype(BF16), w_ref[...], preferred_element_type=F32)
    tm, n = y.shape
    tw = cos_ref.shape[1]
    lane = lax.broadcasted_iota(I32, (tm, LANES), 1)
    first_half = (lane % HD_B) < (HD_B // 2)
    for c in range(n // LANES):
        yc = y[:, c * LANES:(c + 1) * LANES]
        off = (c * LANES) % tw
        rot = jnp.where(first_half, pltpu.roll(yc, LANES - HD_B // 2, axis=1), pltpu.roll(yc, HD_B // 2, axis=1))
        yc = yc * cos_ref[:, off:off + LANES] + rot * sin_ref[:, off:off + LANES]
        o_ref[:, c * LANES:(c + 1) * LANES] = yc.astype(o_ref.dtype)


def _proj_rope(x2d, w_bf, cos, sin, seq, tm=512, tn=1024):
    t, d = x2d.shape
    n = w_bf.shape[1]
    tw = cos.shape[1]
    tm, tn = _tile(seq, tm), _tile(n, tn)
    assert tn % tw == 0 and tw % LANES == 0
    ns = seq // tm
    return pl.pallas_call(
        _proj_rope_kernel,
        grid=(n // tn, t // tm),
        in_specs=[pl.BlockSpec((tm, d), lambda j, i: (i, 0)),
                  pl.BlockSpec((d, tn), lambda j, i: (0, j)),
                  pl.BlockSpec((tm, tw), lambda j, i: (i % ns, 0)),
                  pl.BlockSpec((tm, tw), lambda j, i: (i % ns, 0))],
        out_specs=pl.BlockSpec((tm, tn), lambda j, i: (i, j)),
        out_shape=jax.ShapeDtypeStruct((t, n), BF16),
        compiler_params=_cparams("arbitrary", "arbitrary"),
    )(x2d, w_bf, cos, sin)


def _rope_tables(seq, scale, heads, plain_cols=0):
    half = HD_B // 2
    inv = 1.0 / (ROPE_THETA ** (jnp.arange(half, dtype=F32) / half))
    ang = jnp.arange(seq, dtype=F32)[:, None] * inv[None, :]
    cos, sin = jnp.cos(ang), jnp.sin(ang)
    cos_t = jnp.tile(jnp.concatenate([cos, cos], axis=1), (1, heads)) * scale
    sin_t = jnp.tile(jnp.concatenate([-sin, sin], axis=1), (1, heads)) * scale
    cos_t = jnp.concatenate([cos_t, jnp.ones((seq, plain_cols), F32)], axis=1)
    sin_t = jnp.concatenate([sin_t, jnp.zeros((seq, plain_cols), F32)], axis=1)
    return cos_t, sin_t


def _swa_kernel(q_ref, kc_ref, kp_ref, vc_ref, vp_ref, sink_ref, o_ref, *, groups):
    qi = pl.program_id(2)
    tq = q_ref.shape[0]
    w = WINDOW
    row = lax.broadcasted_iota(I32, (w, 2 * w), 0)
    col = lax.broadcasted_iota(I32, (w, 2 * w), 1)
    band = (col > row) & (col <= row + w)
    neg = jnp.float32(-jnp.inf)
    sink = jnp.stack([sink_ref[g:g + 1, 0:1] for g in range(groups)])
    for j in range(tq // w):
        if j == 0:
            keys = [jnp.concatenate([kp_ref[s], kc_ref[s, :w, :]], axis=0) for s in range(2)]
            vals = [jnp.concatenate([vp_ref[s], vc_ref[s, :w, :]], axis=0) for s in range(2)]
            mask = band & ((col >= w) | (qi > 0))
        else:
            keys = [kc_ref[s, (j - 1) * w:(j + 1) * w, :] for s in range(2)]
            vals = [vc_ref[s, (j - 1) * w:(j + 1) * w, :] for s in range(2)]
            mask = band
        scs = []
        for p in range(groups // 2):
            qs = q_ref[j * w:(j + 1) * w, p * LANES:(p + 1) * LANES]
            for s in range(2):
                scs.append(lax.dot_general(qs, keys[s], (((1,), (1,)), ((), ())), preferred_element_type=F32))
        sc = jnp.where(mask[None], jnp.stack(scs), neg)
        m = jnp.maximum(jnp.max(jnp.maximum(sc[..., :w], sc[..., w:]), axis=-1, keepdims=True), sink)
        pr = jnp.exp(sc - m)
        den = jnp.sum(pr[..., :w] + pr[..., w:], axis=-1, keepdims=True) + jnp.exp(sink - m)
        pr = (pr * (1.0 / den)).astype(BF16)
        for p in range(groups // 2):
            out = (jnp.dot(pr[2 * p], vals[0], preferred_element_type=F32)
                   + jnp.dot(pr[2 * p + 1], vals[1], preferred_element_type=F32))
            o_ref[j * w:(j + 1) * w, p * LANES:(p + 1) * LANES] = out.astype(o_ref.dtype)


def _swa_attn(q, kk, vv, sink_tab, batch, seq, kvh, groups, tq=512):
    t = q.shape[0]
    tq = _tile(seq, tq)
    assert tq % WINDOW == 0 and groups % 2 == 0 and 2 * HD_B == LANES
    nq = seq // tq
    per = tq // WINDOW
    gw = groups * HD_B
    cur = pl.BlockSpec((None, None, 2, tq, LANES), lambda b, h, i: (b, h, 0, i, 0))
    prev = pl.BlockSpec((None, None, 2, WINDOW, LANES), lambda b, h, i: (b, h, 0, jnp.maximum(i * per - 1, 0), 0))
    return pl.pallas_call(
        functools.partial(_swa_kernel, groups=groups),
        grid=(batch, kvh, nq),
        in_specs=[pl.BlockSpec((tq, gw), lambda b, h, i: (b * nq + i, h)),
                  cur, prev, cur, prev,
                  pl.BlockSpec((None, groups, LANES), lambda b, h, i: (h, 0, 0))],
        out_specs=pl.BlockSpec((tq, gw), lambda b, h, i: (b * nq + i, h)),
        out_shape=jax.ShapeDtypeStruct((t, kvh * gw), BF16),
        compiler_params=_cparams("arbitrary", "arbitrary", "arbitrary"),
    )(q, kk, kk, vv, vv, sink_tab)


def _lane_pairs(a, batch, seq, kvh):
    a4 = a.reshape(batch, seq, kvh, HD_B).transpose(0, 2, 1, 3)
    z = jnp.zeros_like(a4)
    return jnp.stack([jnp.concatenate([a4, z], axis=-1), jnp.concatenate([z, a4], axis=-1)], axis=2)


def kernel(x, a_w_in, a_b_f, a_w_o, kv_w, b_w_q, b_sinks, b_w_o, router_w, router_bias, moe_w_gu, moe_w_down,
           shared_w_gu, shared_w_down, ln1_g, ln1_b, ln2_g, ln2_b):
    batch, seq, d = x.shape
    n_a, n_b = a_w_in.shape[0], b_w_q.shape[0]
    depth = n_a + n_b
    alpha = float((2 * depth) ** 0.25)
    heads_a = a_b_f.shape[1]
    hd_a = d // heads_a
    heads_b = b_sinks.shape[1]
    kvh = kv_w.shape[1] // (2 * HD_B)
    groups = heads_b // kvh
    t = batch * seq

    x2d = x.reshape(t, d).astype(F32)
    kk = vv = None
    cos_q, sin_q = _rope_tables(seq, 1.0 / math.sqrt(HD_B), LANES // HD_B)
    cos_k, sin_k = _rope_tables(seq, 1.0, kvh, plain_cols=kvh * HD_B)
    for i in range(depth):
        if i < n_a:
            qkv = _proj(x2d, a_w_in, i, 3 * d, scaled_cols=d, scale=LOG2E / math.sqrt(hd_a))
            cum = _fox_cum(x2d, a_w_in, i, 3 * d, a_b_f[i], batch, seq)
            tb = _tile(seq, 512)
            cum_t = cum[:, :heads_a].reshape(batch, seq, heads_a).transpose(0, 2, 1)
            cum_t = cum_t.reshape(batch, heads_a, seq // tb, 1, tb)
            attn = _fox_attn(qkv, cum_t, batch, seq, heads_a, hd_a, tb=tb)
            w_o = a_w_o[i]
        else:
            j = i - n_a
            q = _proj_rope(x2d, b_w_q[j].astype(BF16), cos_q, sin_q, seq)
            sink_tab = jnp.broadcast_to(b_sinks[j].astype(F32).reshape(kvh, groups, 1), (kvh, groups, LANES))
            attn = _swa_attn(q, kk, vv, sink_tab, batch, seq, kvh, groups)
            w_o = b_w_o[j]
        x2d, xbf = _oproj_ln(attn, w_o.astype(BF16), x2d, ln1_g[i], ln1_b[i], alpha)
        x2d = _moe_layer(x2d, xbf, router_w[i], router_bias[i], moe_w_gu, moe_w_down, i,
                         shared_w_gu[i], shared_w_down[i], ln2_g[i], ln2_b[i], alpha)
        if i == n_a - 1:
            kv = _proj_rope(x2d, kv_w.astype(BF16), cos_k, sin_k, seq, tn=kv_w.shape[1])
            kk = _lane_pairs(kv[:, :kvh * HD_B], batch, seq, kvh)
            vv = _lane_pairs(kv[:, kvh * HD_B:], batch, seq, kvh)
    return x2d.reshape(batch, seq, d).astype(x.dtype)
```

```python
import functools
import math

import jax
import jax.numpy as jnp
from jax import lax
from jax.experimental import pallas as pl
from jax.experimental.pallas import tpu as pltpu

F32 = jnp.float32
BF16 = jnp.bfloat16
U32 = jnp.uint32
I32 = jnp.int32

LANES = 128
SUBLANES = 8
HD_B = 64
WINDOW = 128
ROPE_THETA = 10000.0
TOP_K = 8
N_GROUPS = 8
TOPK_GROUPS = 4
ROUTED_SCALE = 2.5
LN_EPS = 1e-5
VMEM_LIMIT_BYTES = 56 * 1024 * 1024
HI_MASK = 0xFFFF0000
LOG2E = math.log2(math.e)


def _cparams(*sem):
    return pltpu.CompilerParams(dimension_semantics=sem, vmem_limit_bytes=VMEM_LIMIT_BYTES)


def _tile(dim, pref):
    t = min(dim, pref)
    assert dim % t == 0, (dim, pref)
    return t


def _silu(x):
    return x * jax.nn.sigmoid(x)


def _layer_norm(z, g, b):
    mu = jnp.mean(z, axis=-1, keepdims=True)
    zc = z - mu
    var = jnp.mean(zc * zc, axis=-1, keepdims=True)
    return zc * lax.rsqrt(var + LN_EPS) * g + b


def _pack_halves(lo, hi):
    lo = lax.bitcast_convert_type(lo, U32)
    hi = lax.bitcast_convert_type(hi, U32)
    return (hi & jnp.uint32(HI_MASK)) | (lo >> jnp.uint32(16))


def _pack_rows(y):
    half = y.shape[1] // 2
    yb = y.astype(BF16).astype(F32)
    return _pack_halves(yb[:, :half], yb[:, half:])


def _unpack_rows(pk):
    lo = lax.bitcast_convert_type(pk << jnp.uint32(16), F32).astype(BF16)
    hi = lax.bitcast_convert_type(pk & jnp.uint32(HI_MASK), F32).astype(BF16)
    return lo, hi


def _proj_kernel(x_ref, w_ref, o_ref, w_bf, *, scaled_tiles, scale, cast_rows):
    j = pl.program_id(0)

    @pl.when(pl.program_id(1) == 0)
    def _():
        def cast(c, carry):
            r = pl.ds(pl.multiple_of(c * cast_rows, cast_rows), cast_rows)
            w_bf[r, :] = w_ref[r, :].astype(BF16)
            return carry
        lax.fori_loop(0, w_ref.shape[0] // cast_rows, cast, 0)

    y = jnp.dot(x_ref[...].astype(BF16), w_bf[...], preferred_element_type=F32)
    s = jnp.where(j < scaled_tiles, jnp.float32(scale), jnp.float32(1.0))
    o_ref[...] = (y * s).astype(o_ref.dtype)


def _proj(x2d, w, layer, n, *, scaled_cols, scale, tm=512, tn=1024):
    t, d = x2d.shape
    tm, tn = _tile(t, tm), _tile(scaled_cols, tn)
    assert n % tn == 0
    return pl.pallas_call(
        functools.partial(_proj_kernel, scaled_tiles=scaled_cols // tn, scale=scale, cast_rows=_tile(d, 256)),
        grid=(n // tn, t // tm),
        in_specs=[pl.BlockSpec((tm, d), lambda j, i: (i, 0)),
                  pl.BlockSpec((None, d, tn), lambda j, i: (layer, 0, j))],
        out_specs=pl.BlockSpec((tm, tn), lambda j, i: (i, j)),
        out_shape=jax.ShapeDtypeStruct((t, n), BF16),
        scratch_shapes=[pltpu.VMEM((d, tn), BF16)],
        compiler_params=_cparams("arbitrary", "arbitrary"),
    )(x2d, w)


def _fox_cum_kernel(x_ref, w_ref, b_ref, o_ref, carry_ref, *, heads):
    @pl.when(pl.program_id(1) == 0)
    def _():
        carry_ref[...] = jnp.zeros_like(carry_ref)

    ts = x_ref.shape[0]
    lane = lax.broadcasted_iota(I32, w_ref.shape, 1)
    w = jnp.where(lane < heads, w_ref[...], 0.0).astype(BF16)
    z = jnp.dot(x_ref[...].astype(BF16), w, preferred_element_type=F32) + b_ref[...]
    logf = -(jnp.maximum(-z, 0.0) + jnp.log1p(jnp.exp(-jnp.abs(z))))
    row = lax.broadcasted_iota(I32, (ts, ts), 0)
    col = lax.broadcasted_iota(I32, (ts, ts), 1)
    tril = (row >= col).astype(F32)
    cum = jnp.dot(tril, logf, preferred_element_type=F32, precision=lax.Precision.HIGHEST) + carry_ref[...]
    o_ref[...] = cum * LOG2E
    carry_ref[...] = cum[ts - 1:ts, :]


def _fox_cum(x2d, w_in, layer, col0, b_f, batch, seq, ts=512):
    t, d = x2d.shape
    h = w_in.shape[2] - col0
    assert h <= LANES and col0 % LANES == 0
    ts = _tile(seq, ts)
    b_pad = jnp.zeros((1, LANES), F32).at[0, :h].set(b_f.astype(F32))
    ns = seq // ts
    return pl.pallas_call(
        functools.partial(_fox_cum_kernel, heads=h),
        grid=(batch, ns),
        in_specs=[pl.BlockSpec((ts, d), lambda b, s: (b * ns + s, 0)),
                  pl.BlockSpec((None, d, LANES), lambda b, s: (layer, 0, col0 // LANES)),
                  pl.BlockSpec((1, LANES), lambda b, s: (0, 0))],
        out_specs=pl.BlockSpec((ts, LANES), lambda b, s: (b * ns + s, 0)),
        out_shape=jax.ShapeDtypeStruct((t, LANES), F32),
        scratch_shapes=[pltpu.VMEM((1, LANES), F32)],
        compiler_params=_cparams("arbitrary", "arbitrary"),
    )(x2d, w_in, b_pad)


def _fox_attn_kernel(q_ref, k_ref, v_ref, c_ref, o_ref, *, tb, hd):
    qi = pl.program_id(2)
    n_heads = q_ref.shape[1] // hd
    cols = [slice(h * hd, (h + 1) * hd) for h in range(n_heads)]
    qs = [q_ref[:, c] for c in cols]

    def scores(h, j):
        k = k_ref[pl.ds(pl.multiple_of(j * tb, tb), tb), cols[h]]
        s = lax.dot_general(qs[h], k, (((1,), (1,)), ((), ())), preferred_element_type=F32)
        return s - c_ref[h, j]

    def update(h, j, t, carry):
        m, l, acc = carry
        v = v_ref[pl.ds(pl.multiple_of(j * tb, tb), tb), cols[h]]
        m_new = jnp.maximum(m, jnp.max(t, axis=-1, keepdims=True))
        p = jnp.exp2(t - m_new)
        alpha = jnp.exp2(m - m_new)
        l = alpha * l + jnp.sum(p, axis=-1, keepdims=True)
        acc = alpha * acc + jnp.dot(p.astype(BF16), v, preferred_element_type=F32)
        return m_new, l, acc

    def body(j, carries):
        return tuple(update(h, j, scores(h, j), carries[h]) for h in range(n_heads))

    init = (jnp.full((tb, 1), -jnp.inf, F32), jnp.zeros((tb, 1), F32), jnp.zeros((tb, hd), F32))
    carries = lax.fori_loop(0, qi, body, (init,) * n_heads)
    row = lax.broadcasted_iota(I32, (tb, tb), 0)
    col = lax.broadcasted_iota(I32, (tb, tb), 1)
    for h in range(n_heads):
        t = jnp.where(row >= col, scores(h, qi), -jnp.inf)
        _, l, acc = update(h, qi, t, carries[h])
        o_ref[:, cols[h]] = (acc / l).astype(o_ref.dtype)


def _fox_attn(qkv, cum_t, batch, seq, heads, hd, tb=512, hp=4):
    t = qkv.shape[0]
    d = heads * hd
    tb = _tile(seq, tb)
    nq = seq // tb
    hp = _tile(heads, hp)
    hg = heads // hp
    assert hd % LANES == 0
    return pl.pallas_call(
        functools.partial(_fox_attn_kernel, tb=tb, hd=hd),
        grid=(batch, hg, nq),
        in_specs=[pl.BlockSpec((tb, hp * hd), lambda b, h, i: (b * nq + i, h)),
                  pl.BlockSpec((seq, hp * hd), lambda b, h, i: (b, hg + h)),
                  pl.BlockSpec((seq, hp * hd), lambda b, h, i: (b, 2 * hg + h)),
                  pl.BlockSpec((None, hp, nq, 1, tb), lambda b, h, i: (b, h, 0, 0, 0))],
        out_specs=pl.BlockSpec((tb, hp * hd), lambda b, h, i: (b * nq + i, h)),
        out_shape=jax.ShapeDtypeStruct((t, d), BF16),
        compiler_params=_cparams("arbitrary", "arbitrary", "arbitrary"),
    )(qkv, qkv, qkv, cum_t)


def _oproj_ln_kernel(a_ref, w_ref, x_ref, g_ref, b_ref, o_ref, ob_ref, *, alpha):
    h = jnp.dot(a_ref[...], w_ref[...], preferred_element_type=F32)
    y = _layer_norm(alpha * x_ref[...] + h, g_ref[...], b_ref[...])
    o_ref[...] = y
    ob_ref[...] = y.astype(BF16)


def _oproj_ln(a_bf, w_bf, x2d, g, b, alpha, tm=512):
    t, d = x2d.shape
    kk = a_bf.shape[1]
    tm = _tile(t, tm)
    return pl.pallas_call(
        functools.partial(_oproj_ln_kernel, alpha=alpha),
        grid=(t // tm,),
        in_specs=[pl.BlockSpec((tm, kk), lambda i: (i, 0)),
                  pl.BlockSpec((kk, d), lambda i: (0, 0)),
                  pl.BlockSpec((tm, d), lambda i: (i, 0)),
                  pl.BlockSpec((1, d), lambda i: (0, 0)),
                  pl.BlockSpec((1, d), lambda i: (0, 0))],
        out_specs=[pl.BlockSpec((tm, d), lambda i: (i, 0)),
                   pl.BlockSpec((tm, d), lambda i: (i, 0))],
        out_shape=[jax.ShapeDtypeStruct((t, d), F32), jax.ShapeDtypeStruct((t, d), BF16)],
        compiler_params=_cparams("arbitrary"),
    )(a_bf, w_bf, x2d, g.reshape(1, d).astype(F32), b.reshape(1, d).astype(F32))


def _first_max(vals, idx, sentinel):
    m = jnp.max(vals, axis=0, keepdims=True)
    first = jnp.min(jnp.where(vals == m, idx, sentinel), axis=0, keepdims=True)
    return m, first


def _router_kernel(x_ref, rw_ref, bias_ref, wts_ref, lpos_ref, run_ref, *, tt):
    e = bias_ref.shape[0]
    tm = x_ref.shape[0]
    per = e // N_GROUPS
    logits = jnp.dot(x_ref[...], rw_ref[...], preferred_element_type=F32, precision=lax.Precision.HIGHEST)
    scores = jax.nn.sigmoid(logits.T[:e, :])
    choice = scores + bias_ref[...]
    neg = jnp.float32(-jnp.inf)

    ip = lax.broadcasted_iota(I32, (per, tm), 0)
    gs = []
    for g in range(N_GROUPS):
        c = choice[g * per:(g + 1) * per, :]
        m1, f1 = _first_max(c, ip, per)
        m2 = jnp.max(jnp.where(ip == f1, neg, c), axis=0, keepdims=True)
        gs.append(m1 + m2)
    gscore = jnp.concatenate(gs, axis=0)

    ig = lax.broadcasted_iota(I32, (N_GROUPS, tm), 0)
    sel = jnp.zeros((N_GROUPS, tm), F32)
    for _ in range(TOPK_GROUPS):
        _, fg = _first_max(gscore, ig, N_GROUPS)
        hit = ig == fg
        sel = jnp.where(hit, 1.0, sel)
        gscore = jnp.where(hit, neg, gscore)

    masked = jnp.concatenate(
        [jnp.where(sel[g:g + 1, :] > 0.0, choice[g * per:(g + 1) * per, :], neg) for g in range(N_GROUPS)],
        axis=0)
    ie = lax.broadcasted_iota(I32, (e, tm), 0)
    hits, ws = [], []
    for _ in range(TOP_K):
        _, fe = _first_max(masked, ie, e)
        hit = ie == fe
        hits.append(hit)
        ws.append(jnp.sum(jnp.where(hit, scores, 0.0), axis=0, keepdims=True))
        masked = jnp.where(hit, neg, masked)
    w = jnp.concatenate(ws, axis=0)
    wts_ref[...] = w / jnp.sum(w, axis=0, keepdims=True) * ROUTED_SCALE

    earlier_tok = (lax.broadcasted_iota(I32, (tt, tt), 0) < lax.broadcasted_iota(I32, (tt, tt), 1))
    earlier_tok = jnp.where(earlier_tok, 1.0, 0.0).astype(BF16)
    earlier_exp = (lax.broadcasted_iota(I32, (e, e), 1) < lax.broadcasted_iota(I32, (e, e), 0))
    earlier_exp = jnp.where(earlier_exp, 1.0, 0.0).astype(BF16)
    lane = lax.broadcasted_iota(I32, (e, LANES), 1)
    run_out = jnp.zeros((e, LANES), I32)
    lpos = []
    for s in range(tm // tt):
        sl = slice(s * tt, (s + 1) * tt)
        chosen = jnp.zeros((e, tt), F32)
        for hit in hits:
            chosen = jnp.where(hit[:, sl], 1.0, chosen)
        rank = jnp.dot(chosen.astype(BF16), earlier_tok, preferred_element_type=F32)
        cnt = (rank[:, tt - 1:tt] + chosen[:, tt - 1:tt]).astype(I32)
        shift = SUBLANES.bit_length() - 1
        run = jnp.left_shift(jnp.right_shift(cnt + (SUBLANES - 1), shift), shift)
        run_b = jnp.broadcast_to(run.astype(F32), (e, LANES)).astype(BF16)
        off = jnp.dot(earlier_exp, run_b, preferred_element_type=F32)[:, 0:1]
        pos = (off + rank).astype(I32)
        lpos.append(jnp.concatenate(
            [jnp.sum(jnp.where(hit[:, sl], pos, 0), axis=0, keepdims=True) for hit in hits], axis=0))
        run_out = jnp.where(lane == s, run, run_out)
    lpos_ref[...] = jnp.concatenate(lpos, axis=1)
    run_ref[...] = run_out


def _router(x2d, router_w, router_bias, tt, tm=512):
    t, d = x2d.shape
    e = router_w.shape[1]
    assert e % N_GROUPS == 0 and (e // N_GROUPS) % SUBLANES == 0
    tm = max(_tile(t, tm), tt)
    assert tm % tt == 0 and tm // tt <= LANES and tt <= 256
    ep = -(-e // LANES) * LANES
    rw_pad = jnp.zeros((d, ep), F32).at[:, :e].set(router_w.astype(F32))
    wts_t, lpos_t, run_tab = pl.pallas_call(
        functools.partial(_router_kernel, tt=tt),
        grid=(t // tm,),
        in_specs=[pl.BlockSpec((tm, d), lambda i: (i, 0)),
                  pl.BlockSpec((d, ep), lambda i: (0, 0)),
                  pl.BlockSpec((e, 1), lambda i: (0, 0))],
        out_specs=[pl.BlockSpec((TOP_K, tm), lambda i: (0, i)),
                   pl.BlockSpec((TOP_K, tm), lambda i: (0, i)),
                   pl.BlockSpec((None, e, LANES), lambda i: (i, 0, 0))],
        out_shape=[jax.ShapeDtypeStruct((TOP_K, t), F32), jax.ShapeDtypeStruct((TOP_K, t), I32),
                   jax.ShapeDtypeStruct((t // tm, e, LANES), I32)],
        compiler_params=_cparams("arbitrary"),
    )(x2d, rw_pad, router_bias.reshape(e, 1).astype(F32))
    run = run_tab[:, :, :tm // tt].transpose(0, 2, 1).reshape(t // tt, e)
    return wts_t, lpos_t, run


def _seg_copy(src, dst, src_off, dst_off, length, max_len, sem, wait=False):
    done = 0
    b = max_len
    while b >= SUBLANES:
        piece = length & b

        @pl.when(piece != 0)
        def _(b=b, done=done):
            cp = pltpu.make_async_copy(src.at[pl.ds(pl.multiple_of(src_off + done, SUBLANES), b)],
                                       dst.at[pl.ds(pl.multiple_of(dst_off + done, SUBLANES), b)], sem)
            if wait:
                cp.wait()
            else:
                cp.start()
        done = done + piece
        b //= 2


def _pow2_floor(n):
    return 1 << (n.bit_length() - 1)


def _dispatch_kernel(tab_ref, pad_ref, lpt_ref, x_ref, o_hbm, xs, zbuf, sem, *, n_exp, rc):
    i = pl.program_id(0)
    n = pl.num_programs(0)
    slot = i % 2
    tt = x_ref.shape[0]
    r_tot, half = xs.shape[1], xs.shape[2]

    def wait_slot(s, step):
        _seg_copy(xs.at[s], o_hbm, 0, 0, pad_ref[2 * n_exp + step], _pow2_floor(r_tot), sem.at[s], wait=True)

    @pl.when(i >= 2)
    def _():
        wait_slot(slot, i - 2)

    xlo, xhi = x_ref[:, :half], x_ref[:, half:]
    lpt = lpt_ref[...]
    for c in range(r_tot // rc):
        rows = lax.broadcasted_iota(I32, (rc, tt), 0) + c * rc
        hit = rows == lpt[0:1, :]
        for k in range(1, TOP_K):
            hit = hit | (rows == lpt[k:k + 1, :])
        g = jnp.where(hit, 1.0, 0.0).astype(BF16)
        lo = jnp.dot(g, xlo, preferred_element_type=F32)
        hi = jnp.dot(g, xhi, preferred_element_type=F32)
        xs[slot, c * rc:(c + 1) * rc, :] = _pack_halves(lo, hi)

    for e in range(n_exp):
        _seg_copy(xs.at[slot], o_hbm, tab_ref[2 * n_exp + e], tab_ref[e], tab_ref[n_exp + e], tt, sem.at[slot])

    @pl.when(i == 0)
    def _():
        zbuf[...] = jnp.zeros_like(zbuf)
        for w in (False, True):
            for e in range(n_exp):
                _seg_copy(zbuf, o_hbm, 0, pad_ref[e], pad_ref[n_exp + e], zbuf.shape[0] // 2, sem.at[2], wait=w)

    @pl.when(i == n - 1)
    def _():
        @pl.when(n >= 2)
        def _():
            wait_slot(1 - slot, i - 1)
        wait_slot(slot, i)


def _tile_rows(tt, n_exp, rc=512):
    r = tt * TOP_K + (SUBLANES - 1) * n_exp
    rc = min(rc, tt * TOP_K)
    return -(-r // rc) * rc, rc


def _dispatch(xbf, tab, pad_tab, lpos_t, m_pad, n_exp, tt, bm):
    t, d = xbf.shape
    assert tt & (tt - 1) == 0 and bm & (bm - 1) == 0 and d % 2 == 0 and tt % SUBLANES == 0
    r_tot, rc = _tile_rows(tt, n_exp)
    return pl.pallas_call(
        functools.partial(_dispatch_kernel, n_exp=n_exp, rc=rc),
        grid=(t // tt,),
        in_specs=[pl.BlockSpec((4 * n_exp,), lambda i: (i,), memory_space=pltpu.SMEM),
                  pl.BlockSpec(memory_space=pltpu.SMEM),
                  pl.BlockSpec((TOP_K, tt), lambda i: (0, i)),
                  pl.BlockSpec((tt, d), lambda i: (i, 0))],
        out_specs=pl.BlockSpec(memory_space=pl.ANY),
        out_shape=jax.ShapeDtypeStruct((m_pad, d // 2), U32),
        scratch_shapes=[pltpu.VMEM((2, r_tot, d // 2), U32), pltpu.VMEM((bm, d // 2), U32),
                        pltpu.SemaphoreType.DMA((3,))],
        compiler_params=_cparams("arbitrary"),
    )(tab, pad_tab, lpos_t, xbf)


def _expert_kernel(be_ref, nv_ref, br_ref, nx_ref, od_ref, xs_ref, wgu_hbm, wd_hbm, y_ref,
                   wgu_f, wd_f, wgu_bf, wd_bf, sem, *, cast_rows, layer):
    i = pl.program_id(0)

    def weight_copies(expert, slot):
        return (pltpu.make_async_copy(wgu_hbm.at[layer, expert], wgu_f.at[slot], sem.at[0, slot]),
                pltpu.make_async_copy(wd_hbm.at[layer, expert], wd_f.at[slot], sem.at[1, slot]))

    @pl.when(i < nv_ref[0])
    def _():
        expert = be_ref[i]

        @pl.when((i == 0) | (expert != be_ref[jnp.maximum(i - 1, 0)]))
        def _():
            slot = od_ref[i] % 2

            @pl.when(i == 0)
            def _():
                for cp in weight_copies(expert, slot):
                    cp.start()
            for cp in weight_copies(expert, slot):
                cp.wait()

            @pl.when(nx_ref[i] >= 0)
            def _():
                for cp in weight_copies(nx_ref[i], 1 - slot):
                    cp.start()

            def cast_gu(c, carry):
                r = pl.ds(pl.multiple_of(c * cast_rows, cast_rows), cast_rows)
                wgu_bf[r, :] = wgu_f[slot, r, :].astype(BF16)
                return carry
            lax.fori_loop(0, wgu_bf.shape[0] // cast_rows, cast_gu, 0)
            wd_bf[...] = wd_f[slot].astype(BF16)

        half = xs_ref.shape[1]
        eh = wd_bf.shape[0]
        bm = xs_ref.shape[0]

        def mlp(rows):
            lo, hi = _unpack_rows(xs_ref[:rows, :])
            gu = (jnp.dot(lo, wgu_bf[:half, :], preferred_element_type=F32)
                  + jnp.dot(hi, wgu_bf[half:, :], preferred_element_type=F32))
            a = (_silu(gu[:, :eh]) * gu[:, eh:]).astype(BF16)
            y_ref[:rows, :] = _pack_rows(jnp.dot(a, wd_bf[...], preferred_element_type=F32))

        @pl.when(br_ref[i] > bm // 2)
        def _():
            mlp(bm)

        @pl.when(br_ref[i] <= bm // 2)
        def _():
            mlp(bm // 2)
            y_ref[bm // 2:, :] = jnp.zeros((bm - bm // 2, half), U32)


def _experts(xs, block_e, nvalid, block_rows, next_e, order, w_gu, w_down, layer, bm):
    m_pad, dh = xs.shape
    _, e, d, eh2 = w_gu.shape
    eh = w_down.shape[2]
    nb = m_pad // bm
    cast_rows = _tile(d, 256)

    def blk(i, be, nv, *_):
        return (jnp.minimum(i, nv[0] - 1), 0)

    grid_spec = pltpu.PrefetchScalarGridSpec(
        num_scalar_prefetch=5,
        grid=(nb,),
        in_specs=[pl.BlockSpec((bm, dh), blk),
                  pl.BlockSpec(memory_space=pl.ANY),
                  pl.BlockSpec(memory_space=pl.ANY)],
        out_specs=pl.BlockSpec((bm, dh), blk),
        scratch_shapes=[pltpu.VMEM((2, d, eh2), F32), pltpu.VMEM((2, eh, d), F32),
                        pltpu.VMEM((d, eh2), BF16), pltpu.VMEM((eh, d), BF16),
                        pltpu.SemaphoreType.DMA((2, 2))],
    )
    return pl.pallas_call(
        functools.partial(_expert_kernel, cast_rows=cast_rows, layer=layer),
        grid_spec=grid_spec,
        out_shape=jax.ShapeDtypeStruct((m_pad, dh), U32),
        compiler_params=_cparams("arbitrary"),
    )(block_e, nvalid, block_rows, next_e, order, xs, w_gu.astype(F32), w_down.astype(F32))


def _combine_kernel(tabc_ref, tabn_ref, lp_ref, w_ref, x_ref, sgu_ref, sd_ref, g_ref, b_ref, y_hbm,
                    o_ref, ybuf, sem, *, alpha, n_exp):
    i = pl.program_id(0)
    n = pl.num_programs(0)
    tt = x_ref.shape[0]
    r_tot = ybuf.shape[1]
    slot = i % 2

    def issue(tab_ref, s):
        for e in range(n_exp):
            _seg_copy(y_hbm, ybuf.at[s], tab_ref[e], tab_ref[2 * n_exp + e], tab_ref[n_exp + e], tt, sem.at[s])

    @pl.when(i == 0)
    def _():
        ybuf[...] = jnp.zeros_like(ybuf)
        issue(tabc_ref, 0)

    @pl.when(i + 1 < n)
    def _():
        issue(tabn_ref, 1 - slot)

    _seg_copy(y_hbm, ybuf.at[slot], 0, 0, tabc_ref[3 * n_exp], _pow2_floor(r_tot), sem.at[slot], wait=True)

    lp = lp_ref[...]
    w = w_ref[...]
    cols = lax.broadcasted_iota(I32, (tt, r_tot), 1)
    p = jnp.zeros((tt, r_tot), F32)
    for k in range(TOP_K):
        p = jnp.where(cols == lp[:, k:k + 1], w[:, k:k + 1], p)
    pb = p.astype(BF16)
    ylo, yhi = _unpack_rows(ybuf[slot])
    routed = jnp.concatenate([jnp.dot(pb, ylo, preferred_element_type=F32),
                              jnp.dot(pb, yhi, preferred_element_type=F32)], axis=1)

    x = x_ref[...]
    sh = sd_ref.shape[0]
    sg = jnp.dot(x.astype(BF16), sgu_ref[...], preferred_element_type=F32)
    a = (_silu(sg[:, :sh]) * sg[:, sh:]).astype(BF16)
    shared = jnp.dot(a, sd_ref[...], preferred_element_type=F32)
    o_ref[...] = _layer_norm(alpha * x + (routed + shared), g_ref[...], b_ref[...])


def _combine(x2d, wts, lpos, tab, y_sorted, s_gu_bf, s_down_bf, g, b, alpha, n_exp, tt):
    t, d = x2d.shape
    n = t // tt
    sh = s_down_bf.shape[0]
    return pl.pallas_call(
        functools.partial(_combine_kernel, alpha=alpha, n_exp=n_exp),
        grid=(n,),
        in_specs=[pl.BlockSpec((4 * n_exp,), lambda i: (i,), memory_space=pltpu.SMEM),
                  pl.BlockSpec((4 * n_exp,), lambda i: (jnp.minimum(i + 1, n - 1),), memory_space=pltpu.SMEM),
                  pl.BlockSpec((tt, TOP_K), lambda i: (i, 0)),
                  pl.BlockSpec((tt, TOP_K), lambda i: (i, 0)),
                  pl.BlockSpec((tt, d), lambda i: (i, 0)),
                  pl.BlockSpec((d, 2 * sh), lambda i: (0, 0)),
                  pl.BlockSpec((sh, d), lambda i: (0, 0)),
                  pl.BlockSpec((1, d), lambda i: (0, 0)),
                  pl.BlockSpec((1, d), lambda i: (0, 0)),
                  pl.BlockSpec(memory_space=pl.ANY)],
        out_specs=pl.BlockSpec((tt, d), lambda i: (i, 0)),
        out_shape=jax.ShapeDtypeStruct((t, d), F32),
        scratch_shapes=[pltpu.VMEM((2, _tile_rows(tt, n_exp)[0], d // 2), U32), pltpu.SemaphoreType.DMA((2,))],
        compiler_params=_cparams("arbitrary"),
    )(tab, tab, lpos, wts, x2d, s_gu_bf, s_down_bf,
      g.reshape(1, d).astype(F32), b.reshape(1, d).astype(F32), y_sorted)


def _moe_layer(x2d, xbf, router_w, router_bias, w_gu, w_down, layer, s_gu, s_down, g, b, alpha, bm=512, tt=256):
    t, d = x2d.shape
    e = router_w.shape[1]
    tt = _tile(t, tt)
    nt = t // tt
    wts_t, lpos_t, run = _router(x2d, router_w, router_bias, tt)

    before = jnp.cumsum(run, axis=0) - run
    off = jnp.cumsum(run, axis=1) - run
    counts = jnp.sum(run, axis=0)
    padded = ((counts + bm - 1) // bm) * bm
    ends = jnp.cumsum(padded)
    starts = ends - padded
    totals = jnp.sum(run, axis=1)
    spare = jnp.zeros_like(run).at[:, 0].set(totals)
    tab = jnp.concatenate([starts[None, :] + before, run, off, spare], axis=1).reshape(-1).astype(I32)
    pad_tab = jnp.concatenate([starts + counts, padded - counts, totals]).astype(I32)
    m_pad = -(-(t * TOP_K + (SUBLANES - 1) * nt * e) // bm) * bm + e * bm
    nb = m_pad // bm
    block_start = jnp.arange(nb, dtype=I32) * bm
    block_e = jnp.minimum(jnp.sum((ends[None, :] <= block_start[:, None]).astype(I32), axis=1), e - 1).astype(I32)
    nvalid = (ends[-1:] // bm).astype(I32)
    block_rows = jnp.sum(jnp.clip(jnp.minimum((starts + counts)[None, :], block_start[:, None] + bm)
                                  - jnp.maximum(starts[None, :], block_start[:, None]), 0, bm), axis=1).astype(I32)

    ar = jnp.arange(e, dtype=I32)
    has_rows = (padded > 0)[None, :]
    next_e = jnp.min(jnp.where((ar[None, :] > block_e[:, None]) & has_rows, ar[None, :], e), axis=1)
    next_e = jnp.where(next_e == e, -1, next_e).astype(I32)
    order = jnp.sum(((ar[None, :] < block_e[:, None]) & has_rows).astype(I32), axis=1).astype(I32)

    xs = _dispatch(xbf, tab, pad_tab, lpos_t, m_pad, e, tt, bm)
    y_sorted = _experts(xs, block_e, nvalid, block_rows, next_e, order, w_gu, w_down, layer, bm)
    return _combine(x2d, wts_t.T, lpos_t.T, tab, y_sorted, s_gu.astype(BF16), s_down.astype(BF16), g, b, alpha, e, tt)


def _proj_rope_kernel(x_ref, w_ref, cos_ref, sin_ref, o_ref):
    y = jnp.dot(x_ref[...].astype(BF16), w_ref[...], preferred_element_type=F32)
    tm, n = y.shape
    tw = cos_ref.shape[1]
    lane = lax.broadcasted_iota(I32, (tm, LANES), 1)
    first_half = (lane % HD_B) < (HD_B // 2)
    for c in range(n // LANES):
        yc = y[:, c * LANES:(c + 1) * LANES]
        off = (c * LANES) % tw
        rot = jnp.where(first_half, pltpu.roll(yc, LANES - HD_B // 2, axis=1), pltpu.roll(yc, HD_B // 2, axis=1))
        yc = yc * cos_ref[:, off:off + LANES] + rot * sin_ref[:, off:off + LANES]
        o_ref[:, c * LANES:(c + 1) * LANES] = yc.astype(o_ref.dtype)


def _proj_rope(x2d, w_bf, cos, sin, seq, tm=512, tn=1024):
    t, d = x2d.shape
    n = w_bf.shape[1]
    tw = cos.shape[1]
    tm, tn = _tile(seq, tm), _tile(n, tn)
    assert tn % tw == 0 and tw % LANES == 0
    ns = seq // tm
    return pl.pallas_call(
        _proj_rope_kernel,
        grid=(n // tn, t // tm),
        in_specs=[pl.BlockSpec((tm, d), lambda j, i: (i, 0)),
                  pl.BlockSpec((d, tn), lambda j, i: (0, j)),
                  pl.BlockSpec((tm, tw), lambda j, i: (i % ns, 0)),
                  pl.BlockSpec((tm, tw), lambda j, i: (i % ns, 0))],
        out_specs=pl.BlockSpec((tm, tn), lambda j, i: (i, j)),
        out_shape=jax.ShapeDtypeStruct((t, n), BF16),
        compiler_params=_cparams("arbitrary", "arbitrary"),
    )(x2d, w_bf, cos, sin)


def _rope_tables(seq, scale, heads, plain_cols=0):
    half = HD_B // 2
    inv = 1.0 / (ROPE_THETA ** (jnp.arange(half, dtype=F32) / half))
    ang = jnp.arange(seq, dtype=F32)[:, None] * inv[None, :]
    cos, sin = jnp.cos(ang), jnp.sin(ang)
    cos_t = jnp.tile(jnp.concatenate([cos, cos], axis=1), (1, heads)) * scale
    sin_t = jnp.tile(jnp.concatenate([-sin, sin], axis=1), (1, heads)) * scale
    cos_t = jnp.concatenate([cos_t, jnp.ones((seq, plain_cols), F32)], axis=1)
    sin_t = jnp.concatenate([sin_t, jnp.zeros((seq, plain_cols), F32)], axis=1)
    return cos_t, sin_t


def _swa_kernel(q_ref, kc_ref, kp_ref, vc_ref, vp_ref, sink_ref, o_ref, *, groups):
    qi = pl.program_id(2)
    tq = q_ref.shape[0]
    w = WINDOW
    row = lax.broadcasted_iota(I32, (w, 2 * w), 0)
    col = lax.broadcasted_iota(I32, (w, 2 * w), 1)
    band = (col > row) & (col <= row + w)
    neg = jnp.float32(-jnp.inf)
    sink = jnp.stack([sink_ref[g:g + 1, 0:1] for g in range(groups)])
    for j in range(tq // w):
        if j == 0:
            keys = [jnp.concatenate([kp_ref[s], kc_ref[s, :w, :]], axis=0) for s in range(2)]
            vals = [jnp.concatenate([vp_ref[s], vc_ref[s, :w, :]], axis=0) for s in range(2)]
            mask = band & ((col >= w) | (qi > 0))
        else:
            keys = [kc_ref[s, (j - 1) * w:(j + 1) * w, :] for s in range(2)]
            vals = [vc_ref[s, (j - 1) * w:(j + 1) * w, :] for s in range(2)]
            mask = band
        scs = []
        for p in range(groups // 2):
            qs = q_ref[j * w:(j + 1) * w, p * LANES:(p + 1) * LANES]
            for s in range(2):
                scs.append(lax.dot_general(qs, keys[s], (((1,), (1,)), ((), ())), preferred_element_type=F32))
        sc = jnp.where(mask[None], jnp.stack(scs), neg)
        m = jnp.maximum(jnp.max(jnp.maximum(sc[..., :w], sc[..., w:]), axis=-1, keepdims=True), sink)
        pr = jnp.exp(sc - m)
        den = jnp.sum(pr[..., :w] + pr[..., w:], axis=-1, keepdims=True) + jnp.exp(sink - m)
        pr = (pr * (1.0 / den)).astype(BF16)
        for p in range(groups // 2):
            out = (jnp.dot(pr[2 * p], vals[0], preferred_element_type=F32)
                   + jnp.dot(pr[2 * p + 1], vals[1], preferred_element_type=F32))
            o_ref[j * w:(j + 1) * w, p * LANES:(p + 1) * LANES] = out.astype(o_ref.dtype)


def _swa_attn(q, kk, vv, sink_tab, batch, seq, kvh, groups, tq=512):
    t = q.shape[0]
    tq = _tile(seq, tq)
    assert tq % WINDOW == 0 and groups % 2 == 0 and 2 * HD_B == LANES
    nq = seq // tq
    per = tq // WINDOW
    gw = groups * HD_B
    cur = pl.BlockSpec((None, None, 2, tq, LANES), lambda b, h, i: (b, h, 0, i, 0))
    prev = pl.BlockSpec((None, None, 2, WINDOW, LANES), lambda b, h, i: (b, h, 0, jnp.maximum(i * per - 1, 0), 0))
    return pl.pallas_call(
        functools.partial(_swa_kernel, groups=groups),
        grid=(batch, kvh, nq),
        in_specs=[pl.BlockSpec((tq, gw), lambda b, h, i: (b * nq + i, h)),
                  cur, prev, cur, prev,
                  pl.BlockSpec((None, groups, LANES), lambda b, h, i: (h, 0, 0))],
        out_specs=pl.BlockSpec((tq, gw), lambda b, h, i: (b * nq + i, h)),
        out_shape=jax.ShapeDtypeStruct((t, kvh * gw), BF16),
        compiler_params=_cparams("arbitrary", "arbitrary", "arbitrary"),
    )(q, kk, kk, vv, vv, sink_tab)


def _lane_pairs(a, batch, seq, kvh):
    a4 = a.reshape(batch, seq, kvh, HD_B).transpose(0, 2, 1, 3)
    z = jnp.zeros_like(a4)
    return jnp.stack([jnp.concatenate([a4, z], axis=-1), jnp.concatenate([z, a4], axis=-1)], axis=2)


def kernel(x, a_w_in, a_b_f, a_w_o, kv_w, b_w_q, b_sinks, b_w_o, router_w, router_bias, moe_w_gu, moe_w_down,
           shared_w_gu, shared_w_down, ln1_g, ln1_b, ln2_g, ln2_b):
    batch, seq, d = x.shape
    n_a, n_b = a_w_in.shape[0], b_w_q.shape[0]
    depth = n_a + n_b
    alpha = float((2 * depth) ** 0.25)
    heads_a = a_b_f.shape[1]
    hd_a = d // heads_a
    heads_b = b_sinks.shape[1]
    kvh = kv_w.shape[1] // (2 * HD_B)
    groups = heads_b // kvh
    t = batch * seq

    x2d = x.reshape(t, d).astype(F32)
    kk = vv = None
    cos_q, sin_q = _rope_tables(seq, 1.0 / math.sqrt(HD_B), LANES // HD_B)
    cos_k, sin_k = _rope_tables(seq, 1.0, kvh, plain_cols=kvh * HD_B)
    for i in range(depth):
        if i < n_a:
            qkv = _proj(x2d, a_w_in, i, 3 * d, scaled_cols=d, scale=LOG2E / math.sqrt(hd_a))
            cum = _fox_cum(x2d, a_w_in, i, 3 * d, a_b_f[i], batch, seq)
            tb = _tile(seq, 512)
            cum_t = cum[:, :heads_a].reshape(batch, seq, heads_a).transpose(0, 2, 1)
            cum_t = cum_t.reshape(batch, heads_a, seq // tb, 1, tb)
            attn = _fox_attn(qkv, cum_t, batch, seq, heads_a, hd_a, tb=tb)
            w_o = a_w_o[i]
        else:
            j = i - n_a
            q = _proj_rope(x2d, b_w_q[j].astype(BF16), cos_q, sin_q, seq)
            sink_tab = jnp.broadcast_to(b_sinks[j].astype(F32).reshape(kvh, groups, 1), (kvh, groups, LANES))
            attn = _swa_attn(q, kk, vv, sink_tab, batch, seq, kvh, groups)
            w_o = b_w_o[j]
        x2d, xbf = _oproj_ln(attn, w_o.astype(BF16), x2d, ln1_g[i], ln1_b[i], alpha)
        x2d = _moe_layer(x2d, xbf, router_w[i], router_bias[i], moe_w_gu, moe_w_down, i,
                         shared_w_gu[i], shared_w_down[i], ln2_g[i], ln2_b[i], alpha)
        if i == n_a - 1:
            kv = _proj_rope(x2d, kv_w.astype(BF16), cos_k, sin_k, seq, tn=kv_w.shape[1])
            kk = _lane_pairs(kv[:, :kvh * HD_B], batch, seq, kvh)
            vv = _lane_pairs(kv[:, kvh * HD_B:], batch, seq, kvh)
    return x2d.reshape(batch, seq, d).astype(x.dtype)
```

```python
import functools
import math

import jax
import jax.numpy as jnp
import numpy as np
from jax import lax
from jax.experimental import pallas as pl
from jax.experimental.pallas import tpu as pltpu

F32 = jnp.float32
BF16 = jnp.bfloat16
U32 = jnp.uint32
I32 = jnp.int32

LANES = 128
SUBLANES = 8
HD_B = 64
WINDOW = 128
ROPE_THETA = 10000.0
TOP_K = 8
N_GROUPS = 8
TOPK_GROUPS = 4
ROUTED_SCALE = 2.5
LN_EPS = 1e-5
VMEM_LIMIT_BYTES = 56 * 1024 * 1024
HI_MASK = 0xFFFF0000
LOG2E = math.log2(math.e)


def _cparams(*sem):
    return pltpu.CompilerParams(dimension_semantics=sem, vmem_limit_bytes=VMEM_LIMIT_BYTES)


def _tile(dim, pref):
    t = min(dim, pref)
    assert dim % t == 0, (dim, pref)
    return t


def _silu(x):
    return x * jax.nn.sigmoid(x)


def _layer_norm(z, g, b):
    mu = jnp.mean(z, axis=-1, keepdims=True)
    zc = z - mu
    var = jnp.mean(zc * zc, axis=-1, keepdims=True)
    return zc * lax.rsqrt(var + LN_EPS) * g + b


def _pack_halves(lo, hi):
    lo = lax.bitcast_convert_type(lo, U32)
    hi = lax.bitcast_convert_type(hi, U32)
    return (hi & jnp.uint32(HI_MASK)) | (lo >> jnp.uint32(16))


def _pack_rows(y):
    half = y.shape[1] // 2
    yb = y.astype(BF16).astype(F32)
    return _pack_halves(yb[:, :half], yb[:, half:])


def _unpack_rows(pk):
    lo = lax.bitcast_convert_type(pk << jnp.uint32(16), F32).astype(BF16)
    hi = lax.bitcast_convert_type(pk & jnp.uint32(HI_MASK), F32).astype(BF16)
    return lo, hi


def _proj_kernel(x_ref, w_ref, o_ref, w_bf, *, scaled_tiles, scale, cast_rows):
    j = pl.program_id(0)

    @pl.when(pl.program_id(1) == 0)
    def _():
        def cast(c, carry):
            r = pl.ds(pl.multiple_of(c * cast_rows, cast_rows), cast_rows)
            w_bf[r, :] = w_ref[r, :].astype(BF16)
            return carry
        lax.fori_loop(0, w_ref.shape[0] // cast_rows, cast, 0)

    y = jnp.dot(x_ref[...].astype(BF16), w_bf[...], preferred_element_type=F32)
    s = jnp.where(j < scaled_tiles, jnp.float32(scale), jnp.float32(1.0))
    o_ref[...] = (y * s).astype(o_ref.dtype)


def _proj(x2d, w, layer, n, *, scaled_cols, scale, tm=512, tn=1024):
    t, d = x2d.shape
    tm, tn = _tile(t, tm), _tile(scaled_cols, tn)
    assert n % tn == 0
    return pl.pallas_call(
        functools.partial(_proj_kernel, scaled_tiles=scaled_cols // tn, scale=scale, cast_rows=_tile(d, 256)),
        grid=(n // tn, t // tm),
        in_specs=[pl.BlockSpec((tm, d), lambda j, i: (i, 0)),
                  pl.BlockSpec((None, d, tn), lambda j, i: (layer, 0, j))],
        out_specs=pl.BlockSpec((tm, tn), lambda j, i: (i, j)),
        out_shape=jax.ShapeDtypeStruct((t, n), BF16),
        scratch_shapes=[pltpu.VMEM((d, tn), BF16)],
        compiler_params=_cparams("arbitrary", "arbitrary"),
    )(x2d, w)


def _fox_cum_kernel(x_ref, w_ref, b_ref, o_ref, carry_ref, *, heads):
    @pl.when(pl.program_id(1) == 0)
    def _():
        carry_ref[...] = jnp.zeros_like(carry_ref)

    ts = x_ref.shape[0]
    lane = lax.broadcasted_iota(I32, w_ref.shape, 1)
    w = jnp.where(lane < heads, w_ref[...], 0.0).astype(BF16)
    z = jnp.dot(x_ref[...].astype(BF16), w, preferred_element_type=F32) + b_ref[...]
    logf = -(jnp.maximum(-z, 0.0) + jnp.log1p(jnp.exp(-jnp.abs(z))))
    row = lax.broadcasted_iota(I32, (ts, ts), 0)
    col = lax.broadcasted_iota(I32, (ts, ts), 1)
    tril = (row >= col).astype(F32)
    cum = jnp.dot(tril, logf, preferred_element_type=F32, precision=lax.Precision.HIGHEST) + carry_ref[...]
    o_ref[...] = (cum * LOG2E).T
    carry_ref[...] = cum[ts - 1:ts, :]


def _fox_cum(x2d, w_in, layer, col0, b_f, batch, seq, ts=512):
    t, d = x2d.shape
    h = w_in.shape[2] - col0
    assert h <= LANES and col0 % LANES == 0
    ts = _tile(seq, ts)
    b_pad = jnp.zeros((1, LANES), F32).at[0, :h].set(b_f.astype(F32))
    ns = seq // ts
    return pl.pallas_call(
        functools.partial(_fox_cum_kernel, heads=h),
        grid=(batch, ns),
        in_specs=[pl.BlockSpec((ts, d), lambda b, s: (b * ns + s, 0)),
                  pl.BlockSpec((None, d, LANES), lambda b, s: (layer, 0, col0 // LANES)),
                  pl.BlockSpec((1, LANES), lambda b, s: (0, 0))],
        out_specs=pl.BlockSpec((None, LANES, ts), lambda b, s: (b, 0, s)),
        out_shape=jax.ShapeDtypeStruct((batch, LANES, seq), F32),
        scratch_shapes=[pltpu.VMEM((1, LANES), F32)],
        compiler_params=_cparams("arbitrary", "arbitrary"),
    )(x2d, w_in, b_pad)


def _fox_attn_kernel(q_ref, k_ref, v_ref, c_ref, o_ref, *, tb, hd):
    qi = pl.program_id(2)
    n_heads = q_ref.shape[1] // hd
    cols = [slice(h * hd, (h + 1) * hd) for h in range(n_heads)]
    qs = [q_ref[:, c] for c in cols]

    def scores(h, j):
        k = k_ref[pl.ds(pl.multiple_of(j * tb, tb), tb), cols[h]]
        s = lax.dot_general(qs[h], k, (((1,), (1,)), ((), ())), preferred_element_type=F32)
        return s - c_ref[h, j]

    def update(h, j, t, carry):
        m, l, acc = carry
        v = v_ref[pl.ds(pl.multiple_of(j * tb, tb), tb), cols[h]]
        m_new = jnp.maximum(m, jnp.max(t, axis=-1, keepdims=True))
        p = jnp.exp2(t - m_new)
        alpha = jnp.exp2(m - m_new)
        l = alpha * l + jnp.sum(p, axis=-1, keepdims=True)
        acc = alpha * acc + jnp.dot(p.astype(BF16), v, preferred_element_type=F32)
        return m_new, l, acc

    def body(j, carries):
        return tuple(update(h, j, scores(h, j), carries[h]) for h in range(n_heads))

    init = (jnp.full((tb, 1), -jnp.inf, F32), jnp.zeros((tb, 1), F32), jnp.zeros((tb, hd), F32))
    carries = lax.fori_loop(0, qi, body, (init,) * n_heads)
    row = lax.broadcasted_iota(I32, (tb, tb), 0)
    col = lax.broadcasted_iota(I32, (tb, tb), 1)
    for h in range(n_heads):
        t = jnp.where(row >= col, scores(h, qi), -jnp.inf)
        _, l, acc = update(h, qi, t, carries[h])
        o_ref[:, cols[h]] = (acc / l).astype(o_ref.dtype)


def _fox_attn(qkv, cum_t, batch, seq, heads, hd, tb=512, hp=4):
    t = qkv.shape[0]
    d = heads * hd
    tb = _tile(seq, tb)
    nq = seq // tb
    hp = _tile(heads, hp)
    hg = heads // hp
    assert hd % LANES == 0
    return pl.pallas_call(
        functools.partial(_fox_attn_kernel, tb=tb, hd=hd),
        grid=(batch, hg, nq),
        in_specs=[pl.BlockSpec((tb, hp * hd), lambda b, h, i: (b * nq + i, h)),
                  pl.BlockSpec((seq, hp * hd), lambda b, h, i: (b, hg + h)),
                  pl.BlockSpec((seq, hp * hd), lambda b, h, i: (b, 2 * hg + h)),
                  pl.BlockSpec((None, hp, nq, 1, tb), lambda b, h, i: (b, h, 0, 0, 0))],
        out_specs=pl.BlockSpec((tb, hp * hd), lambda b, h, i: (b * nq + i, h)),
        out_shape=jax.ShapeDtypeStruct((t, d), BF16),
        compiler_params=_cparams("arbitrary", "arbitrary", "arbitrary"),
    )(qkv, qkv, qkv, cum_t)


def _oproj_ln_kernel(a_ref, w_ref, x_ref, g_ref, b_ref, o_ref, ob_ref, *, alpha):
    h = jnp.dot(a_ref[...], w_ref[...], preferred_element_type=F32)
    y = _layer_norm(alpha * x_ref[...] + h, g_ref[...], b_ref[...])
    o_ref[...] = y
    ob_ref[...] = y.astype(BF16)


def _oproj_ln(a_bf, w_bf, x2d, g, b, alpha, tm=512):
    t, d = x2d.shape
    kk = a_bf.shape[1]
    tm = _tile(t, tm)
    return pl.pallas_call(
        functools.partial(_oproj_ln_kernel, alpha=alpha),
        grid=(t // tm,),
        in_specs=[pl.BlockSpec((tm, kk), lambda i: (i, 0)),
                  pl.BlockSpec((kk, d), lambda i: (0, 0)),
                  pl.BlockSpec((tm, d), lambda i: (i, 0)),
                  pl.BlockSpec((1, d), lambda i: (0, 0)),
                  pl.BlockSpec((1, d), lambda i: (0, 0))],
        out_specs=[pl.BlockSpec((tm, d), lambda i: (i, 0)),
                   pl.BlockSpec((tm, d), lambda i: (i, 0))],
        out_shape=[jax.ShapeDtypeStruct((t, d), F32), jax.ShapeDtypeStruct((t, d), BF16)],
        compiler_params=_cparams("arbitrary"),
    )(a_bf, w_bf, x2d, g.reshape(1, d).astype(F32), b.reshape(1, d).astype(F32))


def _first_max(vals, idx, sentinel):
    m = jnp.max(vals, axis=0, keepdims=True)
    first = jnp.min(jnp.where(vals == m, idx, sentinel), axis=0, keepdims=True)
    return m, first


def _router_kernel(x_ref, rw_ref, bias_ref, wts_ref, lpos_ref, run_ref, *, tt):
    e = bias_ref.shape[0]
    tm = x_ref.shape[0]
    per = e // N_GROUPS
    logits = jnp.dot(x_ref[...], rw_ref[...], preferred_element_type=F32, precision=lax.Precision.HIGHEST)
    scores = jax.nn.sigmoid(logits.T[:e, :])
    choice = scores + bias_ref[...]
    neg = jnp.float32(-jnp.inf)

    ip = lax.broadcasted_iota(I32, (per, tm), 0)
    gs = []
    for g in range(N_GROUPS):
        c = choice[g * per:(g + 1) * per, :]
        m1, f1 = _first_max(c, ip, per)
        m2 = jnp.max(jnp.where(ip == f1, neg, c), axis=0, keepdims=True)
        gs.append(m1 + m2)
    gscore = jnp.concatenate(gs, axis=0)

    ig = lax.broadcasted_iota(I32, (N_GROUPS, tm), 0)
    sel = jnp.zeros((N_GROUPS, tm), F32)
    for _ in range(TOPK_GROUPS):
        _, fg = _first_max(gscore, ig, N_GROUPS)
        hit = ig == fg
        sel = jnp.where(hit, 1.0, sel)
        gscore = jnp.where(hit, neg, gscore)

    masked = jnp.concatenate(
        [jnp.where(sel[g:g + 1, :] > 0.0, choice[g * per:(g + 1) * per, :], neg) for g in range(N_GROUPS)],
        axis=0)
    ie = lax.broadcasted_iota(I32, (e, tm), 0)
    hits, ws = [], []
    for _ in range(TOP_K):
        _, fe = _first_max(masked, ie, e)
        hit = ie == fe
        hits.append(hit)
        ws.append(jnp.sum(jnp.where(hit, scores, 0.0), axis=0, keepdims=True))
        masked = jnp.where(hit, neg, masked)
    w = jnp.concatenate(ws, axis=0)
    wts_ref[...] = w / jnp.sum(w, axis=0, keepdims=True) * ROUTED_SCALE

    earlier_tok = (lax.broadcasted_iota(I32, (tt, tt), 0) < lax.broadcasted_iota(I32, (tt, tt), 1))
    earlier_tok = jnp.where(earlier_tok, 1.0, 0.0).astype(BF16)
    earlier_exp = (lax.broadcasted_iota(I32, (e, e), 1) < lax.broadcasted_iota(I32, (e, e), 0))
    earlier_exp = jnp.where(earlier_exp, 1.0, 0.0).astype(BF16)
    lane = lax.broadcasted_iota(I32, (e, LANES), 1)
    run_out = jnp.zeros((e, LANES), I32)
    lpos = []
    for s in range(tm // tt):
        sl = slice(s * tt, (s + 1) * tt)
        chosen = jnp.zeros((e, tt), F32)
        for hit in hits:
            chosen = jnp.where(hit[:, sl], 1.0, chosen)
        rank = jnp.dot(chosen.astype(BF16), earlier_tok, preferred_element_type=F32)
        cnt = (rank[:, tt - 1:tt] + chosen[:, tt - 1:tt]).astype(I32)
        shift = SUBLANES.bit_length() - 1
        run = jnp.left_shift(jnp.right_shift(cnt + (SUBLANES - 1), shift), shift)
        run_b = jnp.broadcast_to(run.astype(F32), (e, LANES)).astype(BF16)
        off = jnp.dot(earlier_exp, run_b, preferred_element_type=F32)[:, 0:1]
        pos = (off + rank).astype(I32)
        lpos.append(jnp.concatenate(
            [jnp.sum(jnp.where(hit[:, sl], pos, 0), axis=0, keepdims=True) for hit in hits], axis=0))
        run_out = jnp.where(lane == s, run, run_out)
    lpos_ref[...] = jnp.concatenate(lpos, axis=1)
    run_ref[...] = run_out


def _router(x2d, router_w, router_bias, tt, tm=512):
    t, d = x2d.shape
    e = router_w.shape[1]
    assert e % N_GROUPS == 0 and (e // N_GROUPS) % SUBLANES == 0
    tm = max(_tile(t, tm), tt)
    assert tm % tt == 0 and tm // tt <= LANES and tt <= 256
    ep = -(-e // LANES) * LANES
    rw_pad = jnp.zeros((d, ep), F32).at[:, :e].set(router_w.astype(F32))
    wts_t, lpos_t, run_tab = pl.pallas_call(
        functools.partial(_router_kernel, tt=tt),
        grid=(t // tm,),
        in_specs=[pl.BlockSpec((tm, d), lambda i: (i, 0)),
                  pl.BlockSpec((d, ep), lambda i: (0, 0)),
                  pl.BlockSpec((e, 1), lambda i: (0, 0))],
        out_specs=[pl.BlockSpec((TOP_K, tm), lambda i: (0, i)),
                   pl.BlockSpec((TOP_K, tm), lambda i: (0, i)),
                   pl.BlockSpec((None, e, LANES), lambda i: (i, 0, 0))],
        out_shape=[jax.ShapeDtypeStruct((TOP_K, t), F32), jax.ShapeDtypeStruct((TOP_K, t), I32),
                   jax.ShapeDtypeStruct((t // tm, e, LANES), I32)],
        compiler_params=_cparams("arbitrary"),
    )(x2d, rw_pad, router_bias.reshape(e, 1).astype(F32))
    run = run_tab[:, :, :tm // tt].transpose(0, 2, 1).reshape(t // tt, e)
    return wts_t, lpos_t, run


def _seg_copy(src, dst, src_off, dst_off, length, max_len, sem, wait=False):
    def piece(b):
        done = length & (-2 * b)

        @pl.when((length & b) != 0)
        def _():
            cp = pltpu.make_async_copy(src.at[pl.ds(pl.multiple_of(src_off + done, SUBLANES), b)],
                                       dst.at[pl.ds(pl.multiple_of(dst_off + done, SUBLANES), b)], sem)
            if wait:
                cp.wait()
            else:
                cp.start()

    b = max_len
    while b >= SUBLANES:
        piece(b)
        b //= 2


def _pow2_floor(n):
    return 1 << (n.bit_length() - 1)


def _dispatch_kernel(tab_ref, pad_ref, lpt_ref, x_ref, o_hbm, xs, zbuf, sem, *, n_exp, rc):
    i = pl.program_id(0)
    n = pl.num_programs(0)
    slot = i % 2
    tt = x_ref.shape[0]
    r_tot, half = xs.shape[1], xs.shape[2]

    def wait_slot(s, step):
        _seg_copy(xs.at[s], o_hbm, 0, 0, pad_ref[2 * n_exp + step], _pow2_floor(r_tot), sem.at[s], wait=True)

    @pl.when(i >= 2)
    def _():
        wait_slot(slot, i - 2)

    xlo, xhi = x_ref[:, :half], x_ref[:, half:]
    lpt = lpt_ref[...]
    for c in range(r_tot // rc):
        rows = lax.broadcasted_iota(I32, (rc, tt), 0) + c * rc
        hit = rows == lpt[0:1, :]
        for k in range(1, TOP_K):
            hit = hit | (rows == lpt[k:k + 1, :])
        g = jnp.where(hit, 1.0, 0.0).astype(BF16)
        lo = jnp.dot(g, xlo, preferred_element_type=F32)
        hi = jnp.dot(g, xhi, preferred_element_type=F32)
        xs[slot, c * rc:(c + 1) * rc, :] = _pack_halves(lo, hi)

    for e in range(n_exp):
        _seg_copy(xs.at[slot], o_hbm, tab_ref[2 * n_exp + e], tab_ref[e], tab_ref[n_exp + e], tt, sem.at[slot])

    @pl.when(i == 0)
    def _():
        zbuf[...] = jnp.zeros_like(zbuf)
        for w in (False, True):
            for e in range(n_exp):
                _seg_copy(zbuf, o_hbm, 0, pad_ref[e], pad_ref[n_exp + e], zbuf.shape[0] // 2, sem.at[2], wait=w)

    @pl.when(i == n - 1)
    def _():
        @pl.when(n >= 2)
        def _():
            wait_slot(1 - slot, i - 1)
        wait_slot(slot, i)


def _tile_rows(tt, n_exp, rc=512):
    r = tt * TOP_K + (SUBLANES - 1) * n_exp
    rc = min(rc, tt * TOP_K)
    return -(-r // rc) * rc, rc


def _dispatch(xbf, tab, pad_tab, lpos_t, m_pad, n_exp, tt, bm):
    t, d = xbf.shape
    assert tt & (tt - 1) == 0 and bm & (bm - 1) == 0 and d % 2 == 0 and tt % SUBLANES == 0
    r_tot, rc = _tile_rows(tt, n_exp)
    return pl.pallas_call(
        functools.partial(_dispatch_kernel, n_exp=n_exp, rc=rc),
        grid=(t // tt,),
        in_specs=[pl.BlockSpec((4 * n_exp,), lambda i: (i,), memory_space=pltpu.SMEM),
                  pl.BlockSpec(memory_space=pltpu.SMEM),
                  pl.BlockSpec((TOP_K, tt), lambda i: (0, i)),
                  pl.BlockSpec((tt, d), lambda i: (i, 0))],
        out_specs=pl.BlockSpec(memory_space=pl.ANY),
        out_shape=jax.ShapeDtypeStruct((m_pad, d // 2), U32),
        scratch_shapes=[pltpu.VMEM((2, r_tot, d // 2), U32), pltpu.VMEM((bm, d // 2), U32),
                        pltpu.SemaphoreType.DMA((3,))],
        compiler_params=_cparams("arbitrary"),
    )(tab, pad_tab, lpos_t, xbf)


def _expert_kernel(be_ref, nv_ref, br_ref, nx_ref, od_ref, xs_ref, wgu_hbm, wd_hbm, y_ref,
                   wgu_f, wd_f, wgu_bf, wd_bf, sem, *, cast_rows, layer):
    i = pl.program_id(0)

    def weight_copies(expert, slot):
        return (pltpu.make_async_copy(wgu_hbm.at[layer, expert], wgu_f.at[slot], sem.at[0, slot]),
                pltpu.make_async_copy(wd_hbm.at[layer, expert], wd_f.at[slot], sem.at[1, slot]))

    @pl.when(i < nv_ref[0])
    def _():
        expert = be_ref[i]

        @pl.when((i == 0) | (expert != be_ref[jnp.maximum(i - 1, 0)]))
        def _():
            slot = od_ref[i] % 2

            @pl.when(i == 0)
            def _():
                for cp in weight_copies(expert, slot):
                    cp.start()
            for cp in weight_copies(expert, slot):
                cp.wait()

            @pl.when(nx_ref[i] >= 0)
            def _():
                for cp in weight_copies(nx_ref[i], 1 - slot):
                    cp.start()

            def cast_gu(c, carry):
                r = pl.ds(pl.multiple_of(c * cast_rows, cast_rows), cast_rows)
                wgu_bf[r, :] = wgu_f[slot, r, :].astype(BF16)
                return carry
            lax.fori_loop(0, wgu_bf.shape[0] // cast_rows, cast_gu, 0)
            wd_bf[...] = wd_f[slot].astype(BF16)

        half = xs_ref.shape[1]
        eh = wd_bf.shape[0]
        bm = xs_ref.shape[0]

        def mlp(rows):
            lo, hi = _unpack_rows(xs_ref[:rows, :])
            gu = (jnp.dot(lo, wgu_bf[:half, :], preferred_element_type=F32)
                  + jnp.dot(hi, wgu_bf[half:, :], preferred_element_type=F32))
            a = (_silu(gu[:, :eh]) * gu[:, eh:]).astype(BF16)
            y_ref[:rows, :] = _pack_rows(jnp.dot(a, wd_bf[...], preferred_element_type=F32))

        quarter = bm // 4
        for q in range(1, 5):
            @pl.when((br_ref[i] > (q - 1) * quarter) & ((br_ref[i] <= q * quarter) | (q == 4)))
            def _(q=q):
                mlp(q * quarter)
                if q < 4:
                    y_ref[q * quarter:, :] = jnp.zeros((bm - q * quarter, half), U32)


def _experts(xs, block_e, nvalid, block_rows, next_e, order, w_gu, w_down, layer, bm):
    m_pad, dh = xs.shape
    _, e, d, eh2 = w_gu.shape
    eh = w_down.shape[2]
    nb = m_pad // bm
    cast_rows = _tile(d, 256)

    def blk(i, be, nv, *_):
        return (jnp.minimum(i, nv[0] - 1), 0)

    grid_spec = pltpu.PrefetchScalarGridSpec(
        num_scalar_prefetch=5,
        grid=(nb,),
        in_specs=[pl.BlockSpec((bm, dh), blk),
                  pl.BlockSpec(memory_space=pl.ANY),
                  pl.BlockSpec(memory_space=pl.ANY)],
        out_specs=pl.BlockSpec((bm, dh), blk),
        scratch_shapes=[pltpu.VMEM((2, d, eh2), F32), pltpu.VMEM((2, eh, d), F32),
                        pltpu.VMEM((d, eh2), BF16), pltpu.VMEM((eh, d), BF16),
                        pltpu.SemaphoreType.DMA((2, 2))],
    )
    return pl.pallas_call(
        functools.partial(_expert_kernel, cast_rows=cast_rows, layer=layer),
        grid_spec=grid_spec,
        out_shape=jax.ShapeDtypeStruct((m_pad, dh), U32),
        compiler_params=_cparams("arbitrary"),
    )(block_e, nvalid, block_rows, next_e, order, xs, w_gu.astype(F32), w_down.astype(F32))


def _combine_kernel(tabc_ref, tabn_ref, lp_ref, w_ref, x_ref, sgu_ref, sd_ref, g_ref, b_ref, y_hbm,
                    o_ref, ybuf, sem, *, alpha, n_exp):
    i = pl.program_id(0)
    n = pl.num_programs(0)
    tt = x_ref.shape[0]
    r_tot = ybuf.shape[1]
    slot = i % 2

    def issue(tab_ref, s):
        for e in range(n_exp):
            _seg_copy(y_hbm, ybuf.at[s], tab_ref[e], tab_ref[2 * n_exp + e], tab_ref[n_exp + e], tt, sem.at[s])

    @pl.when(i == 0)
    def _():
        ybuf[...] = jnp.zeros_like(ybuf)
        issue(tabc_ref, 0)

    @pl.when(i + 1 < n)
    def _():
        issue(tabn_ref, 1 - slot)

    _seg_copy(y_hbm, ybuf.at[slot], 0, 0, tabc_ref[3 * n_exp], _pow2_floor(r_tot), sem.at[slot], wait=True)

    lp = lp_ref[...]
    w = w_ref[...]
    cols = lax.broadcasted_iota(I32, (tt, r_tot), 1)
    p = jnp.zeros((tt, r_tot), F32)
    for k in range(TOP_K):
        p = jnp.where(cols == lp[:, k:k + 1], w[:, k:k + 1], p)
    pb = p.astype(BF16)
    ylo, yhi = _unpack_rows(ybuf[slot])
    routed = jnp.concatenate([jnp.dot(pb, ylo, preferred_element_type=F32),
                              jnp.dot(pb, yhi, preferred_element_type=F32)], axis=1)

    x = x_ref[...]
    sh = sd_ref.shape[0]
    sg = jnp.dot(x.astype(BF16), sgu_ref[...], preferred_element_type=F32)
    a = (_silu(sg[:, :sh]) * sg[:, sh:]).astype(BF16)
    shared = jnp.dot(a, sd_ref[...], preferred_element_type=F32)
    o_ref[...] = _layer_norm(alpha * x + (routed + shared), g_ref[...], b_ref[...])


def _combine(x2d, wts, lpos, tab, y_sorted, s_gu_bf, s_down_bf, g, b, alpha, n_exp, tt):
    t, d = x2d.shape
    n = t // tt
    sh = s_down_bf.shape[0]
    return pl.pallas_call(
        functools.partial(_combine_kernel, alpha=alpha, n_exp=n_exp),
        grid=(n,),
        in_specs=[pl.BlockSpec((4 * n_exp,), lambda i: (i,), memory_space=pltpu.SMEM),
                  pl.BlockSpec((4 * n_exp,), lambda i: (jnp.minimum(i + 1, n - 1),), memory_space=pltpu.SMEM),
                  pl.BlockSpec((tt, TOP_K), lambda i: (i, 0)),
                  pl.BlockSpec((tt, TOP_K), lambda i: (i, 0)),
                  pl.BlockSpec((tt, d), lambda i: (i, 0)),
                  pl.BlockSpec((d, 2 * sh), lambda i: (0, 0)),
                  pl.BlockSpec((sh, d), lambda i: (0, 0)),
                  pl.BlockSpec((1, d), lambda i: (0, 0)),
                  pl.BlockSpec((1, d), lambda i: (0, 0)),
                  pl.BlockSpec(memory_space=pl.ANY)],
        out_specs=pl.BlockSpec((tt, d), lambda i: (i, 0)),
        out_shape=jax.ShapeDtypeStruct((t, d), F32),
        scratch_shapes=[pltpu.VMEM((2, _tile_rows(tt, n_exp)[0], d // 2), U32), pltpu.SemaphoreType.DMA((2,))],
        compiler_params=_cparams("arbitrary"),
    )(tab, tab, lpos, wts, x2d, s_gu_bf, s_down_bf,
      g.reshape(1, d).astype(F32), b.reshape(1, d).astype(F32), y_sorted)


def _moe_layer(x2d, xbf, router_w, router_bias, w_gu, w_down, layer, s_gu, s_down, g, b, alpha, bm=512, tt=256):
    t, d = x2d.shape
    e = router_w.shape[1]
    tt = _tile(t, tt)
    nt = t // tt
    wts_t, lpos_t, run = _router(x2d, router_w, router_bias, tt)

    before = jnp.cumsum(run, axis=0) - run
    off = jnp.cumsum(run, axis=1) - run
    counts = jnp.sum(run, axis=0)
    padded = ((counts + bm - 1) // bm) * bm
    ends = jnp.cumsum(padded)
    starts = ends - padded
    totals = jnp.sum(run, axis=1)
    spare = jnp.zeros_like(run).at[:, 0].set(totals)
    tab = jnp.concatenate([starts[None, :] + before, run, off, spare], axis=1).reshape(-1).astype(I32)
    pad_tab = jnp.concatenate([starts + counts, padded - counts, totals]).astype(I32)
    m_pad = -(-(t * TOP_K + (SUBLANES - 1) * nt * e) // bm) * bm + e * bm
    nb = m_pad // bm
    block_start = jnp.arange(nb, dtype=I32) * bm
    block_e = jnp.minimum(jnp.sum((ends[None, :] <= block_start[:, None]).astype(I32), axis=1), e - 1).astype(I32)
    nvalid = (ends[-1:] // bm).astype(I32)
    block_rows = jnp.sum(jnp.clip(jnp.minimum((starts + counts)[None, :], block_start[:, None] + bm)
                                  - jnp.maximum(starts[None, :], block_start[:, None]), 0, bm), axis=1).astype(I32)

    ar = jnp.arange(e, dtype=I32)
    has_rows = (padded > 0)[None, :]
    next_e = jnp.min(jnp.where((ar[None, :] > block_e[:, None]) & has_rows, ar[None, :], e), axis=1)
    next_e = jnp.where(next_e == e, -1, next_e).astype(I32)
    order = jnp.sum(((ar[None, :] < block_e[:, None]) & has_rows).astype(I32), axis=1).astype(I32)

    xs = _dispatch(xbf, tab, pad_tab, lpos_t, m_pad, e, tt, bm)
    y_sorted = _experts(xs, block_e, nvalid, block_rows, next_e, order, w_gu, w_down, layer, bm)
    return _combine(x2d, wts_t.T, lpos_t.T, tab, y_sorted, s_gu.astype(BF16), s_down.astype(BF16), g, b, alpha, e, tt)


def _proj_rope_kernel(x_ref, w_ref, cos_ref, sin_ref, o_ref, *, plain_from):
    y = jnp.dot(x_ref[...].astype(BF16), w_ref[...], preferred_element_type=F32)
    tm, n = y.shape
    tw = cos_ref.shape[1]
    lane = lax.broadcasted_iota(I32, (tm, LANES), 1)
    first_half = (lane % HD_B) < (HD_B // 2)
    for c in range(n // LANES):
        yc = y[:, c * LANES:(c + 1) * LANES]
        if c * LANES < plain_from:
            off = (c * LANES) % tw
            rot = jnp.where(first_half, pltpu.roll(yc, LANES - HD_B // 2, axis=1), pltpu.roll(yc, HD_B // 2, axis=1))
            yc = yc * cos_ref[:, off:off + LANES] + rot * sin_ref[:, off:off + LANES]
        o_ref[:, c * LANES:(c + 1) * LANES] = yc.astype(o_ref.dtype)


def _proj_rope(x2d, w_bf, cos, sin, seq, plain_from=None, tm=512, tn=1024):
    t, d = x2d.shape
    n = w_bf.shape[1]
    tw = cos.shape[1]
    tm, tn = _tile(seq, tm), _tile(n, tn)
    plain_from = n if plain_from is None else plain_from
    assert tn % tw == 0 and tw % LANES == 0 and plain_from % LANES == 0 and (plain_from == n or tn == n)
    ns = seq // tm
    return pl.pallas_call(
        functools.partial(_proj_rope_kernel, plain_from=plain_from),
        grid=(n // tn, t // tm),
        in_specs=[pl.BlockSpec((tm, d), lambda j, i: (i, 0)),
                  pl.BlockSpec((d, tn), lambda j, i: (0, j)),
                  pl.BlockSpec((tm, tw), lambda j, i: (i % ns, 0)),
                  pl.BlockSpec((tm, tw), lambda j, i: (i % ns, 0))],
        out_specs=pl.BlockSpec((tm, tn), lambda j, i: (i, j)),
        out_shape=jax.ShapeDtypeStruct((t, n), BF16),
        compiler_params=_cparams("arbitrary", "arbitrary"),
    )(x2d, w_bf, cos, sin)


def _rope_tables(seq, scale, heads, plain_cols=0):
    half = HD_B // 2
    f32 = np.float32
    inv = (f32(1.0) / np.power(f32(ROPE_THETA), np.arange(half, dtype=f32) / f32(half))).astype(f32)
    ang = (np.arange(seq, dtype=f32)[:, None] * inv[None, :]).astype(f32)
    cos, sin = np.cos(ang), np.sin(ang)
    cos_t = np.tile(np.concatenate([cos, cos], axis=1), (1, heads)) * f32(scale)
    sin_t = np.tile(np.concatenate([-sin, sin], axis=1), (1, heads)) * f32(scale)
    cos_t = np.concatenate([cos_t, np.ones((seq, plain_cols), f32)], axis=1)
    sin_t = np.concatenate([sin_t, np.zeros((seq, plain_cols), f32)], axis=1)
    return jnp.asarray(cos_t, F32), jnp.asarray(sin_t, F32)


def _swa_kernel(q_ref, kc_ref, kp_ref, vc_ref, vp_ref, sink_ref, o_ref, *, groups):
    qi = pl.program_id(2)
    tq = q_ref.shape[0]
    w = WINDOW
    row = lax.broadcasted_iota(I32, (w, 2 * w), 0)
    col = lax.broadcasted_iota(I32, (w, 2 * w), 1)
    band = (col > row) & (col <= row + w)
    neg = jnp.float32(-jnp.inf)
    sink = jnp.stack([sink_ref[g:g + 1, 0:1] for g in range(groups)])
    for j in range(tq // w):
        if j == 0:
            keys = [jnp.concatenate([kp_ref[s], kc_ref[s, :w, :]], axis=0) for s in range(2)]
            vals = [jnp.concatenate([vp_ref[s], vc_ref[s, :w, :]], axis=0) for s in range(2)]
            mask = band & ((col >= w) | (qi > 0))
        else:
            keys = [kc_ref[s, (j - 1) * w:(j + 1) * w, :] for s in range(2)]
            vals = [vc_ref[s, (j - 1) * w:(j + 1) * w, :] for s in range(2)]
            mask = band
        scs = []
        for p in range(groups // 2):
            qs = q_ref[j * w:(j + 1) * w, p * LANES:(p + 1) * LANES]
            for s in range(2):
                scs.append(lax.dot_general(qs, keys[s], (((1,), (1,)), ((), ())), preferred_element_type=F32))
        sc = jnp.where(mask[None], jnp.stack(scs), neg)
        m = jnp.maximum(jnp.max(jnp.maximum(sc[..., :w], sc[..., w:]), axis=-1, keepdims=True), sink)
        pr = jnp.exp(sc - m)
        den = jnp.sum(pr[..., :w] + pr[..., w:], axis=-1, keepdims=True) + jnp.exp(sink - m)
        pr = (pr * (1.0 / den)).astype(BF16)
        for p in range(groups // 2):
            out = (jnp.dot(pr[2 * p], vals[0], preferred_element_type=F32)
                   + jnp.dot(pr[2 * p + 1], vals[1], preferred_element_type=F32))
            o_ref[j * w:(j + 1) * w, p * LANES:(p + 1) * LANES] = out.astype(o_ref.dtype)


def _swa_attn(q, kk, vv, sink_tab, batch, seq, kvh, groups, tq=512):
    t = q.shape[0]
    tq = _tile(seq, tq)
    assert tq % WINDOW == 0 and groups % 2 == 0 and 2 * HD_B == LANES
    nq = seq // tq
    per = tq // WINDOW
    gw = groups * HD_B
    cur = pl.BlockSpec((None, None, 2, tq, LANES), lambda b, h, i: (b, h, 0, i, 0))
    prev = pl.BlockSpec((None, None, 2, WINDOW, LANES), lambda b, h, i: (b, h, 0, jnp.maximum(i * per - 1, 0), 0))
    return pl.pallas_call(
        functools.partial(_swa_kernel, groups=groups),
        grid=(batch, kvh, nq),
        in_specs=[pl.BlockSpec((tq, gw), lambda b, h, i: (b * nq + i, h)),
                  cur, prev, cur, prev,
                  pl.BlockSpec((None, groups, LANES), lambda b, h, i: (h, 0, 0))],
        out_specs=pl.BlockSpec((tq, gw), lambda b, h, i: (b * nq + i, h)),
        out_shape=jax.ShapeDtypeStruct((t, kvh * gw), BF16),
        compiler_params=_cparams("arbitrary", "arbitrary", "arbitrary"),
    )(q, kk, kk, vv, vv, sink_tab)


def _lane_pairs(a, batch, seq, kvh):
    a4 = a.reshape(batch, seq, kvh, HD_B).transpose(0, 2, 1, 3)
    z = jnp.zeros_like(a4)
    return jnp.stack([jnp.concatenate([a4, z], axis=-1), jnp.concatenate([z, a4], axis=-1)], axis=2)


def kernel(x, a_w_in, a_b_f, a_w_o, kv_w, b_w_q, b_sinks, b_w_o, router_w, router_bias, moe_w_gu, moe_w_down,
           shared_w_gu, shared_w_down, ln1_g, ln1_b, ln2_g, ln2_b):
    batch, seq, d = x.shape
    n_a, n_b = a_w_in.shape[0], b_w_q.shape[0]
    depth = n_a + n_b
    alpha = float((2 * depth) ** 0.25)
    heads_a = a_b_f.shape[1]
    hd_a = d // heads_a
    heads_b = b_sinks.shape[1]
    kvh = kv_w.shape[1] // (2 * HD_B)
    groups = heads_b // kvh
    t = batch * seq

    x2d = x.reshape(t, d).astype(F32)
    kk = vv = None
    cos_q, sin_q = _rope_tables(seq, 1.0 / math.sqrt(HD_B), LANES // HD_B)
    if (kvh * HD_B) % LANES == 0:
        k_plain_from = kvh * HD_B
        cos_k, sin_k = _rope_tables(seq, 1.0, LANES // HD_B)
    else:
        k_plain_from = None
        cos_k, sin_k = _rope_tables(seq, 1.0, kvh, plain_cols=kvh * HD_B)
    for i in range(depth):
        if i < n_a:
            qkv = _proj(x2d, a_w_in, i, 3 * d, scaled_cols=d, scale=LOG2E / math.sqrt(hd_a))
            cum = _fox_cum(x2d, a_w_in, i, 3 * d, a_b_f[i], batch, seq)
            tb = _tile(seq, 512)
            cum_t = cum[:, :heads_a, :].reshape(batch, heads_a, seq // tb, 1, tb)
            attn = _fox_attn(qkv, cum_t, batch, seq, heads_a, hd_a, tb=tb)
            w_o = a_w_o[i]
        else:
            j = i - n_a
            q = _proj_rope(x2d, b_w_q[j].astype(BF16), cos_q, sin_q, seq)
            sink_tab = jnp.broadcast_to(b_sinks[j].astype(F32).reshape(kvh, groups, 1), (kvh, groups, LANES))
            attn = _swa_attn(q, kk, vv, sink_tab, batch, seq, kvh, groups)
            w_o = b_w_o[j]
        x2d, xbf = _oproj_ln(attn, w_o.astype(BF16), x2d, ln1_g[i], ln1_b[i], alpha)
        x2d = _moe_layer(x2d, xbf, router_w[i], router_bias[i], moe_w_gu, moe_w_down, i,
                         shared_w_gu[i], shared_w_down[i], ln2_g[i], ln2_b[i], alpha)
        if i == n_a - 1:
            kv = _proj_rope(x2d, kv_w.astype(BF16), cos_k, sin_k, seq, plain_from=k_plain_from, tn=kv_w.shape[1])
            kk = _lane_pairs(kv[:, :kvh * HD_B], batch, seq, kvh)
            vv = _lane_pairs(kv[:, kvh * HD_B:], batch, seq, kvh)
    return x2d.reshape(batch, seq, d).astype(x.dtype)
```

```python
import functools
import math

import jax
import jax.numpy as jnp
import numpy as np
from jax import lax
from jax.experimental import pallas as pl
from jax.experimental.pallas import tpu as pltpu

F32 = jnp.float32
BF16 = jnp.bfloat16
U32 = jnp.uint32
I32 = jnp.int32

LANES = 128
SUBLANES = 8
HD_B = 64
WINDOW = 128
ROPE_THETA = 10000.0
TOP_K = 8
N_GROUPS = 8
TOPK_GROUPS = 4
ROUTED_SCALE = 2.5
LN_EPS = 1e-5
VMEM_LIMIT_BYTES = 56 * 1024 * 1024
HI_MASK = 0xFFFF0000
LOG2E = math.log2(math.e)


def _cparams(*sem):
    return pltpu.CompilerParams(dimension_semantics=sem, vmem_limit_bytes=VMEM_LIMIT_BYTES)


def _tile(dim, pref):
    t = min(dim, pref)
    assert dim % t == 0, (dim, pref)
    return t


def _silu(x):
    return x * jax.nn.sigmoid(x)


def _layer_norm(z, g, b):
    mu = jnp.mean(z, axis=-1, keepdims=True)
    zc = z - mu
    var = jnp.mean(zc * zc, axis=-1, keepdims=True)
    return zc * lax.rsqrt(var + LN_EPS) * g + b


def _pack_halves(lo, hi):
    lo = lax.bitcast_convert_type(lo, U32)
    hi = lax.bitcast_convert_type(hi, U32)
    return (hi & jnp.uint32(HI_MASK)) | (lo >> jnp.uint32(16))


def _pack_rows(y):
    half = y.shape[1] // 2
    yb = y.astype(BF16).astype(F32)
    return _pack_halves(yb[:, :half], yb[:, half:])


def _unpack_rows(pk):
    lo = lax.bitcast_convert_type(pk << jnp.uint32(16), F32).astype(BF16)
    hi = lax.bitcast_convert_type(pk & jnp.uint32(HI_MASK), F32).astype(BF16)
    return lo, hi


def _proj_kernel(x_ref, w_ref, o_ref, w_bf, *, scaled_tiles, scale, cast_rows):
    j = pl.program_id(0)

    @pl.when(pl.program_id(1) == 0)
    def _():
        def cast(c, carry):
            r = pl.ds(pl.multiple_of(c * cast_rows, cast_rows), cast_rows)
            w_bf[r, :] = w_ref[r, :].astype(BF16)
            return carry
        lax.fori_loop(0, w_ref.shape[0] // cast_rows, cast, 0)

    y = jnp.dot(x_ref[...].astype(BF16), w_bf[...], preferred_element_type=F32)
    s = jnp.where(j < scaled_tiles, jnp.float32(scale), jnp.float32(1.0))
    o_ref[...] = (y * s).astype(o_ref.dtype)


def _proj(x2d, w, layer, n, *, scaled_cols, scale, tm=512, tn=1024):
    t, d = x2d.shape
    tm, tn = _tile(t, tm), _tile(scaled_cols, tn)
    assert n % tn == 0
    return pl.pallas_call(
        functools.partial(_proj_kernel, scaled_tiles=scaled_cols // tn, scale=scale, cast_rows=_tile(d, 256)),
        grid=(n // tn, t // tm),
        in_specs=[pl.BlockSpec((tm, d), lambda j, i: (i, 0)),
                  pl.BlockSpec((None, d, tn), lambda j, i: (layer, 0, j))],
        out_specs=pl.BlockSpec((tm, tn), lambda j, i: (i, j)),
        out_shape=jax.ShapeDtypeStruct((t, n), BF16),
        scratch_shapes=[pltpu.VMEM((d, tn), BF16)],
        compiler_params=_cparams("arbitrary", "arbitrary"),
    )(x2d, w)


def _fox_cum_kernel(x_ref, w_ref, b_ref, o_ref, carry_ref, *, heads):
    @pl.when(pl.program_id(1) == 0)
    def _():
        carry_ref[...] = jnp.zeros_like(carry_ref)

    ts = x_ref.shape[0]
    lane = lax.broadcasted_iota(I32, w_ref.shape, 1)
    w = jnp.where(lane < heads, w_ref[...], 0.0).astype(BF16)
    z = jnp.dot(x_ref[...].astype(BF16), w, preferred_element_type=F32) + b_ref[...]
    logf = -(jnp.maximum(-z, 0.0) + jnp.log1p(jnp.exp(-jnp.abs(z))))
    row = lax.broadcasted_iota(I32, (ts, ts), 0)
    col = lax.broadcasted_iota(I32, (ts, ts), 1)
    tril = (row >= col).astype(F32)
    cum = jnp.dot(tril, logf, preferred_element_type=F32, precision=lax.Precision.HIGHEST) + carry_ref[...]
    o_ref[...] = (cum * LOG2E).T
    carry_ref[...] = cum[ts - 1:ts, :]


def _fox_cum(x2d, w_in, layer, col0, b_f, batch, seq, ts=512):
    t, d = x2d.shape
    h = w_in.shape[2] - col0
    assert h <= LANES and col0 % LANES == 0
    ts = _tile(seq, ts)
    b_pad = jnp.zeros((1, LANES), F32).at[0, :h].set(b_f.astype(F32))
    ns = seq // ts
    return pl.pallas_call(
        functools.partial(_fox_cum_kernel, heads=h),
        grid=(batch, ns),
        in_specs=[pl.BlockSpec((ts, d), lambda b, s: (b * ns + s, 0)),
                  pl.BlockSpec((None, d, LANES), lambda b, s: (layer, 0, col0 // LANES)),
                  pl.BlockSpec((1, LANES), lambda b, s: (0, 0))],
        out_specs=pl.BlockSpec((None, LANES, ts), lambda b, s: (b, 0, s)),
        out_shape=jax.ShapeDtypeStruct((batch, LANES, seq), F32),
        scratch_shapes=[pltpu.VMEM((1, LANES), F32)],
        compiler_params=_cparams("arbitrary", "arbitrary"),
    )(x2d, w_in, b_pad)


def _fox_attn_kernel(q_ref, k_ref, v_ref, c_ref, o_ref, *, tb, hd):
    qi = pl.program_id(2)
    n_heads = q_ref.shape[1] // hd
    cols = [slice(h * hd, (h + 1) * hd) for h in range(n_heads)]
    qs = [q_ref[:, c] for c in cols]

    def scores(h, j):
        k = k_ref[pl.ds(pl.multiple_of(j * tb, tb), tb), cols[h]]
        s = lax.dot_general(qs[h], k, (((1,), (1,)), ((), ())), preferred_element_type=F32)
        return s - c_ref[h, j]

    def update(h, j, t, carry):
        m, l, acc = carry
        v = v_ref[pl.ds(pl.multiple_of(j * tb, tb), tb), cols[h]]
        m_new = jnp.maximum(m, jnp.max(t, axis=-1, keepdims=True))
        p = jnp.exp2(t - m_new)
        alpha = jnp.exp2(m - m_new)
        l = alpha * l + jnp.sum(p, axis=-1, keepdims=True)
        acc = alpha * acc + jnp.dot(p.astype(BF16), v, preferred_element_type=F32)
        return m_new, l, acc

    def body(j, carries):
        return tuple(update(h, j, scores(h, j), carries[h]) for h in range(n_heads))

    init = (jnp.full((tb, 1), -jnp.inf, F32), jnp.zeros((tb, 1), F32), jnp.zeros((tb, hd), F32))
    carries = lax.fori_loop(0, qi, body, (init,) * n_heads)
    row = lax.broadcasted_iota(I32, (tb, tb), 0)
    col = lax.broadcasted_iota(I32, (tb, tb), 1)
    for h in range(n_heads):
        t = jnp.where(row >= col, scores(h, qi), -jnp.inf)
        _, l, acc = update(h, qi, t, carries[h])
        o_ref[:, cols[h]] = (acc / l).astype(o_ref.dtype)


def _fox_attn(qkv, cum_t, batch, seq, heads, hd, tb=512, hp=4):
    t = qkv.shape[0]
    d = heads * hd
    tb = _tile(seq, tb)
    nq = seq // tb
    hp = _tile(heads, hp)
    hg = heads // hp
    assert hd % LANES == 0
    return pl.pallas_call(
        functools.partial(_fox_attn_kernel, tb=tb, hd=hd),
        grid=(batch, hg, nq),
        in_specs=[pl.BlockSpec((tb, hp * hd), lambda b, h, i: (b * nq + i, h)),
                  pl.BlockSpec((seq, hp * hd), lambda b, h, i: (b, hg + h)),
                  pl.BlockSpec((seq, hp * hd), lambda b, h, i: (b, 2 * hg + h)),
                  pl.BlockSpec((None, hp, nq, 1, tb), lambda b, h, i: (b, h, 0, 0, 0))],
        out_specs=pl.BlockSpec((tb, hp * hd), lambda b, h, i: (b * nq + i, h)),
        out_shape=jax.ShapeDtypeStruct((t, d), BF16),
        compiler_params=_cparams("arbitrary", "arbitrary", "arbitrary"),
    )(qkv, qkv, qkv, cum_t)


def _oproj_ln_kernel(a_ref, w_ref, x_ref, g_ref, b_ref, o_ref, ob_ref, *, alpha):
    h = jnp.dot(a_ref[...], w_ref[...], preferred_element_type=F32)
    y = _layer_norm(alpha * x_ref[...] + h, g_ref[...], b_ref[...])
    o_ref[...] = y
    ob_ref[...] = y.astype(BF16)


def _oproj_ln(a_bf, w_bf, x2d, g, b, alpha, tm=512):
    t, d = x2d.shape
    kk = a_bf.shape[1]
    tm = _tile(t, tm)
    return pl.pallas_call(
        functools.partial(_oproj_ln_kernel, alpha=alpha),
        grid=(t // tm,),
        in_specs=[pl.BlockSpec((tm, kk), lambda i: (i, 0)),
                  pl.BlockSpec((kk, d), lambda i: (0, 0)),
                  pl.BlockSpec((tm, d), lambda i: (i, 0)),
                  pl.BlockSpec((1, d), lambda i: (0, 0)),
                  pl.BlockSpec((1, d), lambda i: (0, 0))],
        out_specs=[pl.BlockSpec((tm, d), lambda i: (i, 0)),
                   pl.BlockSpec((tm, d), lambda i: (i, 0))],
        out_shape=[jax.ShapeDtypeStruct((t, d), F32), jax.ShapeDtypeStruct((t, d), BF16)],
        compiler_params=_cparams("arbitrary"),
    )(a_bf, w_bf, x2d, g.reshape(1, d).astype(F32), b.reshape(1, d).astype(F32))


def _first_max(vals, idx, sentinel):
    m = jnp.max(vals, axis=0, keepdims=True)
    first = jnp.min(jnp.where(vals == m, idx, sentinel), axis=0, keepdims=True)
    return m, first


def _router_kernel(x_ref, rw_ref, bias_ref, wts_ref, lpos_ref, run_ref, *, tt):
    e = bias_ref.shape[0]
    tm = x_ref.shape[0]
    per = e // N_GROUPS
    logits = jnp.dot(x_ref[...], rw_ref[...], preferred_element_type=F32, precision=lax.Precision.HIGHEST)
    scores = jax.nn.sigmoid(logits.T[:e, :])
    choice = scores + bias_ref[...]
    neg = jnp.float32(-jnp.inf)

    ip = lax.broadcasted_iota(I32, (per, tm), 0)
    gs = []
    for g in range(N_GROUPS):
        c = choice[g * per:(g + 1) * per, :]
        m1, f1 = _first_max(c, ip, per)
        m2 = jnp.max(jnp.where(ip == f1, neg, c), axis=0, keepdims=True)
        gs.append(m1 + m2)
    gscore = jnp.concatenate(gs, axis=0)

    ig = lax.broadcasted_iota(I32, (N_GROUPS, tm), 0)
    sel = jnp.zeros((N_GROUPS, tm), F32)
    for _ in range(TOPK_GROUPS):
        _, fg = _first_max(gscore, ig, N_GROUPS)
        hit = ig == fg
        sel = jnp.where(hit, 1.0, sel)
        gscore = jnp.where(hit, neg, gscore)

    masked = jnp.concatenate(
        [jnp.where(sel[g:g + 1, :] > 0.0, choice[g * per:(g + 1) * per, :], neg) for g in range(N_GROUPS)],
        axis=0)
    ie = lax.broadcasted_iota(I32, (e, tm), 0)
    hits, ws = [], []
    for _ in range(TOP_K):
        _, fe = _first_max(masked, ie, e)
        hit = ie == fe
        hits.append(hit)
        ws.append(jnp.sum(jnp.where(hit, scores, 0.0), axis=0, keepdims=True))
        masked = jnp.where(hit, neg, masked)
    w = jnp.concatenate(ws, axis=0)
    wts_ref[...] = w / jnp.sum(w, axis=0, keepdims=True) * ROUTED_SCALE

    earlier_tok = (lax.broadcasted_iota(I32, (tt, tt), 0) < lax.broadcasted_iota(I32, (tt, tt), 1))
    earlier_tok = jnp.where(earlier_tok, 1.0, 0.0).astype(BF16)
    earlier_exp = (lax.broadcasted_iota(I32, (e, e), 1) < lax.broadcasted_iota(I32, (e, e), 0))
    earlier_exp = jnp.where(earlier_exp, 1.0, 0.0).astype(BF16)
    lane = lax.broadcasted_iota(I32, (e, LANES), 1)
    run_out = jnp.zeros((e, LANES), I32)
    lpos = []
    for s in range(tm // tt):
        sl = slice(s * tt, (s + 1) * tt)
        chosen = jnp.zeros((e, tt), F32)
        for hit in hits:
            chosen = jnp.where(hit[:, sl], 1.0, chosen)
        rank = jnp.dot(chosen.astype(BF16), earlier_tok, preferred_element_type=F32)
        cnt = (rank[:, tt - 1:tt] + chosen[:, tt - 1:tt]).astype(I32)
        shift = SUBLANES.bit_length() - 1
        run = jnp.left_shift(jnp.right_shift(cnt + (SUBLANES - 1), shift), shift)
        run_b = jnp.broadcast_to(run.astype(F32), (e, LANES)).astype(BF16)
        off = jnp.dot(earlier_exp, run_b, preferred_element_type=F32)[:, 0:1]
        pos = (off + rank).astype(I32)
        lpos.append(jnp.concatenate(
            [jnp.sum(jnp.where(hit[:, sl], pos, 0), axis=0, keepdims=True) for hit in hits], axis=0))
        run_out = jnp.where(lane == s, run, run_out)
    lpos_ref[...] = jnp.concatenate(lpos, axis=1)
    run_ref[...] = run_out


def _router(x2d, router_w, router_bias, tt, tm=512):
    t, d = x2d.shape
    e = router_w.shape[1]
    assert e % N_GROUPS == 0 and (e // N_GROUPS) % SUBLANES == 0
    tm = max(_tile(t, tm), tt)
    assert tm % tt == 0 and tm // tt <= LANES and tt <= 256
    ep = -(-e // LANES) * LANES
    rw_pad = jnp.zeros((d, ep), F32).at[:, :e].set(router_w.astype(F32))
    wts_t, lpos_t, run_tab = pl.pallas_call(
        functools.partial(_router_kernel, tt=tt),
        grid=(t // tm,),
        in_specs=[pl.BlockSpec((tm, d), lambda i: (i, 0)),
                  pl.BlockSpec((d, ep), lambda i: (0, 0)),
                  pl.BlockSpec((e, 1), lambda i: (0, 0))],
        out_specs=[pl.BlockSpec((TOP_K, tm), lambda i: (0, i)),
                   pl.BlockSpec((TOP_K, tm), lambda i: (0, i)),
                   pl.BlockSpec((None, e, LANES), lambda i: (i, 0, 0))],
        out_shape=[jax.ShapeDtypeStruct((TOP_K, t), F32), jax.ShapeDtypeStruct((TOP_K, t), I32),
                   jax.ShapeDtypeStruct((t // tm, e, LANES), I32)],
        compiler_params=_cparams("arbitrary"),
    )(x2d, rw_pad, router_bias.reshape(e, 1).astype(F32))
    run = run_tab[:, :, :tm // tt].transpose(0, 2, 1).reshape(t // tt, e)
    return wts_t, lpos_t, run


def _seg_copy(src, dst, src_off, dst_off, length, max_len, sem, wait=False):
    def piece(b):
        done = length & (-2 * b)

        @pl.when((length & b) != 0)
        def _():
            cp = pltpu.make_async_copy(src.at[pl.ds(pl.multiple_of(src_off + done, SUBLANES), b)],
                                       dst.at[pl.ds(pl.multiple_of(dst_off + done, SUBLANES), b)], sem)
            if wait:
                cp.wait()
            else:
                cp.start()

    b = max_len
    while b >= SUBLANES:
        piece(b)
        b //= 2


def _pow2_floor(n):
    return 1 << (n.bit_length() - 1)


def _dispatch_kernel(tab_ref, pad_ref, lpt_ref, x_ref, o_hbm, xs, zbuf, sem, *, n_exp, rc):
    i = pl.program_id(0)
    n = pl.num_programs(0)
    slot = i % 2
    tt = x_ref.shape[0]
    r_tot, half = xs.shape[1], xs.shape[2]

    def wait_slot(s, step):
        _seg_copy(xs.at[s], o_hbm, 0, 0, pad_ref[2 * n_exp + step], _pow2_floor(r_tot), sem.at[s], wait=True)

    @pl.when(i >= 2)
    def _():
        wait_slot(slot, i - 2)

    xlo, xhi = x_ref[:, :half], x_ref[:, half:]
    lpt = lpt_ref[...]
    for c in range(r_tot // rc):
        rows = lax.broadcasted_iota(I32, (rc, tt), 0) + c * rc
        hit = rows == lpt[0:1, :]
        for k in range(1, TOP_K):
            hit = hit | (rows == lpt[k:k + 1, :])
        g = jnp.where(hit, 1.0, 0.0).astype(BF16)
        lo = jnp.dot(g, xlo, preferred_element_type=F32)
        hi = jnp.dot(g, xhi, preferred_element_type=F32)
        xs[slot, c * rc:(c + 1) * rc, :] = _pack_halves(lo, hi)

    for e in range(n_exp):
        _seg_copy(xs.at[slot], o_hbm, tab_ref[2 * n_exp + e], tab_ref[e], tab_ref[n_exp + e], tt, sem.at[slot])

    @pl.when(i == 0)
    def _():
        zbuf[...] = jnp.zeros_like(zbuf)
        for w in (False, True):
            for e in range(n_exp):
                _seg_copy(zbuf, o_hbm, 0, pad_ref[e], pad_ref[n_exp + e], zbuf.shape[0] // 2, sem.at[2], wait=w)

    @pl.when(i == n - 1)
    def _():
        @pl.when(n >= 2)
        def _():
            wait_slot(1 - slot, i - 1)
        wait_slot(slot, i)


def _tile_rows(tt, n_exp, rc=512):
    r = tt * TOP_K + (SUBLANES - 1) * n_exp
    rc = min(rc, tt * TOP_K)
    return -(-r // rc) * rc, rc


def _dispatch(xbf, tab, pad_tab, lpos_t, m_pad, n_exp, tt, bm):
    t, d = xbf.shape
    assert tt & (tt - 1) == 0 and bm & (bm - 1) == 0 and d % 2 == 0 and tt % SUBLANES == 0
    r_tot, rc = _tile_rows(tt, n_exp)
    return pl.pallas_call(
        functools.partial(_dispatch_kernel, n_exp=n_exp, rc=rc),
        grid=(t // tt,),
        in_specs=[pl.BlockSpec((4 * n_exp,), lambda i: (i,), memory_space=pltpu.SMEM),
                  pl.BlockSpec(memory_space=pltpu.SMEM),
                  pl.BlockSpec((TOP_K, tt), lambda i: (0, i)),
                  pl.BlockSpec((tt, d), lambda i: (i, 0))],
        out_specs=pl.BlockSpec(memory_space=pl.ANY),
        out_shape=jax.ShapeDtypeStruct((m_pad, d // 2), U32),
        scratch_shapes=[pltpu.VMEM((2, r_tot, d // 2), U32), pltpu.VMEM((bm, d // 2), U32),
                        pltpu.SemaphoreType.DMA((3,))],
        compiler_params=_cparams("arbitrary"),
    )(tab, pad_tab, lpos_t, xbf)


def _expert_kernel(be_ref, nv_ref, br_ref, nx_ref, od_ref, xs_ref, wgu_hbm, wd_hbm, y_ref,
                   wgu_f, wd_f, wgu_bf, wd_bf, sem, *, cast_rows, layer):
    i = pl.program_id(0)

    def weight_copies(expert, slot):
        return (pltpu.make_async_copy(wgu_hbm.at[layer, expert], wgu_f.at[slot], sem.at[0, slot]),
                pltpu.make_async_copy(wd_hbm.at[layer, expert], wd_f.at[slot], sem.at[1, slot]))

    @pl.when(i < nv_ref[0])
    def _():
        expert = be_ref[i]

        @pl.when((i == 0) | (expert != be_ref[jnp.maximum(i - 1, 0)]))
        def _():
            slot = od_ref[i] % 2

            @pl.when(i == 0)
            def _():
                for cp in weight_copies(expert, slot):
                    cp.start()
            for cp in weight_copies(expert, slot):
                cp.wait()

            @pl.when(nx_ref[i] >= 0)
            def _():
                for cp in weight_copies(nx_ref[i], 1 - slot):
                    cp.start()

            def cast_gu(c, carry):
                r = pl.ds(pl.multiple_of(c * cast_rows, cast_rows), cast_rows)
                wgu_bf[r, :] = wgu_f[slot, r, :].astype(BF16)
                return carry
            lax.fori_loop(0, wgu_bf.shape[0] // cast_rows, cast_gu, 0)
            wd_bf[...] = wd_f[slot].astype(BF16)

        half = xs_ref.shape[1]
        eh = wd_bf.shape[0]
        bm = xs_ref.shape[0]

        def mlp(rows):
            lo, hi = _unpack_rows(xs_ref[:rows, :])
            gu = (jnp.dot(lo, wgu_bf[:half, :], preferred_element_type=F32)
                  + jnp.dot(hi, wgu_bf[half:, :], preferred_element_type=F32))
            a = (_silu(gu[:, :eh]) * gu[:, eh:]).astype(BF16)
            y_ref[:rows, :] = _pack_rows(jnp.dot(a, wd_bf[...], preferred_element_type=F32))

        quarter = bm // 4
        for q in range(1, 5):
            @pl.when((br_ref[i] > (q - 1) * quarter) & ((br_ref[i] <= q * quarter) | (q == 4)))
            def _(q=q):
                mlp(q * quarter)
                if q < 4:
                    y_ref[q * quarter:, :] = jnp.zeros((bm - q * quarter, half), U32)


def _experts(xs, block_e, nvalid, block_rows, next_e, order, w_gu, w_down, layer, bm):
    m_pad, dh = xs.shape
    _, e, d, eh2 = w_gu.shape
    eh = w_down.shape[2]
    nb = m_pad // bm
    cast_rows = _tile(d, 256)

    def blk(i, be, nv, *_):
        return (jnp.minimum(i, nv[0] - 1), 0)

    grid_spec = pltpu.PrefetchScalarGridSpec(
        num_scalar_prefetch=5,
        grid=(nb,),
        in_specs=[pl.BlockSpec((bm, dh), blk),
                  pl.BlockSpec(memory_space=pl.ANY),
                  pl.BlockSpec(memory_space=pl.ANY)],
        out_specs=pl.BlockSpec((bm, dh), blk),
        scratch_shapes=[pltpu.VMEM((2, d, eh2), F32), pltpu.VMEM((2, eh, d), F32),
                        pltpu.VMEM((d, eh2), BF16), pltpu.VMEM((eh, d), BF16),
                        pltpu.SemaphoreType.DMA((2, 2))],
    )
    return pl.pallas_call(
        functools.partial(_expert_kernel, cast_rows=cast_rows, layer=layer),
        grid_spec=grid_spec,
        out_shape=jax.ShapeDtypeStruct((m_pad, dh), U32),
        compiler_params=_cparams("arbitrary"),
    )(block_e, nvalid, block_rows, next_e, order, xs, w_gu.astype(F32), w_down.astype(F32))


def _combine_kernel(tabc_ref, tabn_ref, lp_ref, w_ref, x_ref, sgu_ref, sd_ref, g_ref, b_ref, y_hbm,
                    o_ref, ybuf, sem, *, alpha, n_exp):
    i = pl.program_id(0)
    n = pl.num_programs(0)
    tt = x_ref.shape[0]
    r_tot = ybuf.shape[1]
    slot = i % 2

    def issue(tab_ref, s):
        for e in range(n_exp):
            _seg_copy(y_hbm, ybuf.at[s], tab_ref[e], tab_ref[2 * n_exp + e], tab_ref[n_exp + e], tt, sem.at[s])

    @pl.when(i == 0)
    def _():
        ybuf[...] = jnp.zeros_like(ybuf)
        issue(tabc_ref, 0)

    @pl.when(i + 1 < n)
    def _():
        issue(tabn_ref, 1 - slot)

    _seg_copy(y_hbm, ybuf.at[slot], 0, 0, tabc_ref[3 * n_exp], _pow2_floor(r_tot), sem.at[slot], wait=True)

    lp = lp_ref[...]
    w = w_ref[...]
    cols = lax.broadcasted_iota(I32, (tt, r_tot), 1)
    p = jnp.zeros((tt, r_tot), F32)
    for k in range(TOP_K):
        p = jnp.where(cols == lp[:, k:k + 1], w[:, k:k + 1], p)
    pb = p.astype(BF16)
    ylo, yhi = _unpack_rows(ybuf[slot])
    routed = jnp.concatenate([jnp.dot(pb, ylo, preferred_element_type=F32),
                              jnp.dot(pb, yhi, preferred_element_type=F32)], axis=1)

    x = x_ref[...]
    sh = sd_ref.shape[0]
    sg = jnp.dot(x.astype(BF16), sgu_ref[...], preferred_element_type=F32)
    a = (_silu(sg[:, :sh]) * sg[:, sh:]).astype(BF16)
    shared = jnp.dot(a, sd_ref[...], preferred_element_type=F32)
    o_ref[...] = _layer_norm(alpha * x + (routed + shared), g_ref[...], b_ref[...])


def _combine(x2d, wts, lpos, tab, y_sorted, s_gu_bf, s_down_bf, g, b, alpha, n_exp, tt):
    t, d = x2d.shape
    n = t // tt
    sh = s_down_bf.shape[0]
    return pl.pallas_call(
        functools.partial(_combine_kernel, alpha=alpha, n_exp=n_exp),
        grid=(n,),
        in_specs=[pl.BlockSpec((4 * n_exp,), lambda i: (i,), memory_space=pltpu.SMEM),
                  pl.BlockSpec((4 * n_exp,), lambda i: (jnp.minimum(i + 1, n - 1),), memory_space=pltpu.SMEM),
                  pl.BlockSpec((tt, TOP_K), lambda i: (i, 0)),
                  pl.BlockSpec((tt, TOP_K), lambda i: (i, 0)),
                  pl.BlockSpec((tt, d), lambda i: (i, 0)),
                  pl.BlockSpec((d, 2 * sh), lambda i: (0, 0)),
                  pl.BlockSpec((sh, d), lambda i: (0, 0)),
                  pl.BlockSpec((1, d), lambda i: (0, 0)),
                  pl.BlockSpec((1, d), lambda i: (0, 0)),
                  pl.BlockSpec(memory_space=pl.ANY)],
        out_specs=pl.BlockSpec((tt, d), lambda i: (i, 0)),
        out_shape=jax.ShapeDtypeStruct((t, d), F32),
        scratch_shapes=[pltpu.VMEM((2, _tile_rows(tt, n_exp)[0], d // 2), U32), pltpu.SemaphoreType.DMA((2,))],
        compiler_params=_cparams("arbitrary"),
    )(tab, tab, lpos, wts, x2d, s_gu_bf, s_down_bf,
      g.reshape(1, d).astype(F32), b.reshape(1, d).astype(F32), y_sorted)


def _moe_layer(x2d, xbf, router_w, router_bias, w_gu, w_down, layer, s_gu, s_down, g, b, alpha, bm=512, tt=256):
    t, d = x2d.shape
    e = router_w.shape[1]
    tt = _tile(t, tt)
    nt = t // tt
    wts_t, lpos_t, run = _router(x2d, router_w, router_bias, tt)

    before = jnp.cumsum(run, axis=0) - run
    off = jnp.cumsum(run, axis=1) - run
    counts = jnp.sum(run, axis=0)
    padded = ((counts + bm - 1) // bm) * bm
    ends = jnp.cumsum(padded)
    starts = ends - padded
    totals = jnp.sum(run, axis=1)
    spare = jnp.zeros_like(run).at[:, 0].set(totals)
    tab = jnp.concatenate([starts[None, :] + before, run, off, spare], axis=1).reshape(-1).astype(I32)
    pad_tab = jnp.concatenate([starts + counts, padded - counts, totals]).astype(I32)
    m_pad = -(-(t * TOP_K + (SUBLANES - 1) * nt * e) // bm) * bm + e * bm
    nb = m_pad // bm
    block_start = jnp.arange(nb, dtype=I32) * bm
    block_e = jnp.minimum(jnp.sum((ends[None, :] <= block_start[:, None]).astype(I32), axis=1), e - 1).astype(I32)
    nvalid = (ends[-1:] // bm).astype(I32)
    block_rows = jnp.sum(jnp.clip(jnp.minimum((starts + counts)[None, :], block_start[:, None] + bm)
                                  - jnp.maximum(starts[None, :], block_start[:, None]), 0, bm), axis=1).astype(I32)

    ar = jnp.arange(e, dtype=I32)
    has_rows = (padded > 0)[None, :]
    next_e = jnp.min(jnp.where((ar[None, :] > block_e[:, None]) & has_rows, ar[None, :], e), axis=1)
    next_e = jnp.where(next_e == e, -1, next_e).astype(I32)
    order = jnp.sum(((ar[None, :] < block_e[:, None]) & has_rows).astype(I32), axis=1).astype(I32)

    xs = _dispatch(xbf, tab, pad_tab, lpos_t, m_pad, e, tt, bm)
    y_sorted = _experts(xs, block_e, nvalid, block_rows, next_e, order, w_gu, w_down, layer, bm)
    return _combine(x2d, wts_t.T, lpos_t.T, tab, y_sorted, s_gu.astype(BF16), s_down.astype(BF16), g, b, alpha, e, tt)


def _proj_rope_kernel(x_ref, w_ref, cos_ref, sin_ref, o_ref):
    y = jnp.dot(x_ref[...].astype(BF16), w_ref[...], preferred_element_type=F32)
    tm, n = y.shape
    lane = lax.broadcasted_iota(I32, (tm, LANES), 1)
    first_half = (lane % HD_B) < (HD_B // 2)
    for c in range(n // LANES):
        yc = y[:, c * LANES:(c + 1) * LANES]
        rot = jnp.where(first_half, pltpu.roll(yc, LANES - HD_B // 2, axis=1), pltpu.roll(yc, HD_B // 2, axis=1))
        o_ref[:, c * LANES:(c + 1) * LANES] = (yc * cos_ref[...] + rot * sin_ref[...]).astype(o_ref.dtype)


def _proj_rope(x2d, w_bf, cos, sin, seq, tm=512, tn=1024):
    t, d = x2d.shape
    n = w_bf.shape[1]
    tm, tn = _tile(seq, tm), _tile(n, tn)
    assert tn % LANES == 0 and cos.shape[1] == LANES
    ns = seq // tm
    return pl.pallas_call(
        _proj_rope_kernel,
        grid=(n // tn, t // tm),
        in_specs=[pl.BlockSpec((tm, d), lambda j, i: (i, 0)),
                  pl.BlockSpec((d, tn), lambda j, i: (0, j)),
                  pl.BlockSpec((tm, LANES), lambda j, i: (i % ns, 0)),
                  pl.BlockSpec((tm, LANES), lambda j, i: (i % ns, 0))],
        out_specs=pl.BlockSpec((tm, tn), lambda j, i: (i, j)),
        out_shape=jax.ShapeDtypeStruct((t, n), BF16),
        compiler_params=_cparams("arbitrary", "arbitrary"),
    )(x2d, w_bf, cos, sin)


def _rope_tables(seq, scale):
    half = HD_B // 2
    f32 = np.float32
    inv = (f32(1.0) / np.power(f32(ROPE_THETA), np.arange(half, dtype=f32) / f32(half))).astype(f32)
    ang = (np.arange(seq, dtype=f32)[:, None] * inv[None, :]).astype(f32)
    cos, sin = np.cos(ang), np.sin(ang)
    reps = LANES // HD_B
    cos_t = np.tile(np.concatenate([cos, cos], axis=1), (1, reps)) * f32(scale)
    sin_t = np.tile(np.concatenate([-sin, sin], axis=1), (1, reps)) * f32(scale)
    return jnp.asarray(cos_t, F32), jnp.asarray(sin_t, F32)


def _swa_kernel(q_ref, kc_ref, kp_ref, vc_ref, vp_ref, sink_ref, o_ref, *, groups):
    qi = pl.program_id(2)
    tq = q_ref.shape[0]
    w = WINDOW
    row = lax.broadcasted_iota(I32, (w, 2 * w), 0)
    col = lax.broadcasted_iota(I32, (w, 2 * w), 1)
    band = (col > row) & (col <= row + w)
    neg = jnp.float32(-jnp.inf)
    sink = jnp.stack([sink_ref[g:g + 1, 0:1] for g in range(groups)])
    ones = jnp.ones((2 * w, LANES), BF16)
    for j in range(tq // w):
        if j == 0:
            keys = [jnp.concatenate([kp_ref[s], kc_ref[s, :w, :]], axis=0) for s in range(2)]
            vals = [jnp.concatenate([vp_ref[s], vc_ref[s, :w, :]], axis=0) for s in range(2)]
            mask = band & ((col >= w) | (qi > 0))
        else:
            keys = [kc_ref[s, (j - 1) * w:(j + 1) * w, :] for s in range(2)]
            vals = [vc_ref[s, (j - 1) * w:(j + 1) * w, :] for s in range(2)]
            mask = band
        scs = []
        for p in range(groups // 2):
            qs = q_ref[j * w:(j + 1) * w, p * LANES:(p + 1) * LANES]
            for s in range(2):
                scs.append(lax.dot_general(qs, keys[s], (((1,), (1,)), ((), ())), preferred_element_type=F32))
        sc = jnp.where(mask[None], jnp.stack(scs), neg)
        m = jnp.maximum(jnp.max(jnp.maximum(sc[..., :w], sc[..., w:]), axis=-1, keepdims=True), sink)
        m_b = jnp.broadcast_to(m, (groups, w, LANES))
        pr = jnp.concatenate([jnp.exp(sc[..., :w] - m_b), jnp.exp(sc[..., w:] - m_b)], axis=-1).astype(BF16)
        sink_term = jnp.exp(sink - m_b)
        for p in range(groups // 2):
            out = jnp.zeros((w, LANES), F32)
            for s in range(2):
                g = 2 * p + s
                den = jnp.dot(pr[g], ones, preferred_element_type=F32) + sink_term[g]
                out = out + jnp.dot(pr[g], vals[s], preferred_element_type=F32) * (1.0 / den)
            o_ref[j * w:(j + 1) * w, p * LANES:(p + 1) * LANES] = out.astype(o_ref.dtype)


def _swa_attn(q, kk, vv, sink_tab, batch, seq, kvh, groups, tq=512):
    t = q.shape[0]
    tq = _tile(seq, tq)
    assert tq % WINDOW == 0 and groups % 2 == 0 and 2 * HD_B == LANES
    nq = seq // tq
    per = tq // WINDOW
    gw = groups * HD_B
    cur = pl.BlockSpec((None, None, 2, tq, LANES), lambda b, h, i: (b, h, 0, i, 0))
    prev = pl.BlockSpec((None, None, 2, WINDOW, LANES), lambda b, h, i: (b, h, 0, jnp.maximum(i * per - 1, 0), 0))
    return pl.pallas_call(
        functools.partial(_swa_kernel, groups=groups),
        grid=(batch, kvh, nq),
        in_specs=[pl.BlockSpec((tq, gw), lambda b, h, i: (b * nq + i, h)),
                  cur, prev, cur, prev,
                  pl.BlockSpec((None, groups, LANES), lambda b, h, i: (h, 0, 0))],
        out_specs=pl.BlockSpec((tq, gw), lambda b, h, i: (b * nq + i, h)),
        out_shape=jax.ShapeDtypeStruct((t, kvh * gw), BF16),
        compiler_params=_cparams("arbitrary", "arbitrary", "arbitrary"),
    )(q, kk, kk, vv, vv, sink_tab)


def _kv_proj_kernel(x_ref, w_ref, cos_ref, sin_ref, kk_ref, vv_ref, *, kvh):
    y = jnp.dot(x_ref[...].astype(BF16), w_ref[...], preferred_element_type=F32)
    tm = y.shape[0]
    lane = lax.broadcasted_iota(I32, (tm, LANES), 1)
    first_half = (lane % HD_B) < (HD_B // 2)
    low = lane < HD_B
    for part, o_ref in enumerate((kk_ref, vv_ref)):
        for c in range(kvh // 2):
            col = part * kvh * HD_B + c * LANES
            yc = y[:, col:col + LANES]
            if part == 0:
                rot = jnp.where(first_half, pltpu.roll(yc, LANES - HD_B // 2, axis=1),
                                pltpu.roll(yc, HD_B // 2, axis=1))
                yc = yc * cos_ref[...] + rot * sin_ref[...]
            sw = pltpu.roll(yc, HD_B, axis=1)
            o_ref[2 * c, 0] = jnp.where(low, yc, 0.0).astype(o_ref.dtype)
            o_ref[2 * c, 1] = jnp.where(low, 0.0, sw).astype(o_ref.dtype)
            o_ref[2 * c + 1, 0] = jnp.where(low, sw, 0.0).astype(o_ref.dtype)
            o_ref[2 * c + 1, 1] = jnp.where(low, 0.0, yc).astype(o_ref.dtype)


def _kv_proj(x2d, w_bf, cos, sin, batch, seq, kvh, tm=512):
    t, d = x2d.shape
    n = w_bf.shape[1]
    tm = _tile(seq, tm)
    ns = seq // tm
    assert kvh % 2 == 0 and n == 2 * kvh * HD_B and cos.shape[1] == LANES
    out = jax.ShapeDtypeStruct((batch, kvh, 2, seq, LANES), BF16)
    out_spec = pl.BlockSpec((None, kvh, 2, tm, LANES), lambda i: (i // ns, 0, 0, i % ns, 0))
    return pl.pallas_call(
        functools.partial(_kv_proj_kernel, kvh=kvh),
        grid=(t // tm,),
        in_specs=[pl.BlockSpec((tm, d), lambda i: (i, 0)),
                  pl.BlockSpec((d, n), lambda i: (0, 0)),
                  pl.BlockSpec((tm, LANES), lambda i: (i % ns, 0)),
                  pl.BlockSpec((tm, LANES), lambda i: (i % ns, 0))],
        out_specs=[out_spec, out_spec],
        out_shape=[out, out],
        compiler_params=_cparams("arbitrary"),
    )(x2d, w_bf, cos, sin)


def kernel(x, a_w_in, a_b_f, a_w_o, kv_w, b_w_q, b_sinks, b_w_o, router_w, router_bias, moe_w_gu, moe_w_down,
           shared_w_gu, shared_w_down, ln1_g, ln1_b, ln2_g, ln2_b):
    batch, seq, d = x.shape
    n_a, n_b = a_w_in.shape[0], b_w_q.shape[0]
    depth = n_a + n_b
    alpha = float((2 * depth) ** 0.25)
    heads_a = a_b_f.shape[1]
    hd_a = d // heads_a
    heads_b = b_sinks.shape[1]
    kvh = kv_w.shape[1] // (2 * HD_B)
    groups = heads_b // kvh
    t = batch * seq

    x2d = x.reshape(t, d).astype(F32)
    kk = vv = None
    cos_q, sin_q = _rope_tables(seq, 1.0 / math.sqrt(HD_B))
    cos_k, sin_k = _rope_tables(seq, 1.0)
    for i in range(depth):
        if i < n_a:
            qkv = _proj(x2d, a_w_in, i, 3 * d, scaled_cols=d, scale=LOG2E / math.sqrt(hd_a))
            cum = _fox_cum(x2d, a_w_in, i, 3 * d, a_b_f[i], batch, seq)
            tb = _tile(seq, 512)
            cum_t = cum[:, :heads_a, :].reshape(batch, heads_a, seq // tb, 1, tb)
            attn = _fox_attn(qkv, cum_t, batch, seq, heads_a, hd_a, tb=tb)
            w_o = a_w_o[i]
        else:
            j = i - n_a
            q = _proj_rope(x2d, b_w_q[j].astype(BF16), cos_q, sin_q, seq)
            sink_tab = jnp.broadcast_to(b_sinks[j].astype(F32).reshape(kvh, groups, 1), (kvh, groups, LANES))
            attn = _swa_attn(q, kk, vv, sink_tab, batch, seq, kvh, groups)
            w_o = b_w_o[j]
        x2d, xbf = _oproj_ln(attn, w_o.astype(BF16), x2d, ln1_g[i], ln1_b[i], alpha)
        x2d = _moe_layer(x2d, xbf, router_w[i], router_bias[i], moe_w_gu, moe_w_down, i,
                         shared_w_gu[i], shared_w_down[i], ln2_g[i], ln2_b[i], alpha)
        if i == n_a - 1:
            kk, vv = _kv_proj(x2d, kv_w.astype(BF16), cos_k, sin_k, batch, seq, kvh)
    return x2d.reshape(batch, seq, d).astype(x.dtype)
```

```python
import functools
import math

import jax
import jax.numpy as jnp
import numpy as np
from jax import lax
from jax.experimental import pallas as pl
from jax.experimental.pallas import tpu as pltpu

F32 = jnp.float32
BF16 = jnp.bfloat16
U32 = jnp.uint32
I32 = jnp.int32

LANES = 128
SUBLANES = 8
HD_B = 64
WINDOW = 128
ROPE_THETA = 10000.0
TOP_K = 8
N_GROUPS = 8
TOPK_GROUPS = 4
ROUTED_SCALE = 2.5
LN_EPS = 1e-5
VMEM_LIMIT_BYTES = 56 * 1024 * 1024
HI_MASK = 0xFFFF0000
LOG2E = math.log2(math.e)


def _cparams(*sem):
    return pltpu.CompilerParams(dimension_semantics=sem, vmem_limit_bytes=VMEM_LIMIT_BYTES)


def _tile(dim, pref):
    t = min(dim, pref)
    assert dim % t == 0, (dim, pref)
    return t


def _silu(x):
    return x * jax.nn.sigmoid(x)


def _layer_norm(z, g, b):
    mu = jnp.mean(z, axis=-1, keepdims=True)
    zc = z - mu
    var = jnp.mean(zc * zc, axis=-1, keepdims=True)
    return zc * lax.rsqrt(var + LN_EPS) * g + b


def _pack_halves(lo, hi):
    lo = lax.bitcast_convert_type(lo, U32)
    hi = lax.bitcast_convert_type(hi, U32)
    return (hi & jnp.uint32(HI_MASK)) | (lo >> jnp.uint32(16))


def _pack_rows(y):
    half = y.shape[1] // 2
    yb = y.astype(BF16).astype(F32)
    return _pack_halves(yb[:, :half], yb[:, half:])


def _unpack_rows(pk):
    lo = lax.bitcast_convert_type(pk << jnp.uint32(16), F32).astype(BF16)
    hi = lax.bitcast_convert_type(pk & jnp.uint32(HI_MASK), F32).astype(BF16)
    return lo, hi


def _proj_kernel(x_ref, w_ref, o_ref, w_bf, *, scaled_tiles, scale, cast_rows):
    j = pl.program_id(0)

    @pl.when(pl.program_id(1) == 0)
    def _():
        def cast(c, carry):
            r = pl.ds(pl.multiple_of(c * cast_rows, cast_rows), cast_rows)
            w_bf[r, :] = w_ref[r, :].astype(BF16)
            return carry
        lax.fori_loop(0, w_ref.shape[0] // cast_rows, cast, 0)

    y = jnp.dot(x_ref[...].astype(BF16), w_bf[...], preferred_element_type=F32)
    s = jnp.where(j < scaled_tiles, jnp.float32(scale), jnp.float32(1.0))
    o_ref[...] = (y * s).astype(o_ref.dtype)


def _proj(x2d, w, layer, n, *, scaled_cols, scale, tm=512, tn=1024):
    t, d = x2d.shape
    tm, tn = _tile(t, tm), _tile(scaled_cols, tn)
    assert n % tn == 0
    return pl.pallas_call(
        functools.partial(_proj_kernel, scaled_tiles=scaled_cols // tn, scale=scale, cast_rows=_tile(d, 256)),
        grid=(n // tn, t // tm),
        in_specs=[pl.BlockSpec((tm, d), lambda j, i: (i, 0)),
                  pl.BlockSpec((None, d, tn), lambda j, i: (layer, 0, j))],
        out_specs=pl.BlockSpec((tm, tn), lambda j, i: (i, j)),
        out_shape=jax.ShapeDtypeStruct((t, n), BF16),
        scratch_shapes=[pltpu.VMEM((d, tn), BF16)],
        compiler_params=_cparams("arbitrary", "arbitrary"),
    )(x2d, w)


def _fox_cum_kernel(x_ref, w_ref, b_ref, o_ref, carry_ref, *, heads):
    @pl.when(pl.program_id(1) == 0)
    def _():
        carry_ref[...] = jnp.zeros_like(carry_ref)

    ts = x_ref.shape[0]
    lane = lax.broadcasted_iota(I32, w_ref.shape, 1)
    w = jnp.where(lane < heads, w_ref[...], 0.0).astype(BF16)
    z = jnp.dot(x_ref[...].astype(BF16), w, preferred_element_type=F32) + b_ref[...]
    logf = -(jnp.maximum(-z, 0.0) + jnp.log1p(jnp.exp(-jnp.abs(z))))
    row = lax.broadcasted_iota(I32, (ts, ts), 0)
    col = lax.broadcasted_iota(I32, (ts, ts), 1)
    tril = (row >= col).astype(F32)
    cum = jnp.dot(tril, logf, preferred_element_type=F32, precision=lax.Precision.HIGHEST) + carry_ref[...]
    o_ref[...] = (cum * LOG2E).T
    carry_ref[...] = cum[ts - 1:ts, :]


def _fox_cum(x2d, w_in, layer, col0, b_f, batch, seq, ts=512):
    t, d = x2d.shape
    h = w_in.shape[2] - col0
    assert h <= LANES and col0 % LANES == 0
    ts = _tile(seq, ts)
    b_pad = jnp.zeros((1, LANES), F32).at[0, :h].set(b_f.astype(F32))
    ns = seq // ts
    return pl.pallas_call(
        functools.partial(_fox_cum_kernel, heads=h),
        grid=(batch, ns),
        in_specs=[pl.BlockSpec((ts, d), lambda b, s: (b * ns + s, 0)),
                  pl.BlockSpec((None, d, LANES), lambda b, s: (layer, 0, col0 // LANES)),
                  pl.BlockSpec((1, LANES), lambda b, s: (0, 0))],
        out_specs=pl.BlockSpec((None, LANES, ts), lambda b, s: (b, 0, s)),
        out_shape=jax.ShapeDtypeStruct((batch, LANES, seq), F32),
        scratch_shapes=[pltpu.VMEM((1, LANES), F32)],
        compiler_params=_cparams("arbitrary", "arbitrary"),
    )(x2d, w_in, b_pad)


def _fox_attn_kernel(q_ref, k_ref, v_ref, c_ref, o_ref, *, tb, hd):
    qi = pl.program_id(2)
    n_heads = q_ref.shape[1] // hd
    cols = [slice(h * hd, (h + 1) * hd) for h in range(n_heads)]
    qs = [q_ref[:, c] for c in cols]

    def scores(h, j):
        k = k_ref[pl.ds(pl.multiple_of(j * tb, tb), tb), cols[h]]
        s = lax.dot_general(qs[h], k, (((1,), (1,)), ((), ())), preferred_element_type=F32)
        return s - c_ref[h, j]

    def update(h, j, t, carry):
        m, l, acc = carry
        v = v_ref[pl.ds(pl.multiple_of(j * tb, tb), tb), cols[h]]
        m_new = jnp.maximum(m, jnp.max(t, axis=-1, keepdims=True))
        p = jnp.exp2(t - m_new)
        alpha = jnp.exp2(m - m_new)
        l = alpha * l + jnp.sum(p, axis=-1, keepdims=True)
        acc = alpha * acc + jnp.dot(p.astype(BF16), v, preferred_element_type=F32)
        return m_new, l, acc

    def body(j, carries):
        return tuple(update(h, j, scores(h, j), carries[h]) for h in range(n_heads))

    init = (jnp.full((tb, 1), -jnp.inf, F32), jnp.zeros((tb, 1), F32), jnp.zeros((tb, hd), F32))
    carries = lax.fori_loop(0, qi, body, (init,) * n_heads)
    row = lax.broadcasted_iota(I32, (tb, tb), 0)
    col = lax.broadcasted_iota(I32, (tb, tb), 1)
    for h in range(n_heads):
        t = jnp.where(row >= col, scores(h, qi), -jnp.inf)
        _, l, acc = update(h, qi, t, carries[h])
        o_ref[:, cols[h]] = (acc / l).astype(o_ref.dtype)


def _fox_attn(qkv, cum_t, batch, seq, heads, hd, tb=512, hp=4):
    t = qkv.shape[0]
    d = heads * hd
    tb = _tile(seq, tb)
    nq = seq // tb
    hp = _tile(heads, hp)
    hg = heads // hp
    assert hd % LANES == 0
    return pl.pallas_call(
        functools.partial(_fox_attn_kernel, tb=tb, hd=hd),
        grid=(batch, hg, nq),
        in_specs=[pl.BlockSpec((tb, hp * hd), lambda b, h, i: (b * nq + i, h)),
                  pl.BlockSpec((seq, hp * hd), lambda b, h, i: (b, hg + h)),
                  pl.BlockSpec((seq, hp * hd), lambda b, h, i: (b, 2 * hg + h)),
                  pl.BlockSpec((None, hp, nq, 1, tb), lambda b, h, i: (b, h, 0, 0, 0))],
        out_specs=pl.BlockSpec((tb, hp * hd), lambda b, h, i: (b * nq + i, h)),
        out_shape=jax.ShapeDtypeStruct((t, d), BF16),
        compiler_params=_cparams("arbitrary", "arbitrary", "arbitrary"),
    )(qkv, qkv, qkv, cum_t)


def _oproj_ln_kernel(a_ref, w_ref, x_ref, g_ref, b_ref, o_ref, ob_ref, *, alpha):
    h = jnp.dot(a_ref[...], w_ref[...], preferred_element_type=F32)
    y = _layer_norm(alpha * x_ref[...] + h, g_ref[...], b_ref[...])
    o_ref[...] = y
    ob_ref[...] = y.astype(BF16)


def _oproj_ln(a_bf, w_bf, x2d, g, b, alpha, tm=512):
    t, d = x2d.shape
    kk = a_bf.shape[1]
    tm = _tile(t, tm)
    return pl.pallas_call(
        functools.partial(_oproj_ln_kernel, alpha=alpha),
        grid=(t // tm,),
        in_specs=[pl.BlockSpec((tm, kk), lambda i: (i, 0)),
                  pl.BlockSpec((kk, d), lambda i: (0, 0)),
                  pl.BlockSpec((tm, d), lambda i: (i, 0)),
                  pl.BlockSpec((1, d), lambda i: (0, 0)),
                  pl.BlockSpec((1, d), lambda i: (0, 0))],
        out_specs=[pl.BlockSpec((tm, d), lambda i: (i, 0)),
                   pl.BlockSpec((tm, d), lambda i: (i, 0))],
        out_shape=[jax.ShapeDtypeStruct((t, d), F32), jax.ShapeDtypeStruct((t, d), BF16)],
        compiler_params=_cparams("arbitrary"),
    )(a_bf, w_bf, x2d, g.reshape(1, d).astype(F32), b.reshape(1, d).astype(F32))


def _first_max(vals, idx, sentinel):
    m = jnp.max(vals, axis=0, keepdims=True)
    first = jnp.min(jnp.where(vals == m, idx, sentinel), axis=0, keepdims=True)
    return m, first


def _router_kernel(x_ref, rwh_ref, rwl_ref, bias_ref, wts_ref, lpos_ref, run_ref, *, tt):
    e = bias_ref.shape[0]
    tm = x_ref.shape[0]
    per = e // N_GROUPS
    x = x_ref[...]
    x_top = lax.bitcast_convert_type(lax.bitcast_convert_type(x, U32) & jnp.uint32(HI_MASK), F32)
    x_hi = x_top.astype(BF16)
    x_lo = (x - x_top).astype(BF16)
    logits = (jnp.dot(x_hi, rwh_ref[...], preferred_element_type=F32)
              + (jnp.dot(x_lo, rwh_ref[...], preferred_element_type=F32)
                 + (jnp.dot(x_hi, rwl_ref[...], preferred_element_type=F32)
                    + jnp.dot(x_lo, rwl_ref[...], preferred_element_type=F32))))
    scores = jax.nn.sigmoid(logits.T[:e, :])
    choice = scores + bias_ref[...]
    neg = jnp.float32(-jnp.inf)

    ip = lax.broadcasted_iota(I32, (per, tm), 0)
    gs = []
    for g in range(N_GROUPS):
        c = choice[g * per:(g + 1) * per, :]
        m1, f1 = _first_max(c, ip, per)
        m2 = jnp.max(jnp.where(ip == f1, neg, c), axis=0, keepdims=True)
        gs.append(m1 + m2)
    gscore = jnp.concatenate(gs, axis=0)

    ig = lax.broadcasted_iota(I32, (N_GROUPS, tm), 0)
    sel = jnp.zeros((N_GROUPS, tm), F32)
    for _ in range(TOPK_GROUPS):
        _, fg = _first_max(gscore, ig, N_GROUPS)
        hit = ig == fg
        sel = jnp.where(hit, 1.0, sel)
        gscore = jnp.where(hit, neg, gscore)

    masked = jnp.concatenate(
        [jnp.where(sel[g:g + 1, :] > 0.0, choice[g * per:(g + 1) * per, :], neg) for g in range(N_GROUPS)],
        axis=0)
    ie = lax.broadcasted_iota(I32, (e, tm), 0)
    hits, ws = [], []
    for _ in range(TOP_K):
        _, fe = _first_max(masked, ie, e)
        hit = ie == fe
        hits.append(hit)
        ws.append(jnp.sum(jnp.where(hit, scores, 0.0), axis=0, keepdims=True))
        masked = jnp.where(hit, neg, masked)
    w = jnp.concatenate(ws, axis=0)
    wts_ref[...] = w / jnp.sum(w, axis=0, keepdims=True) * ROUTED_SCALE

    earlier_tok = (lax.broadcasted_iota(I32, (tt, tt), 0) < lax.broadcasted_iota(I32, (tt, tt), 1))
    earlier_tok = jnp.where(earlier_tok, 1.0, 0.0).astype(BF16)
    earlier_exp = (lax.broadcasted_iota(I32, (e, e), 1) < lax.broadcasted_iota(I32, (e, e), 0))
    earlier_exp = jnp.where(earlier_exp, 1.0, 0.0).astype(BF16)
    lane = lax.broadcasted_iota(I32, (e, LANES), 1)
    run_out = jnp.zeros((e, LANES), I32)
    lpos = []
    for s in range(tm // tt):
        sl = slice(s * tt, (s + 1) * tt)
        chosen = jnp.zeros((e, tt), F32)
        for hit in hits:
            chosen = jnp.where(hit[:, sl], 1.0, chosen)
        rank = jnp.dot(chosen.astype(BF16), earlier_tok, preferred_element_type=F32)
        cnt = (rank[:, tt - 1:tt] + chosen[:, tt - 1:tt]).astype(I32)
        shift = SUBLANES.bit_length() - 1
        run = jnp.left_shift(jnp.right_shift(cnt + (SUBLANES - 1), shift), shift)
        run_b = jnp.broadcast_to(run.astype(F32), (e, LANES)).astype(BF16)
        off = jnp.dot(earlier_exp, run_b, preferred_element_type=F32)[:, 0:1]
        pos = (off + rank).astype(I32)
        lpos.append(jnp.concatenate(
            [jnp.sum(jnp.where(hit[:, sl], pos, 0), axis=0, keepdims=True) for hit in hits], axis=0))
        run_out = jnp.where(lane == s, run, run_out)
    lpos_ref[...] = jnp.concatenate(lpos, axis=1)
    run_ref[...] = run_out


def _router(x2d, router_w, router_bias, tt, tm=512):
    t, d = x2d.shape
    e = router_w.shape[1]
    assert e % N_GROUPS == 0 and (e // N_GROUPS) % SUBLANES == 0
    tm = max(_tile(t, tm), tt)
    assert tm % tt == 0 and tm // tt <= LANES and tt <= 256
    ep = -(-e // LANES) * LANES
    rw_pad = jnp.zeros((d, ep), F32).at[:, :e].set(router_w.astype(F32))
    rw_top = lax.bitcast_convert_type(lax.bitcast_convert_type(rw_pad, U32) & jnp.uint32(HI_MASK), F32)
    rw_hi = rw_top.astype(BF16)
    rw_lo = (rw_pad - rw_top).astype(BF16)
    wts_t, lpos_t, run_tab = pl.pallas_call(
        functools.partial(_router_kernel, tt=tt),
        grid=(t // tm,),
        in_specs=[pl.BlockSpec((tm, d), lambda i: (i, 0)),
                  pl.BlockSpec((d, ep), lambda i: (0, 0)),
                  pl.BlockSpec((d, ep), lambda i: (0, 0)),
                  pl.BlockSpec((e, 1), lambda i: (0, 0))],
        out_specs=[pl.BlockSpec((TOP_K, tm), lambda i: (0, i)),
                   pl.BlockSpec((TOP_K, tm), lambda i: (0, i)),
                   pl.BlockSpec((None, e, LANES), lambda i: (i, 0, 0))],
        out_shape=[jax.ShapeDtypeStruct((TOP_K, t), F32), jax.ShapeDtypeStruct((TOP_K, t), I32),
                   jax.ShapeDtypeStruct((t // tm, e, LANES), I32)],
        compiler_params=_cparams("arbitrary"),
    )(x2d, rw_hi, rw_lo, router_bias.reshape(e, 1).astype(F32))
    run = run_tab[:, :, :tm // tt].transpose(0, 2, 1).reshape(t // tt, e)
    return wts_t, lpos_t, run


def _seg_copy(src, dst, src_off, dst_off, length, max_len, sem, wait=False):
    def piece(b):
        done = length & (-2 * b)

        @pl.when((length & b) != 0)
        def _():
            cp = pltpu.make_async_copy(src.at[pl.ds(pl.multiple_of(src_off + done, SUBLANES), b)],
                                       dst.at[pl.ds(pl.multiple_of(dst_off + done, SUBLANES), b)], sem)
            if wait:
                cp.wait()
            else:
                cp.start()

    b = max_len
    while b >= SUBLANES:
        piece(b)
        b //= 2


def _pow2_floor(n):
    return 1 << (n.bit_length() - 1)


def _dispatch_kernel(tab_ref, pad_ref, lpt_ref, x_ref, o_hbm, xs, zbuf, sem, *, n_exp, rc):
    i = pl.program_id(0)
    n = pl.num_programs(0)
    slot = i % 2
    tt = x_ref.shape[0]
    r_tot, half = xs.shape[1], xs.shape[2]

    def wait_slot(s, step):
        _seg_copy(xs.at[s], o_hbm, 0, 0, pad_ref[2 * n_exp + step], _pow2_floor(r_tot), sem.at[s], wait=True)

    @pl.when(i >= 2)
    def _():
        wait_slot(slot, i - 2)

    xlo, xhi = x_ref[:, :half], x_ref[:, half:]
    lpt = lpt_ref[...]
    for c in range(r_tot // rc):
        rows = lax.broadcasted_iota(I32, (rc, tt), 0) + c * rc
        hit = rows == lpt[0:1, :]
        for k in range(1, TOP_K):
            hit = hit | (rows == lpt[k:k + 1, :])
        g = jnp.where(hit, 1.0, 0.0).astype(BF16)
        lo = jnp.dot(g, xlo, preferred_element_type=F32)
        hi = jnp.dot(g, xhi, preferred_element_type=F32)
        xs[slot, c * rc:(c + 1) * rc, :] = _pack_halves(lo, hi)

    for e in range(n_exp):
        _seg_copy(xs.at[slot], o_hbm, tab_ref[2 * n_exp + e], tab_ref[e], tab_ref[n_exp + e], tt, sem.at[slot])

    @pl.when(i == 0)
    def _():
        zbuf[...] = jnp.zeros_like(zbuf)
        for w in (False, True):
            for e in range(n_exp):
                _seg_copy(zbuf, o_hbm, 0, pad_ref[e], pad_ref[n_exp + e], zbuf.shape[0] // 2, sem.at[2], wait=w)

    @pl.when(i == n - 1)
    def _():
        @pl.when(n >= 2)
        def _():
            wait_slot(1 - slot, i - 1)
        wait_slot(slot, i)


def _tile_rows(tt, n_exp, rc=512):
    r = tt * TOP_K + (SUBLANES - 1) * n_exp
    rc = min(rc, tt * TOP_K)
    return -(-r // rc) * rc, rc


def _dispatch(xbf, tab, pad_tab, lpos_t, m_pad, n_exp, tt, bm):
    t, d = xbf.shape
    assert tt & (tt - 1) == 0 and bm & (bm - 1) == 0 and d % 2 == 0 and tt % SUBLANES == 0
    r_tot, rc = _tile_rows(tt, n_exp)
    return pl.pallas_call(
        functools.partial(_dispatch_kernel, n_exp=n_exp, rc=rc),
        grid=(t // tt,),
        in_specs=[pl.BlockSpec((4 * n_exp,), lambda i: (i,), memory_space=pltpu.SMEM),
                  pl.BlockSpec(memory_space=pltpu.SMEM),
                  pl.BlockSpec((TOP_K, tt), lambda i: (0, i)),
                  pl.BlockSpec((tt, d), lambda i: (i, 0))],
        out_specs=pl.BlockSpec(memory_space=pl.ANY),
        out_shape=jax.ShapeDtypeStruct((m_pad, d // 2), U32),
        scratch_shapes=[pltpu.VMEM((2, r_tot, d // 2), U32), pltpu.VMEM((bm, d // 2), U32),
                        pltpu.SemaphoreType.DMA((3,))],
        compiler_params=_cparams("arbitrary"),
    )(tab, pad_tab, lpos_t, xbf)


def _expert_kernel(be_ref, nv_ref, br_ref, nx_ref, od_ref, xs_ref, wgu_hbm, wd_hbm, y_ref,
                   wgu_f, wd_f, wgu_bf, wd_bf, sem, *, cast_rows, layer):
    i = pl.program_id(0)

    def weight_copies(expert, slot):
        return (pltpu.make_async_copy(wgu_hbm.at[layer, expert], wgu_f.at[slot], sem.at[0, slot]),
                pltpu.make_async_copy(wd_hbm.at[layer, expert], wd_f.at[slot], sem.at[1, slot]))

    @pl.when(i < nv_ref[0])
    def _():
        expert = be_ref[i]

        @pl.when((i == 0) | (expert != be_ref[jnp.maximum(i - 1, 0)]))
        def _():
            slot = od_ref[i] % 2

            @pl.when(i == 0)
            def _():
                for cp in weight_copies(expert, slot):
                    cp.start()
            for cp in weight_copies(expert, slot):
                cp.wait()

            @pl.when(nx_ref[i] >= 0)
            def _():
                for cp in weight_copies(nx_ref[i], 1 - slot):
                    cp.start()

            def cast_gu(c, carry):
                r = pl.ds(pl.multiple_of(c * cast_rows, cast_rows), cast_rows)
                wgu_bf[r, :] = wgu_f[slot, r, :].astype(BF16)
                return carry
            lax.fori_loop(0, wgu_bf.shape[0] // cast_rows, cast_gu, 0)
            wd_bf[...] = wd_f[slot].astype(BF16)

        half = xs_ref.shape[1]
        eh = wd_bf.shape[0]
        bm = xs_ref.shape[0]

        def mlp(rows):
            lo, hi = _unpack_rows(xs_ref[:rows, :])
            gu = (jnp.dot(lo, wgu_bf[:half, :], preferred_element_type=F32)
                  + jnp.dot(hi, wgu_bf[half:, :], preferred_element_type=F32))
            a = (_silu(gu[:, :eh]) * gu[:, eh:]).astype(BF16)
            y_ref[:rows, :] = _pack_rows(jnp.dot(a, wd_bf[...], preferred_element_type=F32))

        quarter = bm // 4
        for q in range(1, 5):
            @pl.when((br_ref[i] > (q - 1) * quarter) & ((br_ref[i] <= q * quarter) | (q == 4)))
            def _(q=q):
                mlp(q * quarter)
                if q < 4:
                    y_ref[q * quarter:, :] = jnp.zeros((bm - q * quarter, half), U32)


def _experts(xs, block_e, nvalid, block_rows, next_e, order, w_gu, w_down, layer, bm):
    m_pad, dh = xs.shape
    _, e, d, eh2 = w_gu.shape
    eh = w_down.shape[2]
    nb = m_pad // bm
    cast_rows = _tile(d, 256)

    def blk(i, be, nv, *_):
        return (jnp.minimum(i, nv[0] - 1), 0)

    grid_spec = pltpu.PrefetchScalarGridSpec(
        num_scalar_prefetch=5,
        grid=(nb,),
        in_specs=[pl.BlockSpec((bm, dh), blk),
                  pl.BlockSpec(memory_space=pl.ANY),
                  pl.BlockSpec(memory_space=pl.ANY)],
        out_specs=pl.BlockSpec((bm, dh), blk),
        scratch_shapes=[pltpu.VMEM((2, d, eh2), F32), pltpu.VMEM((2, eh, d), F32),
                        pltpu.VMEM((d, eh2), BF16), pltpu.VMEM((eh, d), BF16),
                        pltpu.SemaphoreType.DMA((2, 2))],
    )
    return pl.pallas_call(
        functools.partial(_expert_kernel, cast_rows=cast_rows, layer=layer),
        grid_spec=grid_spec,
        out_shape=jax.ShapeDtypeStruct((m_pad, dh), U32),
        compiler_params=_cparams("arbitrary"),
    )(block_e, nvalid, block_rows, next_e, order, xs, w_gu.astype(F32), w_down.astype(F32))


def _combine_kernel(tabc_ref, tabn_ref, lp_ref, w_ref, x_ref, sgu_ref, sd_ref, g_ref, b_ref, y_hbm,
                    o_ref, ybuf, sem, *, alpha, n_exp):
    i = pl.program_id(0)
    n = pl.num_programs(0)
    tt = x_ref.shape[0]
    r_tot = ybuf.shape[1]
    slot = i % 2

    def issue(tab_ref, s):
        for e in range(n_exp):
            _seg_copy(y_hbm, ybuf.at[s], tab_ref[e], tab_ref[2 * n_exp + e], tab_ref[n_exp + e], tt, sem.at[s])

    @pl.when(i == 0)
    def _():
        ybuf[...] = jnp.zeros_like(ybuf)
        issue(tabc_ref, 0)

    @pl.when(i + 1 < n)
    def _():
        issue(tabn_ref, 1 - slot)

    _seg_copy(y_hbm, ybuf.at[slot], 0, 0, tabc_ref[3 * n_exp], _pow2_floor(r_tot), sem.at[slot], wait=True)

    lp = lp_ref[...]
    w = w_ref[...]
    cols = lax.broadcasted_iota(I32, (tt, r_tot), 1)
    p = jnp.zeros((tt, r_tot), F32)
    for k in range(TOP_K):
        p = jnp.where(cols == lp[:, k:k + 1], w[:, k:k + 1], p)
    pb = p.astype(BF16)
    ylo, yhi = _unpack_rows(ybuf[slot])
    routed = jnp.concatenate([jnp.dot(pb, ylo, preferred_element_type=F32),
                              jnp.dot(pb, yhi, preferred_element_type=F32)], axis=1)

    x = x_ref[...]
    sh = sd_ref.shape[0]
    sg = jnp.dot(x.astype(BF16), sgu_ref[...], preferred_element_type=F32)
    a = (_silu(sg[:, :sh]) * sg[:, sh:]).astype(BF16)
    shared = jnp.dot(a, sd_ref[...], preferred_element_type=F32)
    o_ref[...] = _layer_norm(alpha * x + (routed + shared), g_ref[...], b_ref[...])


def _combine(x2d, wts, lpos, tab, y_sorted, s_gu_bf, s_down_bf, g, b, alpha, n_exp, tt):
    t, d = x2d.shape
    n = t // tt
    sh = s_down_bf.shape[0]
    return pl.pallas_call(
        functools.partial(_combine_kernel, alpha=alpha, n_exp=n_exp),
        grid=(n,),
        in_specs=[pl.BlockSpec((4 * n_exp,), lambda i: (i,), memory_space=pltpu.SMEM),
                  pl.BlockSpec((4 * n_exp,), lambda i: (jnp.minimum(i + 1, n - 1),), memory_space=pltpu.SMEM),
                  pl.BlockSpec((tt, TOP_K), lambda i: (i, 0)),
                  pl.BlockSpec((tt, TOP_K), lambda i: (i, 0)),
                  pl.BlockSpec((tt, d), lambda i: (i, 0)),
                  pl.BlockSpec((d, 2 * sh), lambda i: (0, 0)),
                  pl.BlockSpec((sh, d), lambda i: (0, 0)),
                  pl.BlockSpec((1, d), lambda i: (0, 0)),
                  pl.BlockSpec((1, d), lambda i: (0, 0)),
                  pl.BlockSpec(memory_space=pl.ANY)],
        out_specs=pl.BlockSpec((tt, d), lambda i: (i, 0)),
        out_shape=jax.ShapeDtypeStruct((t, d), F32),
        scratch_shapes=[pltpu.VMEM((2, _tile_rows(tt, n_exp)[0], d // 2), U32), pltpu.SemaphoreType.DMA((2,))],
        compiler_params=_cparams("arbitrary"),
    )(tab, tab, lpos, wts, x2d, s_gu_bf, s_down_bf,
      g.reshape(1, d).astype(F32), b.reshape(1, d).astype(F32), y_sorted)


def _moe_layer(x2d, xbf, router_w, router_bias, w_gu, w_down, layer, s_gu, s_down, g, b, alpha, bm=512, tt=256):
    t, d = x2d.shape
    e = router_w.shape[1]
    tt = _tile(t, tt)
    nt = t // tt
    wts_t, lpos_t, run = _router(x2d, router_w, router_bias, tt)

    before = jnp.cumsum(run, axis=0) - run
    off = jnp.cumsum(run, axis=1) - run
    counts = jnp.sum(run, axis=0)
    padded = ((counts + bm - 1) // bm) * bm
    ends = jnp.cumsum(padded)
    starts = ends - padded
    totals = jnp.sum(run, axis=1)
    spare = jnp.zeros_like(run).at[:, 0].set(totals)
    tab = jnp.concatenate([starts[None, :] + before, run, off, spare], axis=1).reshape(-1).astype(I32)
    pad_tab = jnp.concatenate([starts + counts, padded - counts, totals]).astype(I32)
    m_pad = -(-(t * TOP_K + (SUBLANES - 1) * nt * e) // bm) * bm + e * bm
    nb = m_pad // bm
    block_start = jnp.arange(nb, dtype=I32) * bm
    block_e = jnp.minimum(jnp.sum((ends[None, :] <= block_start[:, None]).astype(I32), axis=1), e - 1).astype(I32)
    nvalid = (ends[-1:] // bm).astype(I32)
    block_rows = jnp.sum(jnp.clip(jnp.minimum((starts + counts)[None, :], block_start[:, None] + bm)
                                  - jnp.maximum(starts[None, :], block_start[:, None]), 0, bm), axis=1).astype(I32)

    ar = jnp.arange(e, dtype=I32)
    has_rows = (padded > 0)[None, :]
    next_e = jnp.min(jnp.where((ar[None, :] > block_e[:, None]) & has_rows, ar[None, :], e), axis=1)
    next_e = jnp.where(next_e == e, -1, next_e).astype(I32)
    order = jnp.sum(((ar[None, :] < block_e[:, None]) & has_rows).astype(I32), axis=1).astype(I32)

    xs = _dispatch(xbf, tab, pad_tab, lpos_t, m_pad, e, tt, bm)
    y_sorted = _experts(xs, block_e, nvalid, block_rows, next_e, order, w_gu, w_down, layer, bm)
    return _combine(x2d, wts_t.T, lpos_t.T, tab, y_sorted, s_gu.astype(BF16), s_down.astype(BF16), g, b, alpha, e, tt)


def _proj_rope_kernel(x_ref, w_ref, cos_ref, sin_ref, o_ref):
    y = jnp.dot(x_ref[...].astype(BF16), w_ref[...], preferred_element_type=F32)
    tm, n = y.shape
    lane = lax.broadcasted_iota(I32, (tm, LANES), 1)
    first_half = (lane % HD_B) < (HD_B // 2)
    for c in range(n // LANES):
        yc = y[:, c * LANES:(c + 1) * LANES]
        rot = jnp.where(first_half, pltpu.roll(yc, LANES - HD_B // 2, axis=1), pltpu.roll(yc, HD_B // 2, axis=1))
        o_ref[:, c * LANES:(c + 1) * LANES] = (yc * cos_ref[...] + rot * sin_ref[...]).astype(o_ref.dtype)


def _proj_rope(x2d, w_bf, cos, sin, seq, tm=512, tn=1024):
    t, d = x2d.shape
    n = w_bf.shape[1]
    tm, tn = _tile(seq, tm), _tile(n, tn)
    assert tn % LANES == 0 and cos.shape[1] == LANES
    ns = seq // tm
    return pl.pallas_call(
        _proj_rope_kernel,
        grid=(n // tn, t // tm),
        in_specs=[pl.BlockSpec((tm, d), lambda j, i: (i, 0)),
                  pl.BlockSpec((d, tn), lambda j, i: (0, j)),
                  pl.BlockSpec((tm, LANES), lambda j, i: (i % ns, 0)),
                  pl.BlockSpec((tm, LANES), lambda j, i: (i % ns, 0))],
        out_specs=pl.BlockSpec((tm, tn), lambda j, i: (i, j)),
        out_shape=jax.ShapeDtypeStruct((t, n), BF16),
        compiler_params=_cparams("arbitrary", "arbitrary"),
    )(x2d, w_bf, cos, sin)


def _rope_tables(seq, scale):
    half = HD_B // 2
    f32 = np.float32
    inv = (f32(1.0) / np.power(f32(ROPE_THETA), np.arange(half, dtype=f32) / f32(half))).astype(f32)
    ang = (np.arange(seq, dtype=f32)[:, None] * inv[None, :]).astype(f32)
    cos, sin = np.cos(ang), np.sin(ang)
    reps = LANES // HD_B
    cos_t = np.tile(np.concatenate([cos, cos], axis=1), (1, reps)) * f32(scale)
    sin_t = np.tile(np.concatenate([-sin, sin], axis=1), (1, reps)) * f32(scale)
    return jnp.asarray(cos_t, F32), jnp.asarray(sin_t, F32)


def _swa_kernel(q_ref, kc_ref, kp_ref, vc_ref, vp_ref, sink_ref, o_ref, *, groups):
    qi = pl.program_id(2)
    tq = q_ref.shape[0]
    w = WINDOW
    row = lax.broadcasted_iota(I32, (w, 2 * w), 0)
    col = lax.broadcasted_iota(I32, (w, 2 * w), 1)
    band = (col > row) & (col <= row + w)
    neg = jnp.float32(-jnp.inf)
    sink = jnp.stack([sink_ref[g:g + 1, 0:1] for g in range(groups)])
    ones = jnp.ones((2 * w, LANES), BF16)
    for j in range(tq // w):
        if j == 0:
            keys = [jnp.concatenate([kp_ref[s], kc_ref[s, :w, :]], axis=0) for s in range(2)]
            vals = [jnp.concatenate([vp_ref[s], vc_ref[s, :w, :]], axis=0) for s in range(2)]
            mask = band & ((col >= w) | (qi > 0))
        else:
            keys = [kc_ref[s, (j - 1) * w:(j + 1) * w, :] for s in range(2)]
            vals = [vc_ref[s, (j - 1) * w:(j + 1) * w, :] for s in range(2)]
            mask = band
        scs = []
        for p in range(groups // 2):
            qs = q_ref[j * w:(j + 1) * w, p * LANES:(p + 1) * LANES]
            for s in range(2):
                scs.append(lax.dot_general(qs, keys[s], (((1,), (1,)), ((), ())), preferred_element_type=F32))
        sc = jnp.where(mask[None], jnp.stack(scs), neg)
        m = jnp.maximum(jnp.max(jnp.maximum(sc[..., :w], sc[..., w:]), axis=-1, keepdims=True), sink)
        m_b = jnp.broadcast_to(m, (groups, w, LANES))
        pr = jnp.concatenate([jnp.exp(sc[..., :w] - m_b), jnp.exp(sc[..., w:] - m_b)], axis=-1).astype(BF16)
        sink_term = jnp.exp(sink - m_b)
        for p in range(groups // 2):
            out = jnp.zeros((w, LANES), F32)
            for s in range(2):
                g = 2 * p + s
                den = jnp.dot(pr[g], ones, preferred_element_type=F32) + sink_term[g]
                out = out + jnp.dot(pr[g], vals[s], preferred_element_type=F32) * (1.0 / den)
            o_ref[j * w:(j + 1) * w, p * LANES:(p + 1) * LANES] = out.astype(o_ref.dtype)


def _swa_attn(q, kk, vv, sink_tab, batch, seq, kvh, groups, tq=512):
    t = q.shape[0]
    tq = _tile(seq, tq)
    assert tq % WINDOW == 0 and groups % 2 == 0 and 2 * HD_B == LANES
    nq = seq // tq
    per = tq // WINDOW
    gw = groups * HD_B
    cur = pl.BlockSpec((None, None, 2, tq, LANES), lambda b, h, i: (b, h, 0, i, 0))
    prev = pl.BlockSpec((None, None, 2, WINDOW, LANES), lambda b, h, i: (b, h, 0, jnp.maximum(i * per - 1, 0), 0))
    return pl.pallas_call(
        functools.partial(_swa_kernel, groups=groups),
        grid=(batch, kvh, nq),
        in_specs=[pl.BlockSpec((tq, gw), lambda b, h, i: (b * nq + i, h)),
                  cur, prev, cur, prev,
                  pl.BlockSpec((None, groups, LANES), lambda b, h, i: (h, 0, 0))],
        out_specs=pl.BlockSpec((tq, gw), lambda b, h, i: (b * nq + i, h)),
        out_shape=jax.ShapeDtypeStruct((t, kvh * gw), BF16),
        compiler_params=_cparams("arbitrary", "arbitrary", "arbitrary"),
    )(q, kk, kk, vv, vv, sink_tab)


def _kv_proj_kernel(x_ref, w_ref, cos_ref, sin_ref, kk_ref, vv_ref, *, kvh):
    y = jnp.dot(x_ref[...].astype(BF16), w_ref[...], preferred_element_type=F32)
    tm = y.shape[0]
    lane = lax.broadcasted_iota(I32, (tm, LANES), 1)
    first_half = (lane % HD_B) < (HD_B // 2)
    low = lane < HD_B
    for part, o_ref in enumerate((kk_ref, vv_ref)):
        for c in range(kvh // 2):
            col = part * kvh * HD_B + c * LANES
            yc = y[:, col:col + LANES]
            if part == 0:
                rot = jnp.where(first_half, pltpu.roll(yc, LANES - HD_B // 2, axis=1),
                                pltpu.roll(yc, HD_B // 2, axis=1))
                yc = yc * cos_ref[...] + rot * sin_ref[...]
            sw = pltpu.roll(yc, HD_B, axis=1)
            o_ref[2 * c, 0] = jnp.where(low, yc, 0.0).astype(o_ref.dtype)
            o_ref[2 * c, 1] = jnp.where(low, 0.0, sw).astype(o_ref.dtype)
            o_ref[2 * c + 1, 0] = jnp.where(low, sw, 0.0).astype(o_ref.dtype)
            o_ref[2 * c + 1, 1] = jnp.where(low, 0.0, yc).astype(o_ref.dtype)


def _kv_proj(x2d, w_bf, cos, sin, batch, seq, kvh, tm=512):
    t, d = x2d.shape
    n = w_bf.shape[1]
    tm = _tile(seq, tm)
    ns = seq // tm
    assert kvh % 2 == 0 and n == 2 * kvh * HD_B and cos.shape[1] == LANES
    out = jax.ShapeDtypeStruct((batch, kvh, 2, seq, LANES), BF16)
    out_spec = pl.BlockSpec((None, kvh, 2, tm, LANES), lambda i: (i // ns, 0, 0, i % ns, 0))
    return pl.pallas_call(
        functools.partial(_kv_proj_kernel, kvh=kvh),
        grid=(t // tm,),
        in_specs=[pl.BlockSpec((tm, d), lambda i: (i, 0)),
                  pl.BlockSpec((d, n), lambda i: (0, 0)),
                  pl.BlockSpec((tm, LANES), lambda i: (i % ns, 0)),
                  pl.BlockSpec((tm, LANES), lambda i: (i % ns, 0))],
        out_specs=[out_spec, out_spec],
        out_shape=[out, out],
        compiler_params=_cparams("arbitrary"),
    )(x2d, w_bf, cos, sin)


def kernel(x, a_w_in, a_b_f, a_w_o, kv_w, b_w_q, b_sinks, b_w_o, router_w, router_bias, moe_w_gu, moe_w_down,
           shared_w_gu, shared_w_down, ln1_g, ln1_b, ln2_g, ln2_b):
    batch, seq, d = x.shape
    n_a, n_b = a_w_in.shape[0], b_w_q.shape[0]
    depth = n_a + n_b
    alpha = float((2 * depth) ** 0.25)
    heads_a = a_b_f.shape[1]
    hd_a = d // heads_a
    heads_b = b_sinks.shape[1]
    kvh = kv_w.shape[1] // (2 * HD_B)
    groups = heads_b // kvh
    t = batch * seq

    x2d = x.reshape(t, d).astype(F32)
    kk = vv = None
    cos_q, sin_q = _rope_tables(seq, 1.0 / math.sqrt(HD_B))
    cos_k, sin_k = _rope_tables(seq, 1.0)
    for i in range(depth):
        if i < n_a:
            qkv = _proj(x2d, a_w_in, i, 3 * d, scaled_cols=d, scale=LOG2E / math.sqrt(hd_a))
            cum = _fox_cum(x2d, a_w_in, i, 3 * d, a_b_f[i], batch, seq)
            tb = _tile(seq, 512)
            cum_t = cum[:, :heads_a, :].reshape(batch, heads_a, seq // tb, 1, tb)
            attn = _fox_attn(qkv, cum_t, batch, seq, heads_a, hd_a, tb=tb)
            w_o = a_w_o[i]
        else:
            j = i - n_a
            q = _proj_rope(x2d, b_w_q[j].astype(BF16), cos_q, sin_q, seq)
            sink_tab = jnp.broadcast_to(b_sinks[j].astype(F32).reshape(kvh, groups, 1), (kvh, groups, LANES))
            attn = _swa_attn(q, kk, vv, sink_tab, batch, seq, kvh, groups)
            w_o = b_w_o[j]
        x2d, xbf = _oproj_ln(attn, w_o.astype(BF16), x2d, ln1_g[i], ln1_b[i], alpha)
        x2d = _moe_layer(x2d, xbf, router_w[i], router_bias[i], moe_w_gu, moe_w_down, i,
                         shared_w_gu[i], shared_w_down[i], ln2_g[i], ln2_b[i], alpha)
        if i == n_a - 1:
            kk, vv = _kv_proj(x2d, kv_w.astype(BF16), cos_k, sin_k, batch, seq, kvh)
    return x2d.reshape(batch, seq, d).astype(x.dtype)
```
